```python
import math
import jax, jax.numpy as jnp
from jax import lax
import numpy as np

D_MODEL = 2048
BATCH = 2
SEQ = 8192
DEPTH = 2
DEC_BATCH = 4
DEC_SEQ = 2048
PAST_LEN = 128

GRID_W = 64
N_BRANCH = 4
BRANCH_W = D_MODEL // N_BRANCH
EPS = 1e-6
A_HEADS = 4
A_DH = BRANCH_W // A_HEADS
A_GATES = 4 * A_HEADS
CHUNK = 128
M_INIT = -1e30
B_HEADS = 4
B_DH = BRANCH_W // B_HEADS
WIN_ROWS = 8
WIN_COLS = 16
QCOL_BLK = 16
KCOL_BLK = 32
C_HEADS = 8
C_KV_HEADS = 2
C_DH = BRANCH_W // C_HEADS
C_WINDOW = 128
C_BLOCK = 128
N_BUCKETS = 32
MAX_DIST = 128
CONV_W = 31
IN_SIZES = (BRANCH_W, BRANCH_W, BRANCH_W, BRANCH_W, BRANCH_W, A_GATES,
            BRANCH_W, BRANCH_W, BRANCH_W, BRANCH_W,
            BRANCH_W, C_KV_HEADS * C_DH, C_KV_HEADS * C_DH, BRANCH_W,
            2 * BRANCH_W, BRANCH_W,
            N_BRANCH * D_MODEL)
IN_COLS = sum(IN_SIZES)

kernel_name = 'hybrid_bidir_encoder_parallel_gated'


def rms_norm(x, w):
    xf = x.astype(jnp.float32)
    y = xf * lax.rsqrt(jnp.mean(xf * xf, -1, keepdims=True) + EPS) * w.astype(jnp.float32)
    return y.astype(x.dtype)


def head_rms(t, w):
    tf = t.astype(jnp.float32)
    y = tf * lax.rsqrt(jnp.mean(tf * tf, -1, keepdims=True) + EPS) * w.astype(jnp.float32)
    return y.astype(t.dtype)


def mlstm_chunkwise(q, k, v, ig, fg):
    bs, nh, t, d = q.shape
    nc = t // CHUNK
    q = q.reshape(bs, nh, nc, CHUNK, d)
    k = k.reshape(bs, nh, nc, CHUNK, d)
    v = v.reshape(bs, nh, nc, CHUNK, d)
    ig = ig.reshape(bs, nh, nc, CHUNK)
    b = jnp.cumsum(jax.nn.log_sigmoid(fg).reshape(bs, nh, nc, CHUNK), axis=-1)
    g = b[..., -1]
    w_end = g[..., None] - b + ig
    m_loc = jnp.max(w_end, axis=-1)
    a_end = jnp.exp(w_end - m_loc[..., None])
    c_loc = jnp.einsum('bhcs,bhcsd,bhcse->bhcde', a_end, k, v)
    n_loc = jnp.einsum('bhcs,bhcsd->bhcd', a_end, k)

    def step(carry, inp):
        c_st, n_st, m_st = carry
        g_c, m_l, c_l, n_l = inp
        m_new = jnp.maximum(g_c + m_st, m_l)
        a_old = jnp.exp(g_c + m_st - m_new)
        a_new = jnp.exp(m_l - m_new)
        c_new = a_old[..., None, None] * c_st + a_new[..., None, None] * c_l
        n_new = a_old[..., None] * n_st + a_new[..., None] * n_l
        return (c_new, n_new, m_new), (c_st, n_st, m_st)

    init = (jnp.zeros((bs, nh, d, d), jnp.float32), jnp.zeros((bs, nh, d), jnp.float32),
            jnp.full((bs, nh), M_INIT, jnp.float32))
    mv = lambda u: jnp.moveaxis(u, 2, 0)
    _, (c_in, n_in, m_in) = lax.scan(step, init, (mv(g), mv(m_loc), mv(c_loc), mv(n_loc)))
    c_in = jnp.moveaxis(c_in, 0, 2)
    n_in = jnp.moveaxis(n_in, 0, 2)
    m_in = jnp.moveaxis(m_in, 0, 2)
    dmat = b[..., :, None] - b[..., None, :] + ig[..., None, :]
    causal = jnp.tril(jnp.ones((CHUNK, CHUNK), bool))
    dmat = jnp.where(causal, dmat, -jnp.inf)
    inter = b + m_in[..., None]
    m_t = jnp.maximum(jnp.max(dmat, axis=-1), inter)
    s = jnp.einsum('bhctd,bhcsd->bhcts', q, k) * jnp.exp(dmat - m_t[..., None])
    a_in = jnp.exp(inter - m_t)
    num = jnp.einsum('bhcts,bhcse->bhcte', s, v) + a_in[..., None] * jnp.einsum('bhctd,bhcde->bhcte', q, c_in)
    den = jnp.sum(s, axis=-1) + a_in * jnp.einsum('bhctd,bhcd->bhct', q, n_in)
    h = num / jnp.maximum(jnp.abs(den), jnp.exp(-m_t))[..., None]
    return h.reshape(bs, nh, t, d)


def mlstm_branch(aq, ak, av, ao, agates, b_gate, norm_w):
    bs, t, _ = aq.shape
    f32 = jnp.float32
    heads = lambda u: u.astype(f32).reshape(bs, t, A_HEADS, A_DH).transpose(0, 2, 1, 3)
    q, k, v = heads(aq), heads(ak) * (A_DH ** -0.5), heads(av)
    gt = (agates.astype(f32) + b_gate.astype(f32)).transpose(0, 2, 1)
    i_fw, i_bw = gt[:, 0:A_HEADS], gt[:, A_HEADS:2 * A_HEADS]
    f_fw, f_bw = gt[:, 2 * A_HEADS:3 * A_HEADS], gt[:, 3 * A_HEADS:]
    h_fw = mlstm_chunkwise(q, k, v, i_fw, f_fw)
    rev = lambda u: jnp.flip(u, axis=2)
    h_bw = rev(mlstm_chunkwise(rev(q), rev(k), rev(v), rev(i_bw), rev(f_bw)))
    h = (h_fw + h_bw).transpose(0, 2, 1, 3)
    h = h * lax.rsqrt(jnp.mean(h * h, -1, keepdims=True) + EPS) * norm_w.astype(f32).reshape(A_HEADS, A_DH)
    return (h.reshape(bs, t, BRANCH_W) * jax.nn.sigmoid(ao.astype(f32))).astype(aq.dtype)


def neighbourhood_attention(bq, bk, bv, q_norm_w, k_norm_w, rpb):
    bs, t, _ = bq.shape
    rows = t // GRID_W
    kh = min(WIN_ROWS, rows)
    ncb = GRID_W // QCOL_BLK
    grid = lambda u: u.reshape(bs, rows, GRID_W, B_HEADS, B_DH)
    q = (head_rms(grid(bq), q_norm_w) * (B_DH ** -0.5)).reshape(bs, rows, ncb, QCOL_BLK, B_HEADS, B_DH)
    k = head_rms(grid(bk), k_norm_w)
    v = grid(bv)
    r = jnp.arange(rows)
    row_idx = jnp.clip(r - kh // 2, 0, rows - kh)[:, None] + jnp.arange(kh)[None, :]
    cb = jnp.arange(ncb)
    col_idx = jnp.clip(cb * QCOL_BLK - WIN_COLS // 2, 0, GRID_W - KCOL_BLK)[:, None] + jnp.arange(KCOL_BLK)[None, :]
    ri, ci = row_idx[:, None, :, None], col_idx[None, :, None, :]
    kg = k[:, ri, ci]
    vg = v[:, ri, ci]
    s = jnp.einsum('brcqhd,brcjkhd->bhrcqjk', q, kg).astype(jnp.float32)
    qc = cb[:, None] * QCOL_BLK + jnp.arange(QCOL_BLK)[None, :]
    qs = jnp.clip(qc - WIN_COLS // 2, 0, GRID_W - WIN_COLS)
    kc = col_idx[:, None, :]
    col_ok = (kc >= qs[..., None]) & (kc < qs[..., None] + WIN_COLS)
    dr = row_idx - r[:, None] + WIN_ROWS - 1
    dc = jnp.clip(kc - qc[..., None] + WIN_COLS - 1, 0, 2 * WIN_COLS - 2)
    bias = rpb.astype(jnp.float32)[:, dr[:, None, None, :, None], dc[None, :, :, None, :]]
    s = jnp.where(col_ok[:, :, None, :], s + bias[None], -jnp.inf)
    p = jax.nn.softmax(s.reshape(s.shape[:-2] + (kh * KCOL_BLK,)), axis=-1).reshape(s.shape)
    o = jnp.einsum('bhrcqjk,brcjkhd->brcqhd', p.astype(vg.dtype), vg)
    return o.reshape(bs, t, BRANCH_W)


def t5_bucket(rel):
    half = N_BUCKETS // 2
    max_exact = half // 2
    n = jnp.abs(rel)
    nf = jnp.maximum(n, 1).astype(jnp.float32)
    large = max_exact + (jnp.log(nf / max_exact) / math.log(MAX_DIST / max_exact) * (half - max_exact)).astype(jnp.int32)
    large = jnp.minimum(large, half - 1)
    return jnp.where(rel > 0, half, 0) + jnp.where(n < max_exact, n, large)


def window_gqa(cq, ck, cv, q_norm_w, k_norm_w, sink, rel_bias):
    bs, t, _ = cq.shape
    nb = t // C_BLOCK
    grp = C_HEADS // C_KV_HEADS
    q = head_rms(cq.reshape(bs, t, C_HEADS, C_DH), q_norm_w) * (C_DH ** -0.5)
    q = q.reshape(bs, nb, C_BLOCK, C_KV_HEADS, grp, C_DH)
    k = head_rms(ck.reshape(bs, t, C_KV_HEADS, C_DH), k_norm_w)
    v = cv.reshape(bs, t, C_KV_HEADS, C_DH)

    def band(u):
        up = jnp.pad(u, ((0, 0), (C_BLOCK, C_BLOCK), (0, 0), (0, 0))).reshape(bs, nb + 2, C_BLOCK, C_KV_HEADS, C_DH)
        return jnp.concatenate([up[:, :-2], up[:, 1:-1], up[:, 2:]], axis=2)

    kb, vb = band(k), band(v)
    s = jnp.einsum('bnqhgd,bnkhd->bhgnqk', q, kb).astype(jnp.float32)
    qi = jnp.arange(C_BLOCK)
    ki = jnp.arange(3 * C_BLOCK)
    rel = ki[None, :] - C_BLOCK - qi[:, None]
    bias = rel_bias.astype(jnp.float32)[t5_bucket(rel)].transpose(2, 0, 1).reshape(C_KV_HEADS, grp, C_BLOCK, 3 * C_BLOCK)
    kpos = jnp.arange(nb)[:, None, None] * C_BLOCK - C_BLOCK + ki[None, None, :]
    ok = (jnp.abs(rel)[None] <= C_WINDOW) & (kpos >= 0) & (kpos < t)
    s = jnp.where(ok, s + bias[None, :, :, None], -jnp.inf)
    sk = sink.astype(jnp.float32).reshape(1, C_KV_HEADS, grp, 1, 1, 1)
    m = jnp.maximum(jnp.max(s, axis=-1, keepdims=True), sk)
    p = jnp.exp(s - m)
    p = p / (jnp.sum(p, axis=-1, keepdims=True) + jnp.exp(sk - m))
    o = jnp.einsum('bhgnqk,bnkhd->bnqhgd', p.astype(vb.dtype), vb)
    return o.reshape(bs, t, BRANCH_W)


def conformer_conv(dglu, conv_w, conv_b, ln_w, ln_b):
    a, g = jnp.split(dglu, 2, axis=-1)
    u = a * jax.nn.sigmoid(g)
    u = lax.conv_general_dilated(u, conv_w[:, None, :].astype(u.dtype), window_strides=(1,),
                                 padding=[(CONV_W // 2, CONV_W // 2)],
                                 dimension_numbers=('NWC', 'WIO', 'NWC'),
                                 feature_group_count=BRANCH_W) + conv_b
    uf = u.astype(jnp.float32)
    mu = jnp.mean(uf, -1, keepdims=True)
    var = jnp.mean(jnp.square(uf - mu), -1, keepdims=True)
    uf = (uf - mu) * lax.rsqrt(var + EPS) * ln_w.astype(jnp.float32) + ln_b.astype(jnp.float32)
    return jax.nn.silu(uf).astype(dglu.dtype)


def mixer_layer(x, c, rel_bias, norm_w, w_ada, b_ada, w_in, b_gate, mlstm_norm_w, na_q_norm, na_k_norm,
                na_rpb, swa_q_norm, swa_k_norm, swa_sink, conv_w, conv_b, conv_ln_w, conv_ln_b, w_branch, w_out):
    mod = jax.nn.silu(c) @ w_ada + b_ada
    shift, scale, gate = jnp.split(mod, 3, axis=-1)
    h = rms_norm(x, norm_w) * (1 + scale[:, None, :]) + shift[:, None, :]
    pts = []
    acc = 0
    for sz in IN_SIZES[:-1]:
        acc += sz
        pts.append(acc)
    ws = jnp.split(w_in, pts, axis=1)
    (aq, ak, av, ao, az, ag, bq, bk, bv, bz, cq, ck, cv, cz, dglu, dz) = [h @ w for w in ws[:-1]]
    w_merge = ws[-1].reshape(D_MODEL, N_BRANCH, D_MODEL)
    y_a = mlstm_branch(aq, ak, av, ao, ag, b_gate, mlstm_norm_w) * jax.nn.silu(az)
    y_b = neighbourhood_attention(bq, bk, bv, na_q_norm, na_k_norm, na_rpb) * jax.nn.silu(bz)
    y_c = window_gqa(cq, ck, cv, swa_q_norm, swa_k_norm, swa_sink, rel_bias) * jax.nn.silu(cz)
    y_d = conformer_conv(dglu, conv_w, conv_b, conv_ln_w, conv_ln_b) * jax.nn.silu(dz)
    merged = jax.nn.sigmoid(h @ w_merge[:, 0]) * (y_a @ w_branch[0])
    merged = merged + jax.nn.sigmoid(h @ w_merge[:, 1]) * (y_b @ w_branch[1])
    merged = merged + jax.nn.sigmoid(h @ w_merge[:, 2]) * (y_c @ w_branch[2])
    merged = merged + jax.nn.sigmoid(h @ w_merge[:, 3]) * (y_d @ w_branch[3])
    return x + gate[:, None, :] * (merged @ w_out)


def trunk(x, c, rel_bias, norm_w, w_ada, b_ada, w_in, b_gate, mlstm_norm_w, na_q_norm, na_k_norm, na_rpb,
          swa_q_norm, swa_k_norm, swa_sink, conv_w, conv_b, conv_ln_w, conv_ln_b, w_branch, w_out):
    for l in range(DEPTH):
        x = mixer_layer(x, c, rel_bias, norm_w[l], w_ada[l], b_ada[l], w_in[l], b_gate[l], mlstm_norm_w[l],
                        na_q_norm[l], na_k_norm[l], na_rpb[l], swa_q_norm[l], swa_k_norm[l], swa_sink[l],
                        conv_w[l], conv_b[l], conv_ln_w[l], conv_ln_b[l], w_branch[l], w_out[l])
    return x


def setup_inputs(seed: int = 0) -> dict:
    key = jax.random.key(seed)
    ks = jax.random.split(key, 24)
    f32 = jnp.float32
    nrm = lambda k, shape, s: s * jax.random.normal(k, shape, f32)
    d = D_MODEL
    b_gate = jnp.concatenate([jnp.zeros((DEPTH, 2 * A_HEADS), f32),
                              jnp.tile(jnp.linspace(3.0, 6.0, A_HEADS, dtype=f32), (DEPTH, 2))], axis=1) \
        + nrm(ks[9], (DEPTH, A_GATES), 0.1)
    return {
        'x_prompt': nrm(ks[0], (BATCH, SEQ, d), 1.0),
        'x_sample': nrm(ks[1], (DEC_BATCH, DEC_SEQ, d), 1.0),
        'c_prompt': nrm(ks[2], (BATCH, d), 1.0),
        'c_sample': nrm(ks[3], (DEC_BATCH, d), 1.0),
        'rel_bias': nrm(ks[4], (N_BUCKETS, C_HEADS), 0.1),
        'norm_w': 1.0 + nrm(ks[5], (DEPTH, d), 0.05),
        'w_ada': nrm(ks[6], (DEPTH, d, 3 * d), 0.5 * d ** -0.5),
        'b_ada': nrm(ks[7], (DEPTH, 3 * d), 0.01),
        'w_in': nrm(ks[8], (DEPTH, d, IN_COLS), d ** -0.5),
        'b_gate': b_gate,
        'mlstm_norm_w': 1.0 + nrm(ks[10], (DEPTH, BRANCH_W), 0.05),
        'na_q_norm': 1.0 + nrm(ks[11], (DEPTH, B_DH), 0.05),
        'na_k_norm': 1.0 + nrm(ks[12], (DEPTH, B_DH), 0.05),
        'na_rpb': nrm(ks[13], (DEPTH, B_HEADS, 2 * WIN_ROWS - 1, 2 * WIN_COLS - 1), 0.1),
        'swa_q_norm': 1.0 + nrm(ks[14], (DEPTH, C_DH), 0.05),
        'swa_k_norm': 1.0 + nrm(ks[15], (DEPTH, C_DH), 0.05),
        'swa_sink': nrm(ks[16], (DEPTH, C_HEADS), 0.5),
        'conv_w': nrm(ks[17], (DEPTH, CONV_W, BRANCH_W), CONV_W ** -0.5),
        'conv_b': nrm(ks[18], (DEPTH, BRANCH_W), 0.01),
        'conv_ln_w': 1.0 + nrm(ks[19], (DEPTH, BRANCH_W), 0.05),
        'conv_ln_b': nrm(ks[20], (DEPTH, BRANCH_W), 0.01),
        'w_branch': nrm(ks[21], (DEPTH, N_BRANCH, BRANCH_W, d), BRANCH_W ** -0.5),
        'w_out': nrm(ks[22], (DEPTH, d, d), d ** -0.5),
    }


def reference(x_prompt, x_sample, c_prompt, c_sample, rel_bias, norm_w, w_ada, b_ada, w_in, b_gate,
              mlstm_norm_w, na_q_norm, na_k_norm, na_rpb, swa_q_norm, swa_k_norm, swa_sink, conv_w, conv_b,
              conv_ln_w, conv_ln_b, w_branch, w_out):
    y_prompt = trunk(x_prompt, c_prompt, rel_bias, norm_w, w_ada, b_ada, w_in, b_gate, mlstm_norm_w,
                     na_q_norm, na_k_norm, na_rpb, swa_q_norm, swa_k_norm, swa_sink, conv_w, conv_b,
                     conv_ln_w, conv_ln_b, w_branch, w_out)
    y_sample = trunk(x_sample, c_sample, rel_bias, norm_w, w_ada, b_ada, w_in, b_gate, mlstm_norm_w,
                     na_q_norm, na_k_norm, na_rpb, swa_q_norm, swa_k_norm, swa_sink, conv_w, conv_b,
                     conv_ln_w, conv_ln_b, w_branch, w_out)
    return (y_prompt, y_sample)
```

```python
import functools
import math

import numpy as np
import jax
import jax.numpy as jnp
from jax import lax
from jax.experimental import pallas as pl
from jax.experimental.pallas import tpu as pltpu

F32 = jnp.float32
BF16 = jnp.bfloat16

V7X_VMEM_BYTES = 64 * 1024 * 1024
V7X_LANES = 128
MIB = 1024 * 1024

EPS = 1e-6
N_BRANCH = 4
BRANCH_W = 512
GRID_W = 64
A_HEADS = 4
A_DH = BRANCH_W // A_HEADS
A_GATES = 4 * A_HEADS
CHUNK = 128
M_INIT = -1e30
B_HEADS = 4
B_DH = BRANCH_W // B_HEADS
WIN_ROWS = 8
WIN_COLS = 16
NA_BLOCK_ROWS = 8
NA_BLOCK_TOK = NA_BLOCK_ROWS * GRID_W
C_HEADS = 8
C_KV_HEADS = 2
C_GROUP = C_HEADS // C_KV_HEADS
C_DH = BRANCH_W // C_HEADS
C_KV_W = C_KV_HEADS * C_DH
C_WINDOW = 128
C_BLOCK = 128
N_BUCKETS = 32
MAX_DIST = 128
CONV_W = 31
CONV_HALO = 16
CONV_TILE = 512

COL_AQ, COL_AK, COL_AV, COL_AO, COL_AZ = 0, 512, 1024, 1536, 2048
COL_BQ, COL_BK, COL_BV, COL_BZ = 2560, 3072, 3584, 4096
COL_DA, COL_DG, COL_DZ = 4608, 5120, 5632
COL_CQ, COL_CZ = 6144, 6656
COL_CK, COL_CV, COL_AG = 7168, 7296, 7424
PACK_COLS = 7680


def _tiles():
    return dict(
        ada_tn=512,
        in_tm=1024, in_tn=512,
        mo_tm=512, mo_tn=256,
        vmem_small=32 * MIB,
        vmem_big=V7X_VMEM_BYTES - 8 * MIB,
    )


def _sigmoid(x):
    return jax.nn.sigmoid(x)


def _silu(x):
    return x * jax.nn.sigmoid(x)


def _log_sigmoid(x):
    return jnp.minimum(x, 0.0) - jnp.log1p(jnp.exp(-jnp.abs(x)))


def _rms(x, w):
    r = lax.rsqrt(jnp.mean(x * x, axis=-1, keepdims=True) + EPS)
    return x * r * w


def _ada_kernel(c_ref, w_ref, b_ref, o_ref):
    c = c_ref[...]
    sc = _silu(c).astype(BF16)
    o_ref[...] = jnp.dot(sc, w_ref[...].astype(BF16), preferred_element_type=F32) + b_ref[...]


def _ada_call(c_all, w_ada, b_ada):
    t = _tiles()
    depth, d, n3 = w_ada.shape
    rows = c_all.shape[0]
    tn = t["ada_tn"]
    return pl.pallas_call(
        _ada_kernel,
        grid=(depth, n3 // tn),
        in_specs=[
            pl.BlockSpec((rows, d), lambda l, j: (0, 0)),
            pl.BlockSpec((None, d, tn), lambda l, j: (l, 0, j)),
            pl.BlockSpec((None, 1, tn), lambda l, j: (l, 0, j)),
        ],
        out_specs=pl.BlockSpec((None, rows, tn), lambda l, j: (l, 0, j)),
        out_shape=jax.ShapeDtypeStruct((depth, rows, n3), F32),
        compiler_params=pltpu.CompilerParams(
            dimension_semantics=("arbitrary", "arbitrary"), vmem_limit_bytes=t["vmem_small"]),
        name="ada",
    )(c_all, w_ada, b_ada.reshape(depth, 1, n3))


def _mod_spec(d, row_of_tile, kind):
    return pl.BlockSpec((None, 1, d), lambda i, j: (row_of_tile(i) * 3 + kind, 0, 0))


def _modulated(x_ref, nw_ref, scale_ref, shift_ref):
    x = x_ref[...]
    y = _rms(x, nw_ref[...])
    return (y * (1.0 + scale_ref[...]) + shift_ref[...]).astype(BF16)


def _inproj_kernel(x_ref, nw_ref, scale_ref, shift_ref, w_ref, o_ref, h_ref):
    @pl.when(pl.program_id(1) == 0)
    def _():
        h_ref[...] = _modulated(x_ref, nw_ref, scale_ref, shift_ref)

    o_ref[...] = jnp.dot(h_ref[...], w_ref[...], preferred_element_type=F32)


def _row_of_tile(tm, seq, first_row):
    assert seq % tm == 0
    return lambda i: first_row + (i * tm) // seq


def _inproj_call(x2d, nw, mod_rows, w_pack, seq, first_row):
    t = _tiles()
    m, d = x2d.shape
    n = w_pack.shape[1]
    tm, tn = min(t["in_tm"], seq), t["in_tn"]
    row_of_tile = _row_of_tile(tm, seq, first_row)
    return pl.pallas_call(
        _inproj_kernel,
        grid=(m // tm, n // tn),
        in_specs=[
            pl.BlockSpec((tm, d), lambda i, j: (i, 0)),
            pl.BlockSpec((1, d), lambda i, j: (0, 0)),
            _mod_spec(d, row_of_tile, 1),
            _mod_spec(d, row_of_tile, 0),
            pl.BlockSpec((d, tn), lambda i, j: (0, j)),
        ],
        out_specs=pl.BlockSpec((tm, tn), lambda i, j: (i, j)),
        out_shape=jax.ShapeDtypeStruct((m, n), F32),
        scratch_shapes=[pltpu.VMEM((tm, d), BF16)],
        compiler_params=pltpu.CompilerParams(
            dimension_semantics=("arbitrary", "arbitrary"), vmem_limit_bytes=t["vmem_big"]),
        name="inproj",
    )(x2d, nw, mod_rows, mod_rows, w_pack)


def _mlstm_kernel(q_ref, k_ref, v_ref, g_ref, o_ref, z_ref, bg_ref, nw_ref, y_ref,
                  hfw_ref, c_ref, n_ref, m_ref, *, nc):
    s = pl.program_id(1)
    is_fwd = s < nc
    chunk = jnp.where(is_fwd, s, 2 * nc - 1 - s)
    row0 = pl.multiple_of(chunk * CHUNK, CHUNK)

    @pl.when((s == 0) | (s == nc))
    def _():
        c_ref[...] = jnp.zeros_like(c_ref)
        n_ref[...] = jnp.zeros_like(n_ref)
        m_ref[...] = jnp.full_like(m_ref, M_INIT)

    gates = g_ref[...] + bg_ref[...]
    gates = jnp.where(is_fwd, gates, pltpu.roll(gates, V7X_LANES - A_HEADS, axis=1))
    logf = _log_sigmoid(gates)
    ri = lax.broadcasted_iota(jnp.int32, (CHUNK, CHUNK), 0)
    ci = lax.broadcasted_iota(jnp.int32, (CHUNK, CHUNK), 1)
    sgn = jnp.where(is_fwd, 1, -1)
    keep = ((ri - ci) * sgn) >= 0
    tri = keep.astype(F32)
    bcum = jnp.dot(tri, logf, preferred_element_type=F32, precision=lax.Precision.HIGHEST)
    bcum_t = bcum.T
    gates_t = gates.T
    gsum = jnp.sum(logf, axis=0, keepdims=True)

    houts = []
    for h in range(A_HEADS):
        hs = slice(h * A_DH, (h + 1) * A_DH)
        fcol = 2 * A_HEADS + h
        q = q_ref[:, hs]
        k = k_ref[:, hs] * (A_DH ** -0.5)
        qb = q.astype(BF16)
        vb = v_ref[:, hs].astype(BF16)
        b_col = bcum[:, fcol:fcol + 1]
        b_row = bcum_t[fcol:fcol + 1, :]
        i_row = gates_t[h:h + 1, :]
        i_col = gates[:, h:h + 1]
        m_st = m_ref[h:h + 1, 0:1]
        g_tot = gsum[:, fcol:fcol + 1]

        dmat = jnp.where(keep, b_col - b_row + i_row, -jnp.inf)
        inter = b_col + m_st
        m_t = jnp.maximum(jnp.max(dmat, axis=-1, keepdims=True), inter)
        qk = lax.dot_general(qb, k.astype(BF16), (((1,), (1,)), ((), ())), preferred_element_type=F32)
        smat = qk * jnp.exp(dmat - m_t)
        a_in = jnp.exp(inter - m_t)
        c_st = c_ref[h]
        n_st = n_ref[h:h + 1, :]
        num = jnp.dot(smat.astype(BF16), vb, preferred_element_type=F32) \
            + a_in * jnp.dot(qb, c_st.astype(BF16), preferred_element_type=F32)
        den = jnp.sum(smat, axis=-1, keepdims=True) + a_in * jnp.sum(q * n_st, axis=-1, keepdims=True)
        houts.append(num / jnp.maximum(jnp.abs(den), jnp.exp(-m_t)))

        w_end = g_tot - b_col + i_col
        m_loc = jnp.max(w_end, axis=0, keepdims=True)
        ka = k * jnp.exp(w_end - m_loc)
        c_loc = lax.dot_general(ka.astype(BF16), vb, (((0,), (0,)), ((), ())), preferred_element_type=F32)
        n_loc = jnp.sum(ka, axis=0, keepdims=True)
        m_new = jnp.maximum(g_tot + m_st, m_loc)
        a_old = jnp.exp(g_tot + m_st - m_new)
        a_new = jnp.exp(m_loc - m_new)
        c_ref[h] = a_old * c_st + a_new * c_loc
        n_ref[h:h + 1, :] = a_old * n_st + a_new * n_loc
        m_ref[h:h + 1, :] = jnp.broadcast_to(m_new, (1, V7X_LANES))

    @pl.when(is_fwd)
    def _():
        for h in range(A_HEADS):
            hfw_ref[pl.ds(row0, CHUNK), h * A_DH:(h + 1) * A_DH] = houts[h]

    @pl.when(jnp.logical_not(is_fwd))
    def _():
        for h in range(A_HEADS):
            hs = slice(h * A_DH, (h + 1) * A_DH)
            tot = hfw_ref[pl.ds(row0, CHUNK), hs] + houts[h]
            yh = _rms(tot, nw_ref[:, hs]) * _sigmoid(o_ref[:, hs])
            y_ref[:, hs] = (yh * _silu(z_ref[:, hs])).astype(BF16)


def _mlstm_call(p3, bg_pad, nw):
    t = _tiles()
    bsz, seq, _ = p3.shape
    nc = seq // CHUNK

    def cur(s):
        return jnp.where(s < nc, s, 2 * nc - 1 - s)

    def late(s):
        return jnp.where(s < nc, nc - 1, 2 * nc - 1 - s)

    def blk(col, which):
        return pl.BlockSpec((None, CHUNK, BRANCH_W), lambda b, s: (b, which(s), col // BRANCH_W))

    return pl.pallas_call(
        functools.partial(_mlstm_kernel, nc=nc),
        grid=(bsz, 2 * nc),
        in_specs=[
            blk(COL_AQ, cur), blk(COL_AK, cur), blk(COL_AV, cur),
            pl.BlockSpec((None, CHUNK, V7X_LANES), lambda b, s: (b, cur(s), COL_AG // V7X_LANES)),
            blk(COL_AO, late), blk(COL_AZ, late),
            pl.BlockSpec((1, V7X_LANES), lambda b, s: (0, 0)),
            pl.BlockSpec((1, BRANCH_W), lambda b, s: (0, 0)),
        ],
        out_specs=pl.BlockSpec((None, CHUNK, BRANCH_W), lambda b, s: (b, late(s), 0)),
        out_shape=jax.ShapeDtypeStruct((bsz, seq, BRANCH_W), BF16),
        scratch_shapes=[
            pltpu.VMEM((seq, BRANCH_W), F32),
            pltpu.VMEM((A_HEADS, A_DH, A_DH), F32),
            pltpu.VMEM((8, A_DH), F32),
            pltpu.VMEM((8, V7X_LANES), F32),
        ],
        compiler_params=pltpu.CompilerParams(
            dimension_semantics=("arbitrary", "arbitrary"), vmem_limit_bytes=t["vmem_big"]),
        name="mlstm",
    )(p3, p3, p3, p3, p3, p3, bg_pad, nw)


def _nattn_kernel(q_ref, kp_ref, kc_ref, kx_ref, vp_ref, vc_ref, vx_ref, z_ref, qw_ref, kw_ref,
                  bias_ref, y_ref, kn_ref, vn_ref, *, rows):
    i = pl.program_id(1)
    for j, (kr, vr) in enumerate(((kp_ref, vp_ref), (kc_ref, vc_ref), (kx_ref, vx_ref))):
        lo = j * NA_BLOCK_TOK
        for h in range(B_HEADS):
            hs = slice(h * B_DH, (h + 1) * B_DH)
            kn_ref[lo:lo + NA_BLOCK_TOK, hs] = _rms(kr[:, hs], kw_ref[...]).astype(BF16)
        vn_ref[lo:lo + NA_BLOCK_TOK, :] = vr[...].astype(BF16)

    qn = []
    for h in range(B_HEADS):
        hs = slice(h * B_DH, (h + 1) * B_DH)
        qn.append((_rms(q_ref[:, hs], qw_ref[...]) * (B_DH ** -0.5)).astype(BF16))

    for rr in range(NA_BLOCK_ROWS):
        r = i * NA_BLOCK_ROWS + rr
        r0 = jnp.clip(r - WIN_ROWS // 2, 0, rows - WIN_ROWS)
        dr0 = r0 - r + WIN_ROWS - 1
        off = pl.multiple_of((r0 - (i - 1) * NA_BLOCK_ROWS) * GRID_W, GRID_W)
        qrows = slice(rr * GRID_W, (rr + 1) * GRID_W)
        for h in range(B_HEADS):
            hs = slice(h * B_DH, (h + 1) * B_DH)
            kwin = kn_ref[pl.ds(off, NA_BLOCK_TOK), hs]
            vwin = vn_ref[pl.ds(off, NA_BLOCK_TOK), hs]
            sc = lax.dot_general(qn[h][qrows, :], kwin, (((1,), (1,)), ((), ())),
                                 preferred_element_type=F32)
            sc = sc + bias_ref[h, dr0]
            m = jnp.max(sc, axis=-1, keepdims=True)
            p = jnp.exp(sc - m)
            l = jnp.sum(p, axis=-1, keepdims=True)
            o = jnp.dot(p.astype(BF16), vwin, preferred_element_type=F32) / l
            y_ref[qrows, hs] = (o * _silu(z_ref[qrows, hs])).astype(BF16)


def _nattn_call(p3, qw, kw, bias_big):
    t = _tiles()
    bsz, seq, _ = p3.shape
    rows = seq // GRID_W
    assert rows >= WIN_ROWS and rows % NA_BLOCK_ROWS == 0
    nblk = rows // NA_BLOCK_ROWS

    def blk(col, shift):
        cb = col // BRANCH_W
        return pl.BlockSpec((None, NA_BLOCK_TOK, BRANCH_W),
                            lambda b, i: (b, jnp.clip(i + shift, 0, nblk - 1), cb))

    return pl.pallas_call(
        functools.partial(_nattn_kernel, rows=rows),
        grid=(bsz, nblk),
        in_specs=[
            blk(COL_BQ, 0),
            blk(COL_BK, -1), blk(COL_BK, 0), blk(COL_BK, 1),
            blk(COL_BV, -1), blk(COL_BV, 0), blk(COL_BV, 1),
            blk(COL_BZ, 0),
            pl.BlockSpec((1, B_DH), lambda b, i: (0, 0)),
            pl.BlockSpec((1, B_DH), lambda b, i: (0, 0)),
            pl.BlockSpec(bias_big.shape, lambda b, i: (0, 0, 0, 0)),
        ],
        out_specs=pl.BlockSpec((None, NA_BLOCK_TOK, BRANCH_W), lambda b, i: (b, i, 0)),
        out_shape=jax.ShapeDtypeStruct((bsz, seq, BRANCH_W), BF16),
        scratch_shapes=[
            pltpu.VMEM((3 * NA_BLOCK_TOK, BRANCH_W), BF16),
            pltpu.VMEM((3 * NA_BLOCK_TOK, BRANCH_W), BF16),
        ],
        compiler_params=pltpu.CompilerParams(
            dimension_semantics=("arbitrary", "arbitrary"), vmem_limit_bytes=t["vmem_big"]),
        name="nattn",
    )(p3, p3, p3, p3, p3, p3, p3, p3, qw, kw, bias_big)


def _nattn_bias(rpb):
    qc = np.arange(GRID_W)[:, None]
    kc = np.arange(GRID_W)[None, :]
    qs = np.clip(qc - WIN_COLS // 2, 0, GRID_W - WIN_COLS)
    ok = (kc >= qs) & (kc < qs + WIN_COLS)
    dc = np.clip(kc - qc + WIN_COLS - 1, 0, 2 * WIN_COLS - 2)
    tbl = jnp.where(ok[None, None], rpb.astype(F32)[:, :, dc], -jnp.inf)
    per_d = [jnp.concatenate([tbl[:, d + j] for j in range(WIN_ROWS)], axis=-1) for d in range(WIN_ROWS)]
    return jnp.stack(per_d, axis=1)


def _swa_kernel(q_ref, kp_ref, kc_ref, kx_ref, vp_ref, vc_ref, vx_ref, z_ref, qw_ref, kw_ref,
                sink_ref, bias_ref, y_ref, *, nb):
    n = pl.program_id(1)
    kcat = jnp.concatenate([kp_ref[...], kc_ref[...], kx_ref[...]], axis=0)
    vcat = jnp.concatenate([vp_ref[...], vc_ref[...], vx_ref[...]], axis=0).astype(BF16)
    ki = lax.broadcasted_iota(jnp.int32, (1, 3 * C_BLOCK), 1)
    valid = ((ki >= C_BLOCK) | (n > 0)) & ((ki < 2 * C_BLOCK) | (n < nb - 1))
    kn = []
    for g in range(C_KV_HEADS):
        gs = slice(g * C_DH, (g + 1) * C_DH)
        kn.append(_rms(kcat[:, gs], kw_ref[...]).astype(BF16))
    for hq in range(C_HEADS):
        g = hq // C_GROUP
        hs = slice(hq * C_DH, (hq + 1) * C_DH)
        qn = (_rms(q_ref[:, hs], qw_ref[...]) * (C_DH ** -0.5)).astype(BF16)
        sc = lax.dot_general(qn, kn[g], (((1,), (1,)), ((), ())), preferred_element_type=F32)
        sc = jnp.where(valid, sc + bias_ref[hq], -jnp.inf)
        sink = sink_ref[hq:hq + 1, 0:1]
        m = jnp.maximum(jnp.max(sc, axis=-1, keepdims=True), sink)
        p = jnp.exp(sc - m)
        denom = jnp.sum(p, axis=-1, keepdims=True) + jnp.exp(sink - m)
        o = jnp.dot(p.astype(BF16), vcat[:, g * C_DH:(g + 1) * C_DH], preferred_element_type=F32) / denom
        y_ref[:, hs] = (o * _silu(z_ref[:, hs])).astype(BF16)


def _swa_call(p3, qw, kw, sink_rows, bias):
    t = _tiles()
    bsz, seq, _ = p3.shape
    nb = seq // C_BLOCK

    def wide(col):
        return pl.BlockSpec((None, C_BLOCK, BRANCH_W), lambda b, n: (b, n, col // BRANCH_W))

    def kv(col, shift):
        return pl.BlockSpec((None, C_BLOCK, C_KV_W),
                            lambda b, n: (b, jnp.clip(n + shift, 0, nb - 1), col // C_KV_W))

    return pl.pallas_call(
        functools.partial(_swa_kernel, nb=nb),
        grid=(bsz, nb),
        in_specs=[
            wide(COL_CQ),
            kv(COL_CK, -1), kv(COL_CK, 0), kv(COL_CK, 1),
            kv(COL_CV, -1), kv(COL_CV, 0), kv(COL_CV, 1),
            wide(COL_CZ),
            pl.BlockSpec((1, C_DH), lambda b, n: (0, 0)),
            pl.BlockSpec((1, C_DH), lambda b, n: (0, 0)),
            pl.BlockSpec(sink_rows.shape, lambda b, n: (0, 0)),
            pl.BlockSpec(bias.shape, lambda b, n: (0, 0, 0)),
        ],
        out_specs=pl.BlockSpec((None, C_BLOCK, BRANCH_W), lambda b, n: (b, n, 0)),
        out_shape=jax.ShapeDtypeStruct((bsz, seq, BRANCH_W), BF16),
        compiler_params=pltpu.CompilerParams(
            dimension_semantics=("arbitrary", "arbitrary"), vmem_limit_bytes=t["vmem_small"]),
        name="swa",
    )(p3, p3, p3, p3, p3, p3, p3, p3, qw, kw, sink_rows, bias)


def _t5_bucket_table():
    rel = np.arange(3 * C_BLOCK)[None, :] - C_BLOCK - np.arange(C_BLOCK)[:, None]
    half = N_BUCKETS // 2
    max_exact = half // 2
    n = np.abs(rel)
    nf = np.maximum(n, 1).astype(np.float32)
    scale = np.float32(math.log(MAX_DIST / max_exact))
    large = max_exact + (np.log(nf / np.float32(max_exact)) / scale
                         * np.float32(half - max_exact)).astype(np.int32)
    large = np.minimum(large, half - 1)
    bucket = np.where(rel > 0, half, 0) + np.where(n < max_exact, n, large)
    return bucket, n <= C_WINDOW


def _swa_bias(rel_bias):
    bucket, in_window = _t5_bucket_table()
    tbl = rel_bias.astype(F32)[bucket]
    tbl = jnp.where(in_window[..., None], tbl, -jnp.inf)
    return jnp.transpose(tbl, (2, 0, 1))


def _conv_kernel(a_ref, g_ref, ap_ref, gp_ref, ax_ref, gx_ref, z_ref, cw_ref, cb_ref, lw_ref, lb_ref,
                 y_ref, ext_ref, *, ntile):
    i = pl.program_id(1)
    tt = a_ref.shape[0]
    prev = jnp.where(i > 0, ap_ref[...] * _sigmoid(gp_ref[...]), 0.0)
    nxt = jnp.where(i < ntile - 1, ax_ref[...] * _sigmoid(gx_ref[...]), 0.0)
    ext_ref[0:CONV_HALO, :] = prev
    ext_ref[CONV_HALO:CONV_HALO + tt, :] = a_ref[...] * _sigmoid(g_ref[...])
    ext_ref[CONV_HALO + tt:2 * CONV_HALO + tt, :] = nxt
    acc = None
    for w in range(CONV_W):
        term = ext_ref[pl.ds(CONV_HALO - CONV_W // 2 + w, tt), :] * cw_ref[w:w + 1, :]
        acc = term if acc is None else acc + term
    u = acc + cb_ref[...]
    mu = jnp.mean(u, axis=-1, keepdims=True)
    var = jnp.mean(jnp.square(u - mu), axis=-1, keepdims=True)
    u = (u - mu) * lax.rsqrt(var + EPS) * lw_ref[...] + lb_ref[...]
    y_ref[...] = (_silu(u) * _silu(z_ref[...])).astype(BF16)


def _conv_call(p3, cw_pad, cb, lw, lb):
    t = _tiles()
    bsz, seq, _ = p3.shape
    tt = CONV_TILE
    ntile = seq // tt
    per = tt // CONV_HALO
    nhalo = seq // CONV_HALO

    def cur(col):
        return pl.BlockSpec((None, tt, BRANCH_W), lambda b, i: (b, i, col // BRANCH_W))

    def halo(col, after):
        def idx(b, i):
            r = (i + 1) * per if after else i * per - 1
            return (b, jnp.clip(r, 0, nhalo - 1), col // BRANCH_W)
        return pl.BlockSpec((None, CONV_HALO, BRANCH_W), idx)

    vec = pl.BlockSpec((1, BRANCH_W), lambda b, i: (0, 0))
    return pl.pallas_call(
        functools.partial(_conv_kernel, ntile=ntile),
        grid=(bsz, ntile),
        in_specs=[
            cur(COL_DA), cur(COL_DG),
            halo(COL_DA, False), halo(COL_DG, False), halo(COL_DA, True), halo(COL_DG, True),
            cur(COL_DZ),
            pl.BlockSpec(cw_pad.shape, lambda b, i: (0, 0)),
            vec, vec, vec,
        ],
        out_specs=pl.BlockSpec((None, tt, BRANCH_W), lambda b, i: (b, i, 0)),
        out_shape=jax.ShapeDtypeStruct((bsz, seq, BRANCH_W), BF16),
        scratch_shapes=[pltpu.VMEM((tt + 2 * CONV_HALO, BRANCH_W), F32)],
        compiler_params=pltpu.CompilerParams(
            dimension_semantics=("arbitrary", "arbitrary"), vmem_limit_bytes=t["vmem_small"]),
        name="conv",
    )(p3, p3, p3, p3, p3, p3, p3, cw_pad, cb, lw, lb)


def _mergeout_kernel(x_ref, nw_ref, scale_ref, shift_ref, gate_ref, ya_ref, yb_ref, yc_ref, yd_ref,
                     wm_ref, wb_ref, wo_ref, o_ref, h_ref, acc_ref):
    n = pl.program_id(1)

    @pl.when(n == 0)
    def _():
        h_ref[...] = _modulated(x_ref, nw_ref, scale_ref, shift_ref)
        acc_ref[...] = jnp.zeros_like(acc_ref)

    h = h_ref[...]
    merged = None
    for b, y_ref in enumerate((ya_ref, yb_ref, yc_ref, yd_ref)):
        gate = _sigmoid(jnp.dot(h, wm_ref[b], preferred_element_type=F32))
        term = gate * jnp.dot(y_ref[...], wb_ref[b], preferred_element_type=F32)
        merged = term if merged is None else merged + term
    acc_ref[...] += jnp.dot(merged.astype(BF16), wo_ref[...], preferred_element_type=F32)

    @pl.when(n == pl.num_programs(1) - 1)
    def _():
        o_ref[...] = x_ref[...] + gate_ref[...] * acc_ref[...]


def _mergeout_call(x2d, nw, mod_rows, ys, w_merge, w_branch, w_out, seq, first_row):
    t = _tiles()
    m, d = x2d.shape
    tm, tn = min(t["mo_tm"], seq), t["mo_tn"]
    row_of_tile = _row_of_tile(tm, seq, first_row)
    yspec = pl.BlockSpec((tm, BRANCH_W), lambda i, j: (i, 0))
    return pl.pallas_call(
        _mergeout_kernel,
        grid=(m // tm, d // tn),
        in_specs=[
            pl.BlockSpec((tm, d), lambda i, j: (i, 0)),
            pl.BlockSpec((1, d), lambda i, j: (0, 0)),
            _mod_spec(d, row_of_tile, 1),
            _mod_spec(d, row_of_tile, 0),
            _mod_spec(d, row_of_tile, 2),
            yspec, yspec, yspec, yspec,
            pl.BlockSpec((N_BRANCH, d, tn), lambda i, j: (0, 0, j)),
            pl.BlockSpec((N_BRANCH, BRANCH_W, tn), lambda i, j: (0, 0, j)),
            pl.BlockSpec((tn, d), lambda i, j: (j, 0)),
        ],
        out_specs=pl.BlockSpec((tm, d), lambda i, j: (i, 0)),
        out_shape=jax.ShapeDtypeStruct((m, d), F32),
        scratch_shapes=[pltpu.VMEM((tm, d), BF16), pltpu.VMEM((tm, d), F32)],
        compiler_params=pltpu.CompilerParams(
            dimension_semantics=("arbitrary", "arbitrary"), vmem_limit_bytes=t["vmem_big"]),
        name="mergeout",
    )(x2d, nw, mod_rows, mod_rows, mod_rows, *ys, w_merge, w_branch, w_out)


def _pack_w_in(w):
    d = w.shape[0]
    sizes = (512, 512, 512, 512, 512, A_GATES, 512, 512, 512, 512, 512, C_KV_W, C_KV_W, 512, 1024, 512)
    offs = np.concatenate([[0], np.cumsum(sizes)])
    (aq, ak, av, ao, az, ag, bq, bk, bv, bz, cq, ck, cv, cz, dglu, dz) = [
        w[:, int(offs[i]):int(offs[i + 1])] for i in range(len(sizes))]
    pad = jnp.zeros((d, PACK_COLS - COL_AG - A_GATES), w.dtype)
    packed = jnp.concatenate([aq, ak, av, ao, az, bq, bk, bv, bz, dglu, dz, cq, cz, ck, cv, ag, pad], axis=1)
    merge = w[:, int(offs[-1]):].reshape(d, N_BRANCH, d).transpose(1, 0, 2)
    return packed.astype(BF16), merge.astype(BF16)


def kernel(x_prompt, x_sample, c_prompt, c_sample, rel_bias, norm_w, w_ada, b_ada, w_in, b_gate, mlstm_norm_w, na_q_norm, na_k_norm, na_rpb, swa_q_norm, swa_k_norm, swa_sink, conv_w, conv_b, conv_ln_w, conv_ln_b, w_branch, w_out):
    depth, d = norm_w.shape
    groups = ((x_prompt, 0), (x_sample, c_prompt.shape[0]))
    n_cond = c_prompt.shape[0] + c_sample.shape[0]
    cond_rows = -(-n_cond // 8) * 8
    c_all = jnp.concatenate([c_prompt, c_sample, jnp.zeros((cond_rows - n_cond, d), F32)], axis=0)
    mod = _ada_call(c_all, w_ada, b_ada)
    mod_rows = mod.reshape(depth * cond_rows * 3, 1, d)

    swa_bias = _swa_bias(rel_bias)
    layers = []
    for l in range(depth):
        w_pack, w_merge = _pack_w_in(w_in[l])
        layers.append(dict(
            w_pack=w_pack, w_merge=w_merge,
            w_branch=w_branch[l].astype(BF16), w_out=w_out[l].astype(BF16),
            nw=norm_w[l].reshape(1, d),
            bg=jnp.pad(b_gate[l], (0, V7X_LANES - A_GATES)).reshape(1, V7X_LANES),
            mnw=mlstm_norm_w[l].reshape(1, BRANCH_W),
            na_qw=na_q_norm[l].reshape(1, B_DH), na_kw=na_k_norm[l].reshape(1, B_DH),
            na_bias=_nattn_bias(na_rpb[l]),
            swa_qw=swa_q_norm[l].reshape(1, C_DH), swa_kw=swa_k_norm[l].reshape(1, C_DH),
            sink=jnp.broadcast_to(swa_sink[l].reshape(C_HEADS, 1), (C_HEADS, V7X_LANES)),
            cw=jnp.pad(conv_w[l], ((0, 1), (0, 0))),
            cb=conv_b[l].reshape(1, BRANCH_W), lw=conv_ln_w[l].reshape(1, BRANCH_W),
            lb=conv_ln_b[l].reshape(1, BRANCH_W),
        ))

    outs = []
    for x, cond_off in groups:
        bsz, seq, _ = x.shape
        x2d = x.reshape(bsz * seq, d)
        for l, lw in enumerate(layers):
            first_row = l * cond_rows + cond_off
            p = _inproj_call(x2d, lw["nw"], mod_rows, lw["w_pack"], seq, first_row)
            p3 = p.reshape(bsz, seq, PACK_COLS)
            ya = _mlstm_call(p3, lw["bg"], lw["mnw"])
            yb = _nattn_call(p3, lw["na_qw"], lw["na_kw"], lw["na_bias"])
            yc = _swa_call(p3, lw["swa_qw"], lw["swa_kw"], lw["sink"], swa_bias)
            yd = _conv_call(p3, lw["cw"], lw["cb"], lw["lw"], lw["lb"])
            ys = [y.reshape(bsz * seq, BRANCH_W) for y in (ya, yb, yc, yd)]
            x2d = _mergeout_call(x2d, lw["nw"], mod_rows, ys, lw["w_merge"], lw["w_branch"], lw["w_out"],
                                 seq, first_row)
        outs.append(x2d.reshape(bsz, seq, d))
    return tuple(outs)
```

```python
import functools
import math

import numpy as np
import jax
import jax.numpy as jnp
from jax import lax
from jax.experimental import pallas as pl
from jax.experimental.pallas import tpu as pltpu

F32 = jnp.float32
BF16 = jnp.bfloat16

V7X_VMEM_BYTES = 64 * 1024 * 1024
V7X_LANES = 128
V7X_SUBLANES = 8
MIB = 1024 * 1024

EPS = 1e-6
N_BRANCH = 4
BRANCH_W = 512
GRID_W = 64
A_HEADS = 4
A_DH = BRANCH_W // A_HEADS
A_GATES = 4 * A_HEADS
CHUNK = 128
MLSTM_BLOCK_CHUNKS = 4
M_INIT = -1e30
B_HEADS = 4
B_DH = BRANCH_W // B_HEADS
WIN_ROWS = 8
WIN_COLS = 16
NA_BLOCK_ROWS = 8
NA_BLOCK_TOK = NA_BLOCK_ROWS * GRID_W
NA_QUAD_ROWS = 4
NA_QUAD_TOK = NA_QUAD_ROWS * GRID_W
NA_WIN_TOK = (WIN_ROWS + NA_QUAD_ROWS) * GRID_W
NA_RING_SLOTS = 4
C_HEADS = 8
C_KV_HEADS = 2
C_GROUP = C_HEADS // C_KV_HEADS
C_DH = BRANCH_W // C_HEADS
C_KV_W = C_KV_HEADS * C_DH
C_WINDOW = 128
C_BLOCK = 128
N_BUCKETS = 32
MAX_DIST = 128
CONV_W = 31
CONV_HALO = 16
CONV_TILE = 512

COL_AQ, COL_AK, COL_AV, COL_AO, COL_AZ = 0, 512, 1024, 1536, 2048
COL_BQ, COL_BK, COL_BV, COL_BZ = 2560, 3072, 3584, 4096
COL_DA, COL_DG, COL_DZ = 4608, 5120, 5632
COL_CQ, COL_CZ = 6144, 6656
COL_CK, COL_CV, COL_AG = 7168, 7296, 7424
PACK_COLS = 7680


def _tiles():
    return dict(
        ada_tn=512,
        in_tm=1024, in_tn=512,
        mo_tm=512, mo_tn=256,
        vmem_small=32 * MIB,
        vmem_big=V7X_VMEM_BYTES - 8 * MIB,
    )


def _sigmoid(x):
    return jax.nn.sigmoid(x)


def _silu(x):
    return x * jax.nn.sigmoid(x)


def _log_sigmoid(x):
    return jnp.minimum(x, 0.0) - jnp.log1p(jnp.exp(-jnp.abs(x)))


def _rms(x, w):
    r = lax.rsqrt(jnp.mean(x * x, axis=-1, keepdims=True) + EPS)
    return x * r * w


def _ada_kernel(c_ref, w_ref, b_ref, o_ref):
    c = c_ref[...]
    sc = _silu(c).astype(BF16)
    o_ref[...] = jnp.dot(sc, w_ref[...].astype(BF16), preferred_element_type=F32) + b_ref[...]


def _ada_call(c_all, w_ada, b_ada):
    t = _tiles()
    depth, d, n3 = w_ada.shape
    rows = c_all.shape[0]
    tn = t["ada_tn"]
    return pl.pallas_call(
        _ada_kernel,
        grid=(depth, n3 // tn),
        in_specs=[
            pl.BlockSpec((rows, d), lambda l, j: (0, 0)),
            pl.BlockSpec((None, d, tn), lambda l, j: (l, 0, j)),
            pl.BlockSpec((None, 1, tn), lambda l, j: (l, 0, j)),
        ],
        out_specs=pl.BlockSpec((None, rows, tn), lambda l, j: (l, 0, j)),
        out_shape=jax.ShapeDtypeStruct((depth, rows, n3), F32),
        compiler_params=pltpu.CompilerParams(
            dimension_semantics=("arbitrary", "arbitrary"), vmem_limit_bytes=t["vmem_small"]),
        name="ada",
    )(c_all, w_ada, b_ada.reshape(depth, 1, n3))


def _mod_spec(d, row_of_tile, kind):
    return pl.BlockSpec((None, 1, d), lambda i, j: (row_of_tile(i) * 3 + kind, 0, 0))


def _modulated(x_ref, nw_ref, scale_ref, shift_ref):
    x = x_ref[...]
    y = _rms(x, nw_ref[...])
    return (y * (1.0 + scale_ref[...]) + shift_ref[...]).astype(BF16)


def _inproj_kernel(x_ref, nw_ref, scale_ref, shift_ref, w_ref, o_ref, h_ref):
    @pl.when(pl.program_id(1) == 0)
    def _():
        h_ref[...] = _modulated(x_ref, nw_ref, scale_ref, shift_ref)

    o_ref[...] = jnp.dot(h_ref[...], w_ref[...], preferred_element_type=F32)


def _row_of_tile(tm, seq, first_row):
    assert seq % tm == 0
    return lambda i: first_row + (i * tm) // seq


def _inproj_call(x2d, nw, mod_rows, w_pack, seq, first_row):
    t = _tiles()
    m, d = x2d.shape
    n = w_pack.shape[1]
    tm, tn = min(t["in_tm"], seq), t["in_tn"]
    row_of_tile = _row_of_tile(tm, seq, first_row)
    return pl.pallas_call(
        _inproj_kernel,
        grid=(m // tm, n // tn),
        in_specs=[
            pl.BlockSpec((tm, d), lambda i, j: (i, 0)),
            pl.BlockSpec((1, d), lambda i, j: (0, 0)),
            _mod_spec(d, row_of_tile, 1),
            _mod_spec(d, row_of_tile, 0),
            pl.BlockSpec((d, tn), lambda i, j: (0, j)),
        ],
        out_specs=pl.BlockSpec((tm, tn), lambda i, j: (i, j)),
        out_shape=jax.ShapeDtypeStruct((m, n), F32),
        scratch_shapes=[pltpu.VMEM((tm, d), BF16)],
        compiler_params=pltpu.CompilerParams(
            dimension_semantics=("arbitrary", "arbitrary"), vmem_limit_bytes=t["vmem_big"]),
        name="inproj",
    )(x2d, nw, mod_rows, mod_rows, w_pack)


def _mlstm_kernel(q_ref, k_ref, v_ref, g_ref, o_ref, z_ref, bg_ref, nw_ref, y_ref,
                  hfw_ref, c_ref, n_ref, m_ref, *, nblk, cb):
    s = pl.program_id(1)
    is_fwd = s < nblk
    blk = jnp.where(is_fwd, s, 2 * nblk - 1 - s)

    @pl.when((s == 0) | (s == nblk))
    def _():
        c_ref[...] = jnp.zeros_like(c_ref)
        n_ref[...] = jnp.zeros_like(n_ref)
        m_ref[...] = jnp.full_like(m_ref, M_INIT)

    ri = lax.broadcasted_iota(jnp.int32, (CHUNK, CHUNK), 0)
    ci = lax.broadcasted_iota(jnp.int32, (CHUNK, CHUNK), 1)
    sgn = jnp.where(is_fwd, 1, -1)
    keep = ((ri - ci) * sgn) >= 0
    tri = keep.astype(F32)
    bg = bg_ref[...]
    c_st = [c_ref[h] for h in range(A_HEADS)]
    n_st = [n_ref[h:h + 1, :] for h in range(A_HEADS)]
    m_st = [m_ref[h:h + 1, 0:1] for h in range(A_HEADS)]

    rows, houts = [], []
    for jj in range(cb):
        r0 = pl.multiple_of(jnp.where(is_fwd, jj, cb - 1 - jj) * CHUNK, CHUNK)
        rows.append(r0)
        tsl = pl.ds(r0, CHUNK)
        gates = g_ref[tsl, :] + bg
        gates = jnp.where(is_fwd, gates, pltpu.roll(gates, V7X_LANES - A_HEADS, axis=1))
        logf = _log_sigmoid(gates)
        bcum = jnp.dot(tri, logf, preferred_element_type=F32, precision=lax.Precision.HIGHEST)
        bcum_t = bcum.T
        gates_t = gates.T
        gsum = jnp.sum(logf, axis=0, keepdims=True)
        hrow = []
        for h in range(A_HEADS):
            hs = slice(h * A_DH, (h + 1) * A_DH)
            fcol = 2 * A_HEADS + h
            q = q_ref[tsl, hs]
            k = k_ref[tsl, hs] * (A_DH ** -0.5)
            qb = q.astype(BF16)
            vb = v_ref[tsl, hs].astype(BF16)
            b_col = bcum[:, fcol:fcol + 1]
            b_row = bcum_t[fcol:fcol + 1, :]
            i_row = gates_t[h:h + 1, :]
            i_col = gates[:, h:h + 1]
            g_tot = gsum[:, fcol:fcol + 1]

            dmat = jnp.where(keep, b_col - b_row + i_row, -jnp.inf)
            inter = b_col + m_st[h]
            m_t = jnp.maximum(jnp.max(dmat, axis=-1, keepdims=True), inter)
            qk = lax.dot_general(qb, k.astype(BF16), (((1,), (1,)), ((), ())), preferred_element_type=F32)
            smat = qk * jnp.exp(dmat - m_t)
            a_in = jnp.exp(inter - m_t)
            num = jnp.dot(smat.astype(BF16), vb, preferred_element_type=F32) \
                + a_in * jnp.dot(qb, c_st[h].astype(BF16), preferred_element_type=F32)
            den = jnp.sum(smat, axis=-1, keepdims=True) + a_in * jnp.sum(q * n_st[h], axis=-1, keepdims=True)
            hrow.append(num / jnp.maximum(jnp.abs(den), jnp.exp(-m_t)))

            w_end = g_tot - b_col + i_col
            m_loc = jnp.max(w_end, axis=0, keepdims=True)
            ka = k * jnp.exp(w_end - m_loc)
            c_loc = lax.dot_general(ka.astype(BF16), vb, (((0,), (0,)), ((), ())), preferred_element_type=F32)
            n_loc = jnp.sum(ka, axis=0, keepdims=True)
            m_new = jnp.maximum(g_tot + m_st[h], m_loc)
            a_old = jnp.exp(g_tot + m_st[h] - m_new)
            a_new = jnp.exp(m_loc - m_new)
            c_st[h] = a_old * c_st[h] + a_new * c_loc
            n_st[h] = a_old * n_st[h] + a_new * n_loc
            m_st[h] = m_new
        houts.append(hrow)

    for h in range(A_HEADS):
        c_ref[h] = c_st[h]
        n_ref[h:h + 1, :] = n_st[h]
        m_ref[h:h + 1, :] = jnp.broadcast_to(m_st[h], (1, V7X_LANES))

    base = blk * (cb * CHUNK)

    @pl.when(is_fwd)
    def _():
        for jj in range(cb):
            for h in range(A_HEADS):
                hfw_ref[pl.ds(pl.multiple_of(base + rows[jj], CHUNK), CHUNK), h * A_DH:(h + 1) * A_DH] = houts[jj][h]

    @pl.when(jnp.logical_not(is_fwd))
    def _():
        for jj in range(cb):
            tsl = pl.ds(rows[jj], CHUNK)
            for h in range(A_HEADS):
                hs = slice(h * A_DH, (h + 1) * A_DH)
                tot = hfw_ref[pl.ds(pl.multiple_of(base + rows[jj], CHUNK), CHUNK), hs] + houts[jj][h]
                yh = _rms(tot, nw_ref[:, hs]) * _sigmoid(o_ref[tsl, hs])
                y_ref[tsl, hs] = (yh * _silu(z_ref[tsl, hs])).astype(BF16)


def _mlstm_call(p3, bg_pad, nw):
    t = _tiles()
    bsz, seq, _ = p3.shape
    nc = seq // CHUNK
    cb = min(MLSTM_BLOCK_CHUNKS, nc)
    assert nc % cb == 0
    nblk = nc // cb
    rows = cb * CHUNK

    def cur(s):
        return jnp.where(s < nblk, s, 2 * nblk - 1 - s)

    def late(s):
        return jnp.where(s < nblk, nblk - 1, 2 * nblk - 1 - s)

    def blk(col, which):
        return pl.BlockSpec((None, rows, BRANCH_W), lambda b, s: (b, which(s), col // BRANCH_W))

    return pl.pallas_call(
        functools.partial(_mlstm_kernel, nblk=nblk, cb=cb),
        grid=(bsz, 2 * nblk),
        in_specs=[
            blk(COL_AQ, cur), blk(COL_AK, cur), blk(COL_AV, cur),
            pl.BlockSpec((None, rows, V7X_LANES), lambda b, s: (b, cur(s), COL_AG // V7X_LANES)),
            blk(COL_AO, late), blk(COL_AZ, late),
            pl.BlockSpec((1, V7X_LANES), lambda b, s: (0, 0)),
            pl.BlockSpec((1, BRANCH_W), lambda b, s: (0, 0)),
        ],
        out_specs=pl.BlockSpec((None, rows, BRANCH_W), lambda b, s: (b, late(s), 0)),
        out_shape=jax.ShapeDtypeStruct((bsz, seq, BRANCH_W), BF16),
        scratch_shapes=[
            pltpu.VMEM((seq, BRANCH_W), F32),
            pltpu.VMEM((A_HEADS, A_DH, A_DH), F32),
            pltpu.VMEM((V7X_SUBLANES, A_DH), F32),
            pltpu.VMEM((V7X_SUBLANES, V7X_LANES), F32),
        ],
        compiler_params=pltpu.CompilerParams(
            dimension_semantics=("arbitrary", "arbitrary"), vmem_limit_bytes=t["vmem_big"]),
        name="mlstm",
    )(p3, p3, p3, p3, p3, p3, bg_pad, nw)


def _nattn_kernel(q_ref, kc_ref, kx_ref, vc_ref, vx_ref, z_ref, qw_ref, kw_ref, bias_ref, y_ref,
                  kn_ref, vn_ref, *, nblk):
    i = pl.program_id(1)

    def put(slot, k_src, v_src):
        dst = pl.ds(pl.multiple_of(slot * NA_BLOCK_TOK, NA_BLOCK_TOK), NA_BLOCK_TOK)
        for h in range(B_HEADS):
            hs = slice(h * B_DH, (h + 1) * B_DH)
            kn_ref[dst, hs] = _rms(k_src[:, hs], kw_ref[...]).astype(BF16)
        vn_ref[dst, :] = v_src[...].astype(BF16)

    @pl.when(i == 0)
    def _():
        put(0, kc_ref, vc_ref)

    nslot = (i + 1) % 3
    put(nslot, kx_ref, vx_ref)

    @pl.when((i == 0) | (nslot == 0))
    def _():
        mirror = slice(3 * NA_BLOCK_TOK, 4 * NA_BLOCK_TOK)
        kn_ref[mirror, :] = kn_ref[0:NA_BLOCK_TOK, :]
        vn_ref[mirror, :] = vn_ref[0:NA_BLOCK_TOK, :]

    qn = []
    for h in range(B_HEADS):
        hs = slice(h * B_DH, (h + 1) * B_DH)
        qn.append((_rms(q_ref[:, hs], qw_ref[...]) * (B_DH ** -0.5)).astype(BF16))

    half = NA_BLOCK_TOK // 2
    starts = (jnp.where(i == 0, 0, ((i + 2) % 3) * NA_BLOCK_TOK + half), (i % 3) * NA_BLOCK_TOK)
    variants = (jnp.where(i == 0, 1, 0), jnp.where(i == nblk - 1, 2, 0))
    for quad in range(NA_BLOCK_ROWS // NA_QUAD_ROWS):
        qrows = slice(quad * NA_QUAD_TOK, (quad + 1) * NA_QUAD_TOK)
        win = pl.ds(pl.multiple_of(starts[quad], half), NA_WIN_TOK)
        for h in range(B_HEADS):
            hs = slice(h * B_DH, (h + 1) * B_DH)
            sc = lax.dot_general(qn[h][qrows, :], kn_ref[win, hs], (((1,), (1,)), ((), ())),
                                 preferred_element_type=F32)
            sc = sc + bias_ref[variants[quad], h]
            m = jnp.max(sc, axis=-1, keepdims=True)
            p = jnp.exp(sc - m)
            l = jnp.sum(p, axis=-1, keepdims=True)
            o = jnp.dot(p.astype(BF16), vn_ref[win, hs], preferred_element_type=F32) / l
            y_ref[qrows, hs] = (o * _silu(z_ref[qrows, hs])).astype(BF16)


def _nattn_call(p3, qw, kw, bias_quads):
    t = _tiles()
    bsz, seq, _ = p3.shape
    rows = seq // GRID_W
    assert rows >= WIN_ROWS and rows % NA_BLOCK_ROWS == 0
    nblk = rows // NA_BLOCK_ROWS

    def blk(col, shift):
        cb = col // BRANCH_W
        return pl.BlockSpec((None, NA_BLOCK_TOK, BRANCH_W),
                            lambda b, i: (b, jnp.clip(i + shift, 0, nblk - 1), cb))

    return pl.pallas_call(
        functools.partial(_nattn_kernel, nblk=nblk),
        grid=(bsz, nblk),
        in_specs=[
            blk(COL_BQ, 0),
            blk(COL_BK, 0), blk(COL_BK, 1),
            blk(COL_BV, 0), blk(COL_BV, 1),
            blk(COL_BZ, 0),
            pl.BlockSpec((1, B_DH), lambda b, i: (0, 0)),
            pl.BlockSpec((1, B_DH), lambda b, i: (0, 0)),
            pl.BlockSpec(bias_quads.shape, lambda b, i: (0, 0, 0, 0)),
        ],
        out_specs=pl.BlockSpec((None, NA_BLOCK_TOK, BRANCH_W), lambda b, i: (b, i, 0)),
        out_shape=jax.ShapeDtypeStruct((bsz, seq, BRANCH_W), BF16),
        scratch_shapes=[
            pltpu.VMEM((NA_RING_SLOTS * NA_BLOCK_TOK, BRANCH_W), BF16),
            pltpu.VMEM((NA_RING_SLOTS * NA_BLOCK_TOK, BRANCH_W), BF16),
        ],
        compiler_params=pltpu.CompilerParams(
            dimension_semantics=("arbitrary", "arbitrary"), vmem_limit_bytes=t["vmem_big"]),
        name="nattn",
    )(p3, p3, p3, p3, p3, p3, qw, kw, bias_quads)


def _skew(w, nq, first, ncol):
    width = w.shape[-1]
    assert first - (nq - 1) >= 0 and first + ncol <= width - 1
    lead = w.shape[:-1]
    flat = jnp.tile(w, (1,) * len(lead) + (nq,))[..., :nq * (width - 1)]
    return flat.reshape(lead + (nq, width - 1))[..., first:first + ncol]


def _nattn_bias(rpb):
    heads = rpb.shape[0]
    reach = GRID_W - WIN_COLS
    w = jnp.pad(rpb.astype(F32), ((0, 0), (0, 0), (reach, reach + 1)))
    tbl = _skew(w, GRID_W, GRID_W - 1, GRID_W)
    qc = np.arange(GRID_W)[:, None]
    kc = np.arange(GRID_W)[None, :]
    qs = np.clip(qc - WIN_COLS // 2, 0, GRID_W - WIN_COLS)
    ok = (kc >= qs) & (kc < qs + WIN_COLS)
    tbl = jnp.where(ok[None, None], tbl, -jnp.inf)
    per_d = [jnp.concatenate([tbl[:, d + j] for j in range(WIN_ROWS)], axis=-1) for d in range(WIN_ROWS)]
    fill = lambda n: jnp.full((heads, GRID_W, n), -jnp.inf, F32)

    def quad(ds, shifts):
        rows = [jnp.concatenate([fill(GRID_W * sh), per_d[d], fill(NA_WIN_TOK - NA_BLOCK_TOK - GRID_W * sh)],
                                axis=-1) for d, sh in zip(ds, shifts)]
        return jnp.concatenate(rows, axis=1)

    mid = WIN_ROWS - 1 - WIN_ROWS // 2
    interior = quad((mid,) * NA_QUAD_ROWS, range(NA_QUAD_ROWS))
    top = quad(range(WIN_ROWS - 1, WIN_ROWS - 1 - NA_QUAD_ROWS, -1), (0,) * NA_QUAD_ROWS)
    bottom = quad(range(NA_QUAD_ROWS - 1, -1, -1), (0,) * NA_QUAD_ROWS)
    return jnp.stack([interior, top, bottom])


def _swa_kernel(q_ref, kp_ref, kc_ref, kx_ref, vp_ref, vc_ref, vx_ref, z_ref, qw_ref, kw_ref,
                sink_ref, bias_ref, y_ref, *, nb):
    n = pl.program_id(1)
    band = 3 * C_BLOCK
    lane = lax.broadcasted_iota(jnp.int32, (1, V7X_LANES), 1)
    lo = lane < C_DH

    def pair_rms(x, w):
        x2 = x * x
        s_lo = jnp.sum(jnp.where(lo, x2, 0.0), axis=-1, keepdims=True)
        s_hi = jnp.sum(jnp.where(lo, 0.0, x2), axis=-1, keepdims=True)
        r = lax.rsqrt(jnp.where(lo, s_lo, s_hi) * (1.0 / C_DH) + EPS)
        return x * r * w

    kn = pair_rms(jnp.concatenate([kp_ref[...], kc_ref[...], kx_ref[...]], axis=0), kw_ref[...])
    vc = jnp.concatenate([vp_ref[...], vc_ref[...], vx_ref[...]], axis=0)
    kn_sw = pltpu.roll(kn, C_DH, axis=1)
    vc_sw = pltpu.roll(vc, C_DH, axis=1)
    kk, vv = [], []
    for g in range(C_KV_HEADS):
        k_lo, k_hi = (kn, kn_sw) if g == 0 else (kn_sw, kn)
        v_lo, v_hi = (vc, vc_sw) if g == 0 else (vc_sw, vc)
        kk.append(jnp.concatenate([jnp.where(lo, k_lo, 0.0), jnp.where(lo, 0.0, k_hi)], axis=0).astype(BF16))
        vv.append(jnp.concatenate([jnp.where(lo, v_lo, 0.0), jnp.where(lo, 0.0, v_hi)], axis=0).astype(BF16))

    ki = lax.broadcasted_iota(jnp.int32, (1, 2 * band), 1)
    ki = jnp.where(ki >= band, ki - band, ki)
    valid = ((ki >= C_BLOCK) | (n > 0)) & ((ki < 2 * C_BLOCK) | (n < nb - 1))
    upper = lax.broadcasted_iota(jnp.int32, (2 * C_BLOCK, 1), 0) < C_BLOCK
    for g in range(C_KV_HEADS):
        slabs = [slice(p * V7X_LANES, (p + 1) * V7X_LANES) for p in (2 * g, 2 * g + 1)]
        qn = jnp.concatenate([(pair_rms(q_ref[:, sl], qw_ref[...]) * (C_DH ** -0.5)).astype(BF16)
                              for sl in slabs], axis=0)
        sc = lax.dot_general(qn, kk[g], (((1,), (1,)), ((), ())), preferred_element_type=F32)
        sc = jnp.where(valid, sc + bias_ref[g], -jnp.inf)
        probs, denoms = [], []
        for e in range(2):
            se = sc[:, e * band:(e + 1) * band]
            h0 = C_GROUP * g + e
            sink = jnp.where(upper, sink_ref[h0:h0 + 1, 0:1], sink_ref[h0 + 2:h0 + 3, 0:1])
            m = jnp.maximum(jnp.max(se, axis=-1, keepdims=True), sink)
            pe = jnp.exp(se - m)
            probs.append(pe.astype(BF16))
            denoms.append(jnp.sum(pe, axis=-1, keepdims=True) + jnp.exp(sink - m))
        o = jnp.dot(jnp.concatenate(probs, axis=1), vv[g], preferred_element_type=F32)
        o = o / jnp.where(lo, denoms[0], denoms[1])
        for j, sl in enumerate(slabs):
            rows = slice(j * C_BLOCK, (j + 1) * C_BLOCK)
            y_ref[:, sl] = (o[rows] * _silu(z_ref[:, sl])).astype(BF16)


def _swa_call(p3, qw, kw, sink_rows, bias):
    t = _tiles()
    bsz, seq, _ = p3.shape
    nb = seq // C_BLOCK

    def wide(col):
        return pl.BlockSpec((None, C_BLOCK, BRANCH_W), lambda b, n: (b, n, col // BRANCH_W))

    def kv(col, shift):
        return pl.BlockSpec((None, C_BLOCK, C_KV_W),
                            lambda b, n: (b, jnp.clip(n + shift, 0, nb - 1), col // C_KV_W))

    return pl.pallas_call(
        functools.partial(_swa_kernel, nb=nb),
        grid=(bsz, nb),
        in_specs=[
            wide(COL_CQ),
            kv(COL_CK, -1), kv(COL_CK, 0), kv(COL_CK, 1),
            kv(COL_CV, -1), kv(COL_CV, 0), kv(COL_CV, 1),
            wide(COL_CZ),
            pl.BlockSpec((1, V7X_LANES), lambda b, n: (0, 0)),
            pl.BlockSpec((1, V7X_LANES), lambda b, n: (0, 0)),
            pl.BlockSpec(sink_rows.shape, lambda b, n: (0, 0)),
            pl.BlockSpec(bias.shape, lambda b, n: (0, 0, 0)),
        ],
        out_specs=pl.BlockSpec((None, C_BLOCK, BRANCH_W), lambda b, n: (b, n, 0)),
        out_shape=jax.ShapeDtypeStruct((bsz, seq, BRANCH_W), BF16),
        compiler_params=pltpu.CompilerParams(
            dimension_semantics=("arbitrary", "arbitrary"), vmem_limit_bytes=t["vmem_small"]),
        name="swa",
    )(p3, p3, p3, p3, p3, p3, p3, p3, qw, kw, sink_rows, bias)


def _t5_bucket_rel():
    rel = np.arange(-(2 * C_BLOCK - 1), 2 * C_BLOCK)
    half = N_BUCKETS // 2
    max_exact = half // 2
    n = np.abs(rel)
    nf = np.maximum(n, 1).astype(np.float32)
    scale = np.float32(math.log(MAX_DIST / max_exact))
    large = max_exact + (np.log(nf / np.float32(max_exact)) / scale
                         * np.float32(half - max_exact)).astype(np.int32)
    large = np.minimum(large, half - 1)
    bucket = np.where(rel > 0, half, 0) + np.where(n < max_exact, n, large)
    return bucket, n <= C_WINDOW


def _swa_bias(rel_bias):
    bucket, in_window = _t5_bucket_rel()
    per_rel = jnp.where(in_window[:, None], rel_bias.astype(F32)[bucket], -jnp.inf)
    w = jnp.pad(per_rel.T, ((0, 0), (0, 1)))
    tbl = _skew(w, C_BLOCK, C_BLOCK - 1, 3 * C_BLOCK)
    slabs = jnp.concatenate([tbl[0::2], tbl[1::2]], axis=-1)
    return slabs.reshape(C_KV_HEADS, 2 * C_BLOCK, 2 * 3 * C_BLOCK)


def _conv_kernel(a_ref, g_ref, ap_ref, gp_ref, ax_ref, gx_ref, z_ref, cw_ref, cb_ref, lw_ref, lb_ref,
                 y_ref, ext_ref, sh_ref, *, ntile):
    i = pl.program_id(1)
    tt = a_ref.shape[0]
    prev = jnp.where(i > 0, ap_ref[...] * _sigmoid(gp_ref[...]), 0.0)
    nxt = jnp.where(i < ntile - 1, ax_ref[...] * _sigmoid(gx_ref[...]), 0.0)
    ext_ref[0:CONV_HALO, :] = prev
    ext_ref[CONV_HALO:CONV_HALO + tt, :] = a_ref[...] * _sigmoid(g_ref[...])
    ext_ref[CONV_HALO + tt:2 * CONV_HALO + tt, :] = nxt
    span = sh_ref.shape[1]
    for r in range(1, V7X_SUBLANES):
        sh_ref[r - 1] = ext_ref[pl.ds(r, span), :]
    acc = None
    for w in range(CONV_W):
        a, r = divmod(CONV_HALO - CONV_W // 2 + w, V7X_SUBLANES)
        lo = a * V7X_SUBLANES
        src = ext_ref[lo:lo + tt, :] if r == 0 else sh_ref[r - 1, lo:lo + tt, :]
        term = src * cw_ref[w:w + 1, :]
        acc = term if acc is None else acc + term
    u = acc + cb_ref[...]
    mu = jnp.mean(u, axis=-1, keepdims=True)
    var = jnp.mean(jnp.square(u - mu), axis=-1, keepdims=True)
    u = (u - mu) * lax.rsqrt(var + EPS) * lw_ref[...] + lb_ref[...]
    y_ref[...] = (_silu(u) * _silu(z_ref[...])).astype(BF16)


def _conv_call(p3, cw_pad, cb, lw, lb):
    t = _tiles()
    bsz, seq, _ = p3.shape
    tt = CONV_TILE
    ntile = seq // tt
    per = tt // CONV_HALO
    nhalo = seq // CONV_HALO
    span = tt + (2 * CONV_HALO - 1) // V7X_SUBLANES * V7X_SUBLANES

    def cur(col):
        return pl.BlockSpec((None, tt, BRANCH_W), lambda b, i: (b, i, col // BRANCH_W))

    def halo(col, after):
        def idx(b, i):
            r = (i + 1) * per if after else i * per - 1
            return (b, jnp.clip(r, 0, nhalo - 1), col // BRANCH_W)
        return pl.BlockSpec((None, CONV_HALO, BRANCH_W), idx)

    vec = pl.BlockSpec((1, BRANCH_W), lambda b, i: (0, 0))
    return pl.pallas_call(
        functools.partial(_conv_kernel, ntile=ntile),
        grid=(bsz, ntile),
        in_specs=[
            cur(COL_DA), cur(COL_DG),
            halo(COL_DA, False), halo(COL_DG, False), halo(COL_DA, True), halo(COL_DG, True),
            cur(COL_DZ),
            pl.BlockSpec(cw_pad.shape, lambda b, i: (0, 0)),
            vec, vec, vec,
        ],
        out_specs=pl.BlockSpec((None, tt, BRANCH_W), lambda b, i: (b, i, 0)),
        out_shape=jax.ShapeDtypeStruct((bsz, seq, BRANCH_W), BF16),
        scratch_shapes=[
            pltpu.VMEM((tt + 2 * CONV_HALO, BRANCH_W), F32),
            pltpu.VMEM((V7X_SUBLANES - 1, span, BRANCH_W), F32),
        ],
        compiler_params=pltpu.CompilerParams(
            dimension_semantics=("arbitrary", "arbitrary"), vmem_limit_bytes=t["vmem_small"]),
        name="conv",
    )(p3, p3, p3, p3, p3, p3, p3, cw_pad, cb, lw, lb)


def _mergeout_kernel(x_ref, nw_ref, scale_ref, shift_ref, gate_ref, ya_ref, yb_ref, yc_ref, yd_ref,
                     wm_ref, wb_ref, wo_ref, o_ref, h_ref, acc_ref):
    n = pl.program_id(1)

    @pl.when(n == 0)
    def _():
        h_ref[...] = _modulated(x_ref, nw_ref, scale_ref, shift_ref)
        acc_ref[...] = jnp.zeros_like(acc_ref)

    h = h_ref[...]
    merged = None
    for b, y_ref in enumerate((ya_ref, yb_ref, yc_ref, yd_ref)):
        gate = _sigmoid(jnp.dot(h, wm_ref[b], preferred_element_type=F32))
        term = gate * jnp.dot(y_ref[...], wb_ref[b], preferred_element_type=F32)
        merged = term if merged is None else merged + term
    acc_ref[...] += jnp.dot(merged.astype(BF16), wo_ref[...], preferred_element_type=F32)

    @pl.when(n == pl.num_programs(1) - 1)
    def _():
        o_ref[...] = x_ref[...] + gate_ref[...] * acc_ref[...]


def _mergeout_call(x2d, nw, mod_rows, ys, w_merge, w_branch, w_out, seq, first_row):
    t = _tiles()
    m, d = x2d.shape
    tm, tn = min(t["mo_tm"], seq), t["mo_tn"]
    row_of_tile = _row_of_tile(tm, seq, first_row)
    yspec = pl.BlockSpec((tm, BRANCH_W), lambda i, j: (i, 0))
    return pl.pallas_call(
        _mergeout_kernel,
        grid=(m // tm, d // tn),
        in_specs=[
            pl.BlockSpec((tm, d), lambda i, j: (i, 0)),
            pl.BlockSpec((1, d), lambda i, j: (0, 0)),
            _mod_spec(d, row_of_tile, 1),
            _mod_spec(d, row_of_tile, 0),
            _mod_spec(d, row_of_tile, 2),
            yspec, yspec, yspec, yspec,
            pl.BlockSpec((N_BRANCH, d, tn), lambda i, j: (0, 0, j)),
            pl.BlockSpec((N_BRANCH, BRANCH_W, tn), lambda i, j: (0, 0, j)),
            pl.BlockSpec((tn, d), lambda i, j: (j, 0)),
        ],
        out_specs=pl.BlockSpec((tm, d), lambda i, j: (i, 0)),
        out_shape=jax.ShapeDtypeStruct((m, d), F32),
        scratch_shapes=[pltpu.VMEM((tm, d), BF16), pltpu.VMEM((tm, d), F32)],
        compiler_params=pltpu.CompilerParams(
            dimension_semantics=("arbitrary", "arbitrary"), vmem_limit_bytes=t["vmem_big"]),
        name="mergeout",
    )(x2d, nw, mod_rows, mod_rows, mod_rows, *ys, w_merge, w_branch, w_out)


def _pack_w_in(w):
    d = w.shape[0]
    sizes = (512, 512, 512, 512, 512, A_GATES, 512, 512, 512, 512, 512, C_KV_W, C_KV_W, 512, 1024, 512)
    offs = np.concatenate([[0], np.cumsum(sizes)])
    (aq, ak, av, ao, az, ag, bq, bk, bv, bz, cq, ck, cv, cz, dglu, dz) = [
        w[:, int(offs[i]):int(offs[i + 1])] for i in range(len(sizes))]
    pad = jnp.zeros((d, PACK_COLS - COL_AG - A_GATES), w.dtype)
    packed = jnp.concatenate([aq, ak, av, ao, az, bq, bk, bv, bz, dglu, dz, cq, cz, ck, cv, ag, pad], axis=1)
    merge = w[:, int(offs[-1]):].reshape(d, N_BRANCH, d).transpose(1, 0, 2)
    return packed.astype(BF16), merge.astype(BF16)


def kernel(x_prompt, x_sample, c_prompt, c_sample, rel_bias, norm_w, w_ada, b_ada, w_in, b_gate, mlstm_norm_w, na_q_norm, na_k_norm, na_rpb, swa_q_norm, swa_k_norm, swa_sink, conv_w, conv_b, conv_ln_w, conv_ln_b, w_branch, w_out):
    depth, d = norm_w.shape
    groups = ((x_prompt, 0), (x_sample, c_prompt.shape[0]))
    n_cond = c_prompt.shape[0] + c_sample.shape[0]
    cond_rows = -(-n_cond // V7X_SUBLANES) * V7X_SUBLANES
    c_all = jnp.concatenate([c_prompt, c_sample, jnp.zeros((cond_rows - n_cond, d), F32)], axis=0)
    mod = _ada_call(c_all, w_ada, b_ada)
    mod_rows = mod.reshape(depth * cond_rows * 3, 1, d)

    swa_bias = _swa_bias(rel_bias)
    layers = []
    for l in range(depth):
        w_pack, w_merge = _pack_w_in(w_in[l])
        layers.append(dict(
            w_pack=w_pack, w_merge=w_merge,
            w_branch=w_branch[l].astype(BF16), w_out=w_out[l].astype(BF16),
            nw=norm_w[l].reshape(1, d),
            bg=jnp.pad(b_gate[l], (0, V7X_LANES - A_GATES)).reshape(1, V7X_LANES),
            mnw=mlstm_norm_w[l].reshape(1, BRANCH_W),
            na_qw=na_q_norm[l].reshape(1, B_DH), na_kw=na_k_norm[l].reshape(1, B_DH),
            na_bias=_nattn_bias(na_rpb[l]),
            swa_qw=jnp.tile(swa_q_norm[l], 2).reshape(1, V7X_LANES),
            swa_kw=jnp.tile(swa_k_norm[l], 2).reshape(1, V7X_LANES),
            sink=jnp.broadcast_to(swa_sink[l].reshape(C_HEADS, 1), (C_HEADS, V7X_LANES)),
            cw=jnp.pad(conv_w[l], ((0, 1), (0, 0))),
            cb=conv_b[l].reshape(1, BRANCH_W), lw=conv_ln_w[l].reshape(1, BRANCH_W),
            lb=conv_ln_b[l].reshape(1, BRANCH_W),
        ))

    outs = []
    for x, cond_off in groups:
        bsz, seq, _ = x.shape
        x2d = x.reshape(bsz * seq, d)
        for l, lw in enumerate(layers):
            first_row = l * cond_rows + cond_off
            p = _inproj_call(x2d, lw["nw"], mod_rows, lw["w_pack"], seq, first_row)
            p3 = p.reshape(bsz, seq, PACK_COLS)
            ya = _mlstm_call(p3, lw["bg"], lw["mnw"])
            yb = _nattn_call(p3, lw["na_qw"], lw["na_kw"], lw["na_bias"])
            yc = _swa_call(p3, lw["swa_qw"], lw["swa_kw"], lw["sink"], swa_bias)
            yd = _conv_call(p3, lw["cw"], lw["cb"], lw["lw"], lw["lb"])
            ys = [y.reshape(bsz * seq, BRANCH_W) for y in (ya, yb, yc, yd)]
            x2d = _mergeout_call(x2d, lw["nw"], mod_rows, ys, lw["w_merge"], lw["w_branch"], lw["w_out"],
                                 seq, first_row)
        outs.append(x2d.reshape(bsz, seq, d))
    return tuple(outs)
```

```python
import functools
import math

import numpy as np
import jax
import jax.numpy as jnp
from jax import lax
from jax.experimental import pallas as pl
from jax.experimental.pallas import tpu as pltpu

F32 = jnp.float32
BF16 = jnp.bfloat16

V7X_VMEM_BYTES = 64 * 1024 * 1024
V7X_LANES = 128
V7X_SUBLANES = 8
MIB = 1024 * 1024

EPS = 1e-6
N_BRANCH = 4
BRANCH_W = 512
GRID_W = 64
A_HEADS = 4
A_DH = BRANCH_W // A_HEADS
A_GATES = 4 * A_HEADS
CHUNK = 128
MLSTM_BLOCK_CHUNKS = 4
M_INIT = -1e30
B_HEADS = 4
B_DH = BRANCH_W // B_HEADS
WIN_ROWS = 8
WIN_COLS = 16
NA_BLOCK_ROWS = 8
NA_BLOCK_TOK = NA_BLOCK_ROWS * GRID_W
NA_QUAD_ROWS = 4
NA_QUAD_TOK = NA_QUAD_ROWS * GRID_W
NA_WIN_TOK = (WIN_ROWS + NA_QUAD_ROWS) * GRID_W
NA_RING_SLOTS = 4
C_HEADS = 8
C_KV_HEADS = 2
C_GROUP = C_HEADS // C_KV_HEADS
C_DH = BRANCH_W // C_HEADS
C_KV_W = C_KV_HEADS * C_DH
C_WINDOW = 128
C_BLOCK = 128
N_BUCKETS = 32
MAX_DIST = 128
CONV_W = 31
CONV_HALO = 16
CONV_TILE = 512

COL_AQ, COL_AK, COL_AV, COL_AO, COL_AZ = 0, 512, 1024, 1536, 2048
COL_BQ, COL_BK, COL_BV, COL_BZ = 2560, 3072, 3584, 4096
COL_DA, COL_DG, COL_DZ = 4608, 5120, 5632
COL_CQ, COL_CZ = 6144, 6656
COL_CK, COL_CV, COL_AG = 7168, 7296, 7424
PACK_COLS = 7680


def _tiles():
    return dict(
        ada_tn=512,
        in_tm=512, in_tn=1536,
        mo_tm=512, mo_tn=256,
        vmem_small=32 * MIB,
        vmem_big=V7X_VMEM_BYTES - 8 * MIB,
    )


def _sigmoid(x):
    return jax.nn.sigmoid(x)


def _silu(x):
    return x * jax.nn.sigmoid(x)


def _log_sigmoid(x):
    return jnp.minimum(x, 0.0) - jnp.log1p(jnp.exp(-jnp.abs(x)))


def _rms(x, w):
    r = lax.rsqrt(jnp.mean(x * x, axis=-1, keepdims=True) + EPS)
    return x * r * w


def _ada_kernel(c_ref, w_ref, b_ref, o_ref):
    c = c_ref[...]
    sc = _silu(c).astype(BF16)
    o_ref[...] = jnp.dot(sc, w_ref[...].astype(BF16), preferred_element_type=F32) + b_ref[...]


def _ada_call(c_all, w_ada, b_ada):
    t = _tiles()
    depth, d, n3 = w_ada.shape
    rows = c_all.shape[0]
    tn = t["ada_tn"]
    return pl.pallas_call(
        _ada_kernel,
        grid=(depth, n3 // tn),
        in_specs=[
            pl.BlockSpec((rows, d), lambda l, j: (0, 0)),
            pl.BlockSpec((None, d, tn), lambda l, j: (l, 0, j)),
            pl.BlockSpec((None, 1, tn), lambda l, j: (l, 0, j)),
        ],
        out_specs=pl.BlockSpec((None, rows, tn), lambda l, j: (l, 0, j)),
        out_shape=jax.ShapeDtypeStruct((depth, rows, n3), F32),
        compiler_params=pltpu.CompilerParams(
            dimension_semantics=("arbitrary", "arbitrary"), vmem_limit_bytes=t["vmem_small"]),
        name="ada",
    )(c_all, w_ada, b_ada.reshape(depth, 1, n3))


def _mod_spec(d, row_of_tile, kind):
    return pl.BlockSpec((None, 1, d), lambda i, j: (row_of_tile(i) * 3 + kind, 0, 0))


def _modulated(x_ref, nw_ref, scale_ref, shift_ref):
    x = x_ref[...]
    y = _rms(x, nw_ref[...])
    return (y * (1.0 + scale_ref[...]) + shift_ref[...]).astype(BF16)


def _inproj_kernel(x_ref, nw_ref, scale_ref, shift_ref, w_ref, o_ref, g_ref, h_ref, *, gate_tile, gate_col):
    j = pl.program_id(1)

    @pl.when(j == 0)
    def _():
        h_ref[...] = _modulated(x_ref, nw_ref, scale_ref, shift_ref)

    acc = jnp.dot(h_ref[...], w_ref[...], preferred_element_type=F32)
    o_ref[...] = acc.astype(o_ref.dtype)

    @pl.when(j == gate_tile)
    def _():
        g_ref[...] = acc[:, gate_col:gate_col + V7X_LANES]


def _row_of_tile(tm, seq, first_row):
    assert seq % tm == 0
    return lambda i: first_row + (i * tm) // seq


def _inproj_call(x2d, nw, mod_rows, w_pack, seq, first_row):
    t = _tiles()
    m, d = x2d.shape
    n = w_pack.shape[1]
    tm, tn = min(t["in_tm"], seq), t["in_tn"]
    row_of_tile = _row_of_tile(tm, seq, first_row)
    return pl.pallas_call(
        functools.partial(_inproj_kernel, gate_tile=COL_AG // tn, gate_col=COL_AG % tn),
        grid=(m // tm, n // tn),
        in_specs=[
            pl.BlockSpec((tm, d), lambda i, j: (i, 0)),
            pl.BlockSpec((1, d), lambda i, j: (0, 0)),
            _mod_spec(d, row_of_tile, 1),
            _mod_spec(d, row_of_tile, 0),
            pl.BlockSpec((d, tn), lambda i, j: (0, j)),
        ],
        out_specs=[pl.BlockSpec((tm, tn), lambda i, j: (i, j)),
                   pl.BlockSpec((tm, V7X_LANES), lambda i, j: (i, 0))],
        out_shape=[jax.ShapeDtypeStruct((m, n), BF16), jax.ShapeDtypeStruct((m, V7X_LANES), F32)],
        scratch_shapes=[pltpu.VMEM((tm, d), BF16)],
        compiler_params=pltpu.CompilerParams(
            dimension_semantics=("arbitrary", "arbitrary"), vmem_limit_bytes=t["vmem_big"]),
        name="inproj",
    )(x2d, nw, mod_rows, mod_rows, w_pack)


def _mlstm_kernel(q_ref, k_ref, v_ref, g_ref, o_ref, z_ref, bg_ref, nw_ref, y_ref,
                  hfw_ref, c_ref, n_ref, m_ref, *, nblk, cb):
    s = pl.program_id(1)
    is_fwd = s < nblk
    blk = jnp.where(is_fwd, s, 2 * nblk - 1 - s)
    sgn = jnp.where(is_fwd, 1, -1)

    @pl.when((s == 0) | (s == nblk))
    def _():
        c_ref[...] = jnp.zeros_like(c_ref)
        n_ref[...] = jnp.zeros_like(n_ref)
        m_ref[...] = jnp.full_like(m_ref, M_INIT)

    ri = lax.broadcasted_iota(jnp.int32, (CHUNK, CHUNK), 0)
    ci = lax.broadcasted_iota(jnp.int32, (CHUNK, CHUNK), 1)
    keep = ((ri - ci) * sgn) >= 0
    tri = keep.astype(F32)
    bg = bg_ref[...]
    hsl = [slice(h * A_DH, (h + 1) * A_DH) for h in range(A_HEADS)]
    c_st = [c_ref[:, hsl[h]] for h in range(A_HEADS)]
    n_st = [n_ref[0:1, hsl[h]] for h in range(A_HEADS)]
    m_st = [m_ref[h:h + 1, 0:1] for h in range(A_HEADS)]

    rows, houts = [], []
    for jj in range(cb):
        r0 = pl.multiple_of(jnp.where(is_fwd, jj, cb - 1 - jj) * CHUNK, CHUNK)
        rows.append(r0)
        tsl = pl.ds(r0, CHUNK)
        gt = g_ref[tsl, :] + bg
        gt = jnp.where(is_fwd, gt, pltpu.roll(gt, V7X_LANES - A_HEADS, axis=1))
        logf = _log_sigmoid(gt)
        bc = jnp.dot(tri, logf, preferred_element_type=F32, precision=lax.Precision.HIGHEST)
        bc_t = bc.T
        gt_t = gt.T
        g_row = jnp.sum(logf, axis=0, keepdims=True)
        hrow = []
        for h in range(A_HEADS):
            fcol = 2 * A_HEADS + h
            q = q_ref[tsl, hsl[h]].astype(F32)
            k = k_ref[tsl, hsl[h]].astype(F32) * (A_DH ** -0.5)
            qb = q.astype(BF16)
            vb = v_ref[tsl, hsl[h]].astype(BF16)
            b_col = bc[:, fcol:fcol + 1]
            b_row = bc_t[fcol:fcol + 1, :]
            i_row = gt_t[h:h + 1, :]
            i_col = gt[:, h:h + 1]
            g_tot = g_row[:, fcol:fcol + 1]

            dmat = jnp.where(keep, b_col - b_row + i_row, -jnp.inf)
            inter = b_col + m_st[h]
            m_t = jnp.maximum(jnp.max(dmat, axis=-1, keepdims=True), inter)
            qk = lax.dot_general(qb, k.astype(BF16), (((1,), (1,)), ((), ())), preferred_element_type=F32)
            smat = qk * jnp.exp(dmat - m_t)
            a_in = jnp.exp(inter - m_t)
            num = jnp.dot(smat.astype(BF16), vb, preferred_element_type=F32) \
                + a_in * jnp.dot(qb, c_st[h].astype(BF16), preferred_element_type=F32)
            den = jnp.sum(smat, axis=-1, keepdims=True) + a_in * jnp.sum(q * n_st[h], axis=-1, keepdims=True)
            hrow.append(num / jnp.maximum(jnp.abs(den), jnp.exp(-m_t)))

            w_end = g_tot - b_col + i_col
            m_loc = jnp.max(w_end, axis=0, keepdims=True)
            ka = k * jnp.exp(w_end - m_loc)
            c_loc = lax.dot_general(ka.astype(BF16), vb, (((0,), (0,)), ((), ())), preferred_element_type=F32)
            n_loc = jnp.sum(ka, axis=0, keepdims=True)
            m_new = jnp.maximum(g_tot + m_st[h], m_loc)
            a_old = jnp.exp(g_tot + m_st[h] - m_new)
            a_new = jnp.exp(m_loc - m_new)
            c_st[h] = a_old * c_st[h] + a_new * c_loc
            n_st[h] = a_old * n_st[h] + a_new * n_loc
            m_st[h] = m_new
        houts.append(hrow)

    for h in range(A_HEADS):
        c_ref[:, hsl[h]] = c_st[h]
        n_ref[0:1, hsl[h]] = n_st[h]
        m_ref[h:h + 1, :] = jnp.broadcast_to(m_st[h], (1, V7X_LANES))

    base = blk * (cb * CHUNK)

    @pl.when(is_fwd)
    def _():
        for jj in range(cb):
            for h in range(A_HEADS):
                hfw_ref[pl.ds(pl.multiple_of(base + rows[jj], CHUNK), CHUNK), h * A_DH:(h + 1) * A_DH] = houts[jj][h]

    @pl.when(jnp.logical_not(is_fwd))
    def _():
        for jj in range(cb):
            tsl = pl.ds(rows[jj], CHUNK)
            for h in range(A_HEADS):
                hs = slice(h * A_DH, (h + 1) * A_DH)
                tot = hfw_ref[pl.ds(pl.multiple_of(base + rows[jj], CHUNK), CHUNK), hs] + houts[jj][h]
                yh = _rms(tot, nw_ref[:, hs]) * _sigmoid(o_ref[tsl, hs].astype(F32))
                y_ref[tsl, hs] = (yh * _silu(z_ref[tsl, hs].astype(F32))).astype(BF16)


def _mlstm_call(p3, g3, bg_pad, nw):
    t = _tiles()
    bsz, seq, _ = p3.shape
    nc = seq // CHUNK
    cb = min(MLSTM_BLOCK_CHUNKS, nc)
    assert nc % cb == 0
    nblk = nc // cb
    rows = cb * CHUNK

    def cur(s):
        return jnp.where(s < nblk, s, 2 * nblk - 1 - s)

    def late(s):
        return jnp.where(s < nblk, nblk - 1, 2 * nblk - 1 - s)

    def blk(col, which):
        return pl.BlockSpec((None, rows, BRANCH_W), lambda b, s: (b, which(s), col // BRANCH_W))

    return pl.pallas_call(
        functools.partial(_mlstm_kernel, nblk=nblk, cb=cb),
        grid=(bsz, 2 * nblk),
        in_specs=[
            blk(COL_AQ, cur), blk(COL_AK, cur), blk(COL_AV, cur),
            pl.BlockSpec((None, rows, V7X_LANES), lambda b, s: (b, cur(s), 0)),
            blk(COL_AO, late), blk(COL_AZ, late),
            pl.BlockSpec((1, V7X_LANES), lambda b, s: (0, 0)),
            pl.BlockSpec((1, BRANCH_W), lambda b, s: (0, 0)),
        ],
        out_specs=pl.BlockSpec((None, rows, BRANCH_W), lambda b, s: (b, late(s), 0)),
        out_shape=jax.ShapeDtypeStruct((bsz, seq, BRANCH_W), BF16),
        scratch_shapes=[
            pltpu.VMEM((seq, BRANCH_W), F32),
            pltpu.VMEM((A_DH, BRANCH_W), F32),
            pltpu.VMEM((V7X_SUBLANES, BRANCH_W), F32),
            pltpu.VMEM((V7X_SUBLANES, V7X_LANES), F32),
        ],
        compiler_params=pltpu.CompilerParams(
            dimension_semantics=("arbitrary", "arbitrary"), vmem_limit_bytes=t["vmem_big"]),
        name="mlstm",
    )(p3, p3, p3, g3, p3, p3, bg_pad, nw)


def _nattn_kernel(q_ref, kc_ref, kx_ref, vc_ref, vx_ref, z_ref, qw_ref, kw_ref, bias_ref, y_ref,
                  kn_ref, vn_ref, *, nblk):
    i = pl.program_id(1)

    def put(slot, k_src, v_src):
        dst = pl.ds(pl.multiple_of(slot * NA_BLOCK_TOK, NA_BLOCK_TOK), NA_BLOCK_TOK)
        for h in range(B_HEADS):
            hs = slice(h * B_DH, (h + 1) * B_DH)
            kn_ref[dst, hs] = _rms(k_src[:, hs].astype(F32), kw_ref[...]).astype(BF16)
        vn_ref[dst, :] = v_src[...].astype(BF16)

    @pl.when(i == 0)
    def _():
        put(0, kc_ref, vc_ref)

    nslot = (i + 1) % 3
    put(nslot, kx_ref, vx_ref)

    @pl.when((i == 0) | (nslot == 0))
    def _():
        mirror = slice(3 * NA_BLOCK_TOK, 4 * NA_BLOCK_TOK)
        kn_ref[mirror, :] = kn_ref[0:NA_BLOCK_TOK, :]
        vn_ref[mirror, :] = vn_ref[0:NA_BLOCK_TOK, :]

    qn = []
    for h in range(B_HEADS):
        hs = slice(h * B_DH, (h + 1) * B_DH)
        qn.append((_rms(q_ref[:, hs].astype(F32), qw_ref[...]) * (B_DH ** -0.5)).astype(BF16))

    half = NA_BLOCK_TOK // 2
    starts = (jnp.where(i == 0, 0, ((i + 2) % 3) * NA_BLOCK_TOK + half), (i % 3) * NA_BLOCK_TOK)
    variants = (jnp.where(i == 0, 1, 0), jnp.where(i == nblk - 1, 2, 0))
    for quad in range(NA_BLOCK_ROWS // NA_QUAD_ROWS):
        qrows = slice(quad * NA_QUAD_TOK, (quad + 1) * NA_QUAD_TOK)
        win = pl.ds(pl.multiple_of(starts[quad], half), NA_WIN_TOK)
        for h in range(B_HEADS):
            hs = slice(h * B_DH, (h + 1) * B_DH)
            sc = lax.dot_general(qn[h][qrows, :], kn_ref[win, hs], (((1,), (1,)), ((), ())),
                                 preferred_element_type=F32)
            sc = sc + bias_ref[variants[quad], h]
            m = jnp.max(sc, axis=-1, keepdims=True)
            p = jnp.exp(sc - m)
            l = jnp.sum(p, axis=-1, keepdims=True)
            o = jnp.dot(p.astype(BF16), vn_ref[win, hs], preferred_element_type=F32) / l
            y_ref[qrows, hs] = (o * _silu(z_ref[qrows, hs].astype(F32))).astype(BF16)


def _nattn_call(p3, qw, kw, bias_quads):
    t = _tiles()
    bsz, seq, _ = p3.shape
    rows = seq // GRID_W
    assert rows >= WIN_ROWS and rows % NA_BLOCK_ROWS == 0
    nblk = rows // NA_BLOCK_ROWS

    def blk(col, shift):
        cb = col // BRANCH_W
        return pl.BlockSpec((None, NA_BLOCK_TOK, BRANCH_W),
                            lambda b, i: (b, jnp.clip(i + shift, 0, nblk - 1), cb))

    return pl.pallas_call(
        functools.partial(_nattn_kernel, nblk=nblk),
        grid=(bsz, nblk),
        in_specs=[
            blk(COL_BQ, 0),
            blk(COL_BK, 0), blk(COL_BK, 1),
            blk(COL_BV, 0), blk(COL_BV, 1),
            blk(COL_BZ, 0),
            pl.BlockSpec((1, B_DH), lambda b, i: (0, 0)),
            pl.BlockSpec((1, B_DH), lambda b, i: (0, 0)),
            pl.BlockSpec(bias_quads.shape, lambda b, i: (0, 0, 0, 0)),
        ],
        out_specs=pl.BlockSpec((None, NA_BLOCK_TOK, BRANCH_W), lambda b, i: (b, i, 0)),
        out_shape=jax.ShapeDtypeStruct((bsz, seq, BRANCH_W), BF16),
        scratch_shapes=[
            pltpu.VMEM((NA_RING_SLOTS * NA_BLOCK_TOK, BRANCH_W), BF16),
            pltpu.VMEM((NA_RING_SLOTS * NA_BLOCK_TOK, BRANCH_W), BF16),
        ],
        compiler_params=pltpu.CompilerParams(
            dimension_semantics=("arbitrary", "arbitrary"), vmem_limit_bytes=t["vmem_big"]),
        name="nattn",
    )(p3, p3, p3, p3, p3, p3, qw, kw, bias_quads)


def _skew(w, nq, first, ncol):
    width = w.shape[-1]
    assert first - (nq - 1) >= 0 and first + ncol <= width - 1
    lead = w.shape[:-1]
    flat = jnp.tile(w, (1,) * len(lead) + (nq,))[..., :nq * (width - 1)]
    return flat.reshape(lead + (nq, width - 1))[..., first:first + ncol]


def _nattn_bias(rpb):
    heads = rpb.shape[0]
    reach = GRID_W - WIN_COLS
    w = jnp.pad(rpb.astype(F32), ((0, 0), (0, 0), (reach, reach + 1)))
    tbl = _skew(w, GRID_W, GRID_W - 1, GRID_W)
    qc = np.arange(GRID_W)[:, None]
    kc = np.arange(GRID_W)[None, :]
    qs = np.clip(qc - WIN_COLS // 2, 0, GRID_W - WIN_COLS)
    ok = (kc >= qs) & (kc < qs + WIN_COLS)
    tbl = jnp.where(ok[None, None], tbl, -jnp.inf)
    per_d = [jnp.concatenate([tbl[:, d + j] for j in range(WIN_ROWS)], axis=-1) for d in range(WIN_ROWS)]
    fill = lambda n: jnp.full((heads, GRID_W, n), -jnp.inf, F32)

    def quad(ds, shifts):
        rows = [jnp.concatenate([fill(GRID_W * sh), per_d[d], fill(NA_WIN_TOK - NA_BLOCK_TOK - GRID_W * sh)],
                                axis=-1) for d, sh in zip(ds, shifts)]
        return jnp.concatenate(rows, axis=1)

    mid = WIN_ROWS - 1 - WIN_ROWS // 2
    interior = quad((mid,) * NA_QUAD_ROWS, range(NA_QUAD_ROWS))
    top = quad(range(WIN_ROWS - 1, WIN_ROWS - 1 - NA_QUAD_ROWS, -1), (0,) * NA_QUAD_ROWS)
    bottom = quad(range(NA_QUAD_ROWS - 1, -1, -1), (0,) * NA_QUAD_ROWS)
    return jnp.stack([interior, top, bottom])


def _swa_kernel(q_ref, kp_ref, kc_ref, kx_ref, vp_ref, vc_ref, vx_ref, z_ref, qw_ref, kw_ref,
                sink_ref, bias_ref, y_ref, *, nb):
    n = pl.program_id(1)
    band = 3 * C_BLOCK
    lane = lax.broadcasted_iota(jnp.int32, (1, V7X_LANES), 1)
    lo = lane < C_DH

    def pair_rms(x, w):
        x2 = x * x
        s_lo = jnp.sum(jnp.where(lo, x2, 0.0), axis=-1, keepdims=True)
        s_hi = jnp.sum(jnp.where(lo, 0.0, x2), axis=-1, keepdims=True)
        r = lax.rsqrt(jnp.where(lo, s_lo, s_hi) * (1.0 / C_DH) + EPS)
        return x * r * w

    kn = pair_rms(jnp.concatenate([kp_ref[...], kc_ref[...], kx_ref[...]], axis=0).astype(F32), kw_ref[...])
    vc = jnp.concatenate([vp_ref[...], vc_ref[...], vx_ref[...]], axis=0).astype(F32)
    kn_sw = pltpu.roll(kn, C_DH, axis=1)
    vc_sw = pltpu.roll(vc, C_DH, axis=1)
    kk, vv = [], []
    for g in range(C_KV_HEADS):
        k_lo, k_hi = (kn, kn_sw) if g == 0 else (kn_sw, kn)
        v_lo, v_hi = (vc, vc_sw) if g == 0 else (vc_sw, vc)
        kk.append(jnp.concatenate([jnp.where(lo, k_lo, 0.0), jnp.where(lo, 0.0, k_hi)], axis=0).astype(BF16))
        vv.append(jnp.concatenate([jnp.where(lo, v_lo, 0.0), jnp.where(lo, 0.0, v_hi)], axis=0).astype(BF16))

    ki = lax.broadcasted_iota(jnp.int32, (1, 2 * band), 1)
    ki = jnp.where(ki >= band, ki - band, ki)
    valid = ((ki >= C_BLOCK) | (n > 0)) & ((ki < 2 * C_BLOCK) | (n < nb - 1))
    upper = lax.broadcasted_iota(jnp.int32, (2 * C_BLOCK, 1), 0) < C_BLOCK
    for g in range(C_KV_HEADS):
        slabs = [slice(p * V7X_LANES, (p + 1) * V7X_LANES) for p in (2 * g, 2 * g + 1)]
        qn = jnp.concatenate([(pair_rms(q_ref[:, sl].astype(F32), qw_ref[...]) * (C_DH ** -0.5)).astype(BF16)
                              for sl in slabs], axis=0)
        sc = lax.dot_general(qn, kk[g], (((1,), (1,)), ((), ())), preferred_element_type=F32)
        sc = jnp.where(valid, sc + bias_ref[g], -jnp.inf)
        probs, denoms = [], []
        for e in range(2):
            se = sc[:, e * band:(e + 1) * band]
            h0 = C_GROUP * g + e
            sink = jnp.where(upper, sink_ref[h0:h0 + 1, 0:1], sink_ref[h0 + 2:h0 + 3, 0:1])
            m = jnp.maximum(jnp.max(se, axis=-1, keepdims=True), sink)
            pe = jnp.exp(se - m)
            probs.append(pe.astype(BF16))
            denoms.append(jnp.sum(pe, axis=-1, keepdims=True) + jnp.exp(sink - m))
        o = jnp.dot(jnp.concatenate(probs, axis=1), vv[g], preferred_element_type=F32)
        o = o / jnp.where(lo, denoms[0], denoms[1])
        for j, sl in enumerate(slabs):
            rows = slice(j * C_BLOCK, (j + 1) * C_BLOCK)
            y_ref[:, sl] = (o[rows] * _silu(z_ref[:, sl].astype(F32))).astype(BF16)


def _swa_call(p3, qw, kw, sink_rows, bias):
    t = _tiles()
    bsz, seq, _ = p3.shape
    nb = seq // C_BLOCK

    def wide(col):
        return pl.BlockSpec((None, C_BLOCK, BRANCH_W), lambda b, n: (b, n, col // BRANCH_W))

    def kv(col, shift):
        return pl.BlockSpec((None, C_BLOCK, C_KV_W),
                            lambda b, n: (b, jnp.clip(n + shift, 0, nb - 1), col // C_KV_W))

    return pl.pallas_call(
        functools.partial(_swa_kernel, nb=nb),
        grid=(bsz, nb),
        in_specs=[
            wide(COL_CQ),
            kv(COL_CK, -1), kv(COL_CK, 0), kv(COL_CK, 1),
            kv(COL_CV, -1), kv(COL_CV, 0), kv(COL_CV, 1),
            wide(COL_CZ),
            pl.BlockSpec((1, V7X_LANES), lambda b, n: (0, 0)),
            pl.BlockSpec((1, V7X_LANES), lambda b, n: (0, 0)),
            pl.BlockSpec(sink_rows.shape, lambda b, n: (0, 0)),
            pl.BlockSpec(bias.shape, lambda b, n: (0, 0, 0)),
        ],
        out_specs=pl.BlockSpec((None, C_BLOCK, BRANCH_W), lambda b, n: (b, n, 0)),
        out_shape=jax.ShapeDtypeStruct((bsz, seq, BRANCH_W), BF16),
        compiler_params=pltpu.CompilerParams(
            dimension_semantics=("arbitrary", "arbitrary"), vmem_limit_bytes=t["vmem_small"]),
        name="swa",
    )(p3, p3, p3, p3, p3, p3, p3, p3, qw, kw, sink_rows, bias)


def _t5_bucket_rel():
    rel = np.arange(-(2 * C_BLOCK - 1), 2 * C_BLOCK)
    half = N_BUCKETS // 2
    max_exact = half // 2
    n = np.abs(rel)
    nf = np.maximum(n, 1).astype(np.float32)
    scale = np.float32(math.log(MAX_DIST / max_exact))
    large = max_exact + (np.log(nf / np.float32(max_exact)) / scale
                         * np.float32(half - max_exact)).astype(np.int32)
    large = np.minimum(large, half - 1)
    bucket = np.where(rel > 0, half, 0) + np.where(n < max_exact, n, large)
    return bucket, n <= C_WINDOW


def _swa_bias(rel_bias):
    bucket, in_window = _t5_bucket_rel()
    per_rel = jnp.where(in_window[:, None], rel_bias.astype(F32)[bucket], -jnp.inf)
    w = jnp.pad(per_rel.T, ((0, 0), (0, 1)))
    tbl = _skew(w, C_BLOCK, C_BLOCK - 1, 3 * C_BLOCK)
    slabs = jnp.concatenate([tbl[0::2], tbl[1::2]], axis=-1)
    return slabs.reshape(C_KV_HEADS, 2 * C_BLOCK, 2 * 3 * C_BLOCK)


def _conv_kernel(a_ref, g_ref, ap_ref, gp_ref, ax_ref, gx_ref, z_ref, cw_ref, cb_ref, lw_ref, lb_ref,
                 y_ref, ext_ref, sh_ref, *, ntile):
    i = pl.program_id(1)
    tt = a_ref.shape[0]
    glu = lambda a, g: a[...].astype(F32) * _sigmoid(g[...].astype(F32))
    prev = jnp.where(i > 0, glu(ap_ref, gp_ref), 0.0)
    nxt = jnp.where(i < ntile - 1, glu(ax_ref, gx_ref), 0.0)
    ext_ref[0:CONV_HALO, :] = prev
    ext_ref[CONV_HALO:CONV_HALO + tt, :] = glu(a_ref, g_ref)
    ext_ref[CONV_HALO + tt:2 * CONV_HALO + tt, :] = nxt
    span = sh_ref.shape[1]
    for r in range(1, V7X_SUBLANES):
        sh_ref[r - 1] = ext_ref[pl.ds(r, span), :]
    acc = None
    for w in range(CONV_W):
        a, r = divmod(CONV_HALO - CONV_W // 2 + w, V7X_SUBLANES)
        lo = a * V7X_SUBLANES
        src = ext_ref[lo:lo + tt, :] if r == 0 else sh_ref[r - 1, lo:lo + tt, :]
        term = src * cw_ref[w:w + 1, :]
        acc = term if acc is None else acc + term
    u = acc + cb_ref[...]
    mu = jnp.mean(u, axis=-1, keepdims=True)
    var = jnp.mean(jnp.square(u - mu), axis=-1, keepdims=True)
    u = (u - mu) * lax.rsqrt(var + EPS) * lw_ref[...] + lb_ref[...]
    y_ref[...] = (_silu(u) * _silu(z_ref[...].astype(F32))).astype(BF16)


def _conv_call(p3, cw_pad, cb, lw, lb):
    t = _tiles()
    bsz, seq, _ = p3.shape
    tt = CONV_TILE
    ntile = seq // tt
    per = tt // CONV_HALO
    nhalo = seq // CONV_HALO
    span = tt + (2 * CONV_HALO - 1) // V7X_SUBLANES * V7X_SUBLANES

    def cur(col):
        return pl.BlockSpec((None, tt, BRANCH_W), lambda b, i: (b, i, col // BRANCH_W))

    def halo(col, after):
        def idx(b, i):
            r = (i + 1) * per if after else i * per - 1
            return (b, jnp.clip(r, 0, nhalo - 1), col // BRANCH_W)
        return pl.BlockSpec((None, CONV_HALO, BRANCH_W), idx)

    vec = pl.BlockSpec((1, BRANCH_W), lambda b, i: (0, 0))
    return pl.pallas_call(
        functools.partial(_conv_kernel, ntile=ntile),
        grid=(bsz, ntile),
        in_specs=[
            cur(COL_DA), cur(COL_DG),
            halo(COL_DA, False), halo(COL_DG, False), halo(COL_DA, True), halo(COL_DG, True),
            cur(COL_DZ),
            pl.BlockSpec(cw_pad.shape, lambda b, i: (0, 0)),
            vec, vec, vec,
        ],
        out_specs=pl.BlockSpec((None, tt, BRANCH_W), lambda b, i: (b, i, 0)),
        out_shape=jax.ShapeDtypeStruct((bsz, seq, BRANCH_W), BF16),
        scratch_shapes=[
            pltpu.VMEM((tt + 2 * CONV_HALO, BRANCH_W), F32),
            pltpu.VMEM((V7X_SUBLANES - 1, span, BRANCH_W), F32),
        ],
        compiler_params=pltpu.CompilerParams(
            dimension_semantics=("arbitrary", "arbitrary"), vmem_limit_bytes=t["vmem_small"]),
        name="conv",
    )(p3, p3, p3, p3, p3, p3, p3, cw_pad, cb, lw, lb)


def _mergeout_kernel(x_ref, nw_ref, scale_ref, shift_ref, gate_ref, ya_ref, yb_ref, yc_ref, yd_ref,
                     wma_ref, wmb_ref, wmc_ref, wmd_ref, wb_ref, wo_ref, o_ref, h_ref, acc_ref):
    n = pl.program_id(1)

    @pl.when(n == 0)
    def _():
        h_ref[...] = _modulated(x_ref, nw_ref, scale_ref, shift_ref)
        acc_ref[...] = jnp.zeros_like(acc_ref)

    h = h_ref[...]
    merged = None
    branches = ((ya_ref, wma_ref), (yb_ref, wmb_ref), (yc_ref, wmc_ref), (yd_ref, wmd_ref))
    for b, (y_ref, wm_ref) in enumerate(branches):
        gate = _sigmoid(jnp.dot(h, wm_ref[...], preferred_element_type=F32))
        term = gate * jnp.dot(y_ref[...], wb_ref[b], preferred_element_type=F32)
        merged = term if merged is None else merged + term
    acc_ref[...] += jnp.dot(merged.astype(BF16), wo_ref[...], preferred_element_type=F32)

    @pl.when(n == pl.num_programs(1) - 1)
    def _():
        o_ref[...] = x_ref[...] + gate_ref[...] * acc_ref[...]


def _mergeout_call(x2d, nw, mod_rows, ys, w_merge, w_branch, w_out, seq, first_row):
    t = _tiles()
    m, d = x2d.shape
    tm, tn = min(t["mo_tm"], seq), t["mo_tn"]
    row_of_tile = _row_of_tile(tm, seq, first_row)
    yspec = pl.BlockSpec((tm, BRANCH_W), lambda i, j: (i, 0))
    return pl.pallas_call(
        _mergeout_kernel,
        grid=(m // tm, d // tn),
        in_specs=[
            pl.BlockSpec((tm, d), lambda i, j: (i, 0)),
            pl.BlockSpec((1, d), lambda i, j: (0, 0)),
            _mod_spec(d, row_of_tile, 1),
            _mod_spec(d, row_of_tile, 0),
            _mod_spec(d, row_of_tile, 2),
            yspec, yspec, yspec, yspec,
            *[pl.BlockSpec((d, tn), functools.partial(lambda i, j, b: (0, b * (d // tn) + j), b=b))
              for b in range(N_BRANCH)],
            pl.BlockSpec((N_BRANCH, BRANCH_W, tn), lambda i, j: (0, 0, j)),
            pl.BlockSpec((tn, d), lambda i, j: (j, 0)),
        ],
        out_specs=pl.BlockSpec((tm, d), lambda i, j: (i, 0)),
        out_shape=jax.ShapeDtypeStruct((m, d), F32),
        scratch_shapes=[pltpu.VMEM((tm, d), BF16), pltpu.VMEM((tm, d), F32)],
        compiler_params=pltpu.CompilerParams(
            dimension_semantics=("arbitrary", "arbitrary"), vmem_limit_bytes=t["vmem_big"]),
        name="mergeout",
    )(x2d, nw, mod_rows, mod_rows, mod_rows, *ys, *([w_merge] * N_BRANCH), w_branch, w_out)


def _pack_w_in(w):
    d = w.shape[0]
    sizes = (512, 512, 512, 512, 512, A_GATES, 512, 512, 512, 512, 512, C_KV_W, C_KV_W, 512, 1024, 512)
    offs = np.concatenate([[0], np.cumsum(sizes)])
    (aq, ak, av, ao, az, ag, bq, bk, bv, bz, cq, ck, cv, cz, dglu, dz) = [
        w[:, int(offs[i]):int(offs[i + 1])] for i in range(len(sizes))]
    pad = jnp.zeros((d, PACK_COLS - COL_AG - A_GATES), w.dtype)
    packed = jnp.concatenate([aq, ak, av, ao, az, bq, bk, bv, bz, dglu, dz, cq, cz, ck, cv, ag, pad], axis=1)
    return packed.astype(BF16), w[:, int(offs[-1]):].astype(BF16)


def kernel(x_prompt, x_sample, c_prompt, c_sample, rel_bias, norm_w, w_ada, b_ada, w_in, b_gate, mlstm_norm_w, na_q_norm, na_k_norm, na_rpb, swa_q_norm, swa_k_norm, swa_sink, conv_w, conv_b, conv_ln_w, conv_ln_b, w_branch, w_out):
    depth, d = norm_w.shape
    groups = ((x_prompt, 0), (x_sample, c_prompt.shape[0]))
    n_cond = c_prompt.shape[0] + c_sample.shape[0]
    cond_rows = -(-n_cond // V7X_SUBLANES) * V7X_SUBLANES
    c_all = jnp.concatenate([c_prompt, c_sample, jnp.zeros((cond_rows - n_cond, d), F32)], axis=0)
    mod = _ada_call(c_all, w_ada, b_ada)
    mod_rows = mod.reshape(depth * cond_rows * 3, 1, d)

    swa_bias = _swa_bias(rel_bias)
    layers = []
    for l in range(depth):
        w_pack, w_merge = _pack_w_in(w_in[l])
        layers.append(dict(
            w_pack=w_pack, w_merge=w_merge,
            w_branch=w_branch[l].astype(BF16), w_out=w_out[l].astype(BF16),
            nw=norm_w[l].reshape(1, d),
            bg=jnp.pad(b_gate[l], (0, V7X_LANES - A_GATES)).reshape(1, V7X_LANES),
            mnw=mlstm_norm_w[l].reshape(1, BRANCH_W),
            na_qw=na_q_norm[l].reshape(1, B_DH), na_kw=na_k_norm[l].reshape(1, B_DH),
            na_bias=_nattn_bias(na_rpb[l]),
            swa_qw=jnp.tile(swa_q_norm[l], 2).reshape(1, V7X_LANES),
            swa_kw=jnp.tile(swa_k_norm[l], 2).reshape(1, V7X_LANES),
            sink=jnp.broadcast_to(swa_sink[l].reshape(C_HEADS, 1), (C_HEADS, V7X_LANES)),
            cw=jnp.pad(conv_w[l], ((0, 1), (0, 0))),
            cb=conv_b[l].reshape(1, BRANCH_W), lw=conv_ln_w[l].reshape(1, BRANCH_W),
            lb=conv_ln_b[l].reshape(1, BRANCH_W),
        ))

    outs = []
    for x, cond_off in groups:
        bsz, seq, _ = x.shape
        x2d = x.reshape(bsz * seq, d)
        for l, lw in enumerate(layers):
            first_row = l * cond_rows + cond_off
            p, gates = _inproj_call(x2d, lw["nw"], mod_rows, lw["w_pack"], seq, first_row)
            p3 = p.reshape(bsz, seq, PACK_COLS)
            ya = _mlstm_call(p3, gates.reshape(bsz, seq, V7X_LANES), lw["bg"], lw["mnw"])
            yb = _nattn_call(p3, lw["na_qw"], lw["na_kw"], lw["na_bias"])
            yc = _swa_call(p3, lw["swa_qw"], lw["swa_kw"], lw["sink"], swa_bias)
            yd = _conv_call(p3, lw["cw"], lw["cb"], lw["lw"], lw["lb"])
            ys = [y.reshape(bsz * seq, BRANCH_W) for y in (ya, yb, yc, yd)]
            x2d = _mergeout_call(x2d, lw["nw"], mod_rows, ys, lw["w_merge"], lw["w_branch"], lw["w_out"],
                                 seq, first_row)
        outs.append(x2d.reshape(bsz, seq, d))
    return tuple(outs)
```

```python
import functools
import math

import numpy as np
import jax
import jax.numpy as jnp
from jax import lax
from jax.experimental import pallas as pl
from jax.experimental.pallas import tpu as pltpu

F32 = jnp.float32
BF16 = jnp.bfloat16

V7X_VMEM_BYTES = 64 * 1024 * 1024
V7X_LANES = 128
V7X_SUBLANES = 8
MIB = 1024 * 1024

EPS = 1e-6
N_BRANCH = 4
BRANCH_W = 512
GRID_W = 64
A_HEADS = 4
A_DH = BRANCH_W // A_HEADS
A_GATES = 4 * A_HEADS
CHUNK = 128
MLSTM_BLOCK_CHUNKS = 8
M_INIT = -1e30
B_HEADS = 4
B_DH = BRANCH_W // B_HEADS
WIN_ROWS = 8
WIN_COLS = 16
NA_BLOCK_ROWS = 8
NA_BLOCK_TOK = NA_BLOCK_ROWS * GRID_W
NA_QUAD_ROWS = 4
NA_QUAD_TOK = NA_QUAD_ROWS * GRID_W
NA_WIN_TOK = (WIN_ROWS + NA_QUAD_ROWS) * GRID_W
NA_RING_SLOTS = 4
C_HEADS = 8
C_KV_HEADS = 2
C_GROUP = C_HEADS // C_KV_HEADS
C_DH = BRANCH_W // C_HEADS
C_KV_W = C_KV_HEADS * C_DH
C_WINDOW = 128
C_BLOCK = 128
SWA_STEP_BLOCKS = 1
N_BUCKETS = 32
MAX_DIST = 128
CONV_W = 31
CONV_HALO = 16
CONV_TILE = 512

COL_AQ, COL_AK, COL_AV, COL_AO, COL_AZ = 0, 512, 1024, 1536, 2048
COL_BQ, COL_BK, COL_BV, COL_BZ = 2560, 3072, 3584, 4096
COL_DA, COL_DG, COL_DZ = 4608, 5120, 5632
COL_CQ, COL_CZ = 6144, 6656
COL_CK, COL_CV, COL_AG = 7168, 7296, 7424
PACK_COLS = 7680


def _tiles():
    return dict(
        ada_tn=512,
        in_tm=512, in_tn=1536,
        mo_tm=512, mo_tn=256,
        vmem_small=32 * MIB,
        vmem_big=V7X_VMEM_BYTES - 8 * MIB,
    )


def _sigmoid(x):
    return jax.nn.sigmoid(x)


def _silu(x):
    return x * jax.nn.sigmoid(x)


def _log_sigmoid(x):
    return jnp.minimum(x, 0.0) - jnp.log1p(jnp.exp(-jnp.abs(x)))


def _rms(x, w):
    r = lax.rsqrt(jnp.mean(x * x, axis=-1, keepdims=True) + EPS)
    return x * r * w


def _ada_kernel(c_ref, w_ref, b_ref, o_ref):
    c = c_ref[...]
    sc = _silu(c).astype(BF16)
    o_ref[...] = jnp.dot(sc, w_ref[...].astype(BF16), preferred_element_type=F32) + b_ref[...]


def _ada_call(c_all, w_ada, b_ada):
    t = _tiles()
    depth, d, n3 = w_ada.shape
    rows = c_all.shape[0]
    tn = t["ada_tn"]
    return pl.pallas_call(
        _ada_kernel,
        grid=(depth, n3 // tn),
        in_specs=[
            pl.BlockSpec((rows, d), lambda l, j: (0, 0)),
            pl.BlockSpec((None, d, tn), lambda l, j: (l, 0, j)),
            pl.BlockSpec((None, 1, tn), lambda l, j: (l, 0, j)),
        ],
        out_specs=pl.BlockSpec((None, rows, tn), lambda l, j: (l, 0, j)),
        out_shape=jax.ShapeDtypeStruct((depth, rows, n3), F32),
        compiler_params=pltpu.CompilerParams(
            dimension_semantics=("arbitrary", "arbitrary"), vmem_limit_bytes=t["vmem_small"]),
        name="ada",
    )(c_all, w_ada, b_ada.reshape(depth, 1, n3))


def _mod_spec(d, row_of_tile, kind):
    return pl.BlockSpec((None, 1, d), lambda i, j: (row_of_tile(i) * 3 + kind, 0, 0))


def _modulated(x, nw_ref, scale_ref, shift_ref):
    y = _rms(x, nw_ref[...])
    return (y * (1.0 + scale_ref[...]) + shift_ref[...]).astype(BF16)


def _inproj_kernel(x0_ref, xn_ref, nw_ref, scale0_ref, shift0_ref, scalen_ref, shiftn_ref, w_ref,
                   o_ref, g_ref, hout_ref, ha_ref, hb_ref, *, gate_tile, gate_col, nchunk):
    i = pl.program_id(0)
    j = pl.program_id(1)

    @pl.when((i == 0) & (j == 0))
    def _():
        ha_ref[...] = _modulated(x0_ref[...], nw_ref, scale0_ref, shift0_ref)

    def step(h_cur, h_nxt):
        rows = h_nxt.shape[0] // nchunk
        rs = pl.ds(pl.multiple_of(jnp.minimum(j, nchunk - 1) * rows, rows), rows)
        h_nxt[rs, :] = _modulated(xn_ref[rs, :], nw_ref, scalen_ref, shiftn_ref)
        acc = jnp.dot(h_cur[...], w_ref[...], preferred_element_type=F32)
        o_ref[...] = acc.astype(o_ref.dtype)

        @pl.when(j == 0)
        def _():
            hout_ref[...] = h_cur[...]

        @pl.when(j == gate_tile)
        def _():
            g_ref[...] = acc[:, gate_col:gate_col + V7X_LANES]

    @pl.when(i % 2 == 0)
    def _():
        step(ha_ref, hb_ref)

    @pl.when(i % 2 == 1)
    def _():
        step(hb_ref, ha_ref)


def _row_of_tile(tm, seq, first_row):
    assert seq % tm == 0
    return lambda i: first_row + (i * tm) // seq


def _inproj_call(x2d, nw, mod_rows, w_pack, seq, first_row):
    t = _tiles()
    m, d = x2d.shape
    n = w_pack.shape[1]
    tm, tn = min(t["in_tm"], seq), t["in_tn"]
    ni, nj = m // tm, n // tn
    nchunk = max(c for c in (1, 2, 4, 8) if c <= nj)
    row_of_tile = _row_of_tile(tm, seq, first_row)
    nxt = lambda i: jnp.minimum(i + 1, ni - 1)
    return pl.pallas_call(
        functools.partial(_inproj_kernel, gate_tile=COL_AG // tn, gate_col=COL_AG % tn, nchunk=nchunk),
        grid=(ni, nj),
        in_specs=[
            pl.BlockSpec((tm, d), lambda i, j: (0, 0)),
            pl.BlockSpec((tm, d), lambda i, j: (nxt(i), 0)),
            pl.BlockSpec((1, d), lambda i, j: (0, 0)),
            _mod_spec(d, lambda i: row_of_tile(0), 1),
            _mod_spec(d, lambda i: row_of_tile(0), 0),
            _mod_spec(d, lambda i: row_of_tile(nxt(i)), 1),
            _mod_spec(d, lambda i: row_of_tile(nxt(i)), 0),
            pl.BlockSpec((d, tn), lambda i, j: (0, j)),
        ],
        out_specs=[pl.BlockSpec((tm, tn), lambda i, j: (i, j)),
                   pl.BlockSpec((tm, V7X_LANES), lambda i, j: (i, 0)),
                   pl.BlockSpec((tm, d), lambda i, j: (i, 0))],
        out_shape=[jax.ShapeDtypeStruct((m, n), BF16), jax.ShapeDtypeStruct((m, V7X_LANES), F32),
                   jax.ShapeDtypeStruct((m, d), BF16)],
        scratch_shapes=[pltpu.VMEM((tm, d), BF16), pltpu.VMEM((tm, d), BF16)],
        compiler_params=pltpu.CompilerParams(
            dimension_semantics=("arbitrary", "arbitrary"), vmem_limit_bytes=t["vmem_big"]),
        name="inproj",
    )(x2d, x2d, nw, mod_rows, mod_rows, mod_rows, mod_rows, w_pack)


def _mlstm_kernel(q_ref, k_ref, v_ref, g_ref, o_ref, z_ref, bg_ref, nw_ref, y_ref,
                  hfw_ref, c_ref, n_ref, m_ref, *, nblk, cb):
    s = pl.program_id(1)
    is_fwd = s < nblk
    blk = jnp.where(is_fwd, s, 2 * nblk - 1 - s)
    sgn = jnp.where(is_fwd, 1, -1)

    @pl.when((s == 0) | (s == nblk))
    def _():
        c_ref[...] = jnp.zeros_like(c_ref)
        n_ref[...] = jnp.zeros_like(n_ref)
        m_ref[...] = jnp.full_like(m_ref, M_INIT)

    ri = lax.broadcasted_iota(jnp.int32, (CHUNK, CHUNK), 0)
    ci = lax.broadcasted_iota(jnp.int32, (CHUNK, CHUNK), 1)
    keep = ((ri - ci) * sgn) >= 0
    tri = keep.astype(F32)
    bg = bg_ref[...]
    hsl = [slice(h * A_DH, (h + 1) * A_DH) for h in range(A_HEADS)]
    c_st = [c_ref[:, hsl[h]] for h in range(A_HEADS)]
    n_st = [n_ref[0:1, hsl[h]] for h in range(A_HEADS)]
    m_st = [m_ref[h:h + 1, 0:1] for h in range(A_HEADS)]

    rows, houts = [], []
    for jj in range(cb):
        r0 = pl.multiple_of(jnp.where(is_fwd, jj, cb - 1 - jj) * CHUNK, CHUNK)
        rows.append(r0)
        tsl = pl.ds(r0, CHUNK)
        gt = g_ref[tsl, :] + bg
        gt = jnp.where(is_fwd, gt, pltpu.roll(gt, V7X_LANES - A_HEADS, axis=1))
        logf = _log_sigmoid(gt)
        bc = jnp.dot(tri, logf, preferred_element_type=F32, precision=lax.Precision.HIGHEST)
        bc_t = bc.T
        gt_t = gt.T
        g_row = jnp.sum(logf, axis=0, keepdims=True)
        hrow = []
        for h in range(A_HEADS):
            fcol = 2 * A_HEADS + h
            q = q_ref[tsl, hsl[h]].astype(F32)
            k = k_ref[tsl, hsl[h]].astype(F32) * (A_DH ** -0.5)
            qb = q.astype(BF16)
            vb = v_ref[tsl, hsl[h]].astype(BF16)
            b_col = bc[:, fcol:fcol + 1]
            b_row = bc_t[fcol:fcol + 1, :]
            i_row = gt_t[h:h + 1, :]
            i_col = gt[:, h:h + 1]
            g_tot = g_row[:, fcol:fcol + 1]

            dmat = jnp.where(keep, b_col - b_row + i_row, -jnp.inf)
            inter = b_col + m_st[h]
            m_t = jnp.maximum(jnp.max(dmat, axis=-1, keepdims=True), inter)
            qk = lax.dot_general(qb, k.astype(BF16), (((1,), (1,)), ((), ())), preferred_element_type=F32)
            smat = qk * jnp.exp(dmat - m_t)
            a_in = jnp.exp(inter - m_t)
            num = jnp.dot(smat.astype(BF16), vb, preferred_element_type=F32) \
                + a_in * jnp.dot(qb, c_st[h].astype(BF16), preferred_element_type=F32)
            den = jnp.sum(smat, axis=-1, keepdims=True) + a_in * jnp.sum(q * n_st[h], axis=-1, keepdims=True)
            hrow.append(num / jnp.maximum(jnp.abs(den), jnp.exp(-m_t)))

            w_end = g_tot - b_col + i_col
            m_loc = jnp.max(w_end, axis=0, keepdims=True)
            ka = k * jnp.exp(w_end - m_loc)
            c_loc = lax.dot_general(ka.astype(BF16), vb, (((0,), (0,)), ((), ())), preferred_element_type=F32)
            n_loc = jnp.sum(ka, axis=0, keepdims=True)
            m_new = jnp.maximum(g_tot + m_st[h], m_loc)
            a_old = jnp.exp(g_tot + m_st[h] - m_new)
            a_new = jnp.exp(m_loc - m_new)
            c_st[h] = a_old * c_st[h] + a_new * c_loc
            n_st[h] = a_old * n_st[h] + a_new * n_loc
            m_st[h] = m_new
        houts.append(hrow)

    for h in range(A_HEADS):
        c_ref[:, hsl[h]] = c_st[h]
        n_ref[0:1, hsl[h]] = n_st[h]
        m_ref[h:h + 1, :] = jnp.broadcast_to(m_st[h], (1, V7X_LANES))

    base = blk * (cb * CHUNK)

    @pl.when(is_fwd)
    def _():
        for jj in range(cb):
            for h in range(A_HEADS):
                hfw_ref[pl.ds(pl.multiple_of(base + rows[jj], CHUNK), CHUNK), h * A_DH:(h + 1) * A_DH] = houts[jj][h]

    @pl.when(jnp.logical_not(is_fwd))
    def _():
        for jj in range(cb):
            tsl = pl.ds(rows[jj], CHUNK)
            for h in range(A_HEADS):
                hs = slice(h * A_DH, (h + 1) * A_DH)
                tot = hfw_ref[pl.ds(pl.multiple_of(base + rows[jj], CHUNK), CHUNK), hs] + houts[jj][h]
                yh = _rms(tot, nw_ref[:, hs]) * _sigmoid(o_ref[tsl, hs].astype(F32))
                y_ref[tsl, hs] = (yh * _silu(z_ref[tsl, hs].astype(F32))).astype(BF16)


def _mlstm_call(p3, g3, bg_pad, nw):
    t = _tiles()
    bsz, seq, _ = p3.shape
    nc = seq // CHUNK
    cb = min(MLSTM_BLOCK_CHUNKS, nc)
    assert nc % cb == 0
    nblk = nc // cb
    rows = cb * CHUNK

    def cur(s):
        return jnp.where(s < nblk, s, 2 * nblk - 1 - s)

    def late(s):
        return jnp.where(s < nblk, nblk - 1, 2 * nblk - 1 - s)

    def blk(col, which):
        return pl.BlockSpec((None, rows, BRANCH_W), lambda b, s: (b, which(s), col // BRANCH_W))

    return pl.pallas_call(
        functools.partial(_mlstm_kernel, nblk=nblk, cb=cb),
        grid=(bsz, 2 * nblk),
        in_specs=[
            blk(COL_AQ, cur), blk(COL_AK, cur), blk(COL_AV, cur),
            pl.BlockSpec((None, rows, V7X_LANES), lambda b, s: (b, cur(s), 0)),
            blk(COL_AO, late), blk(COL_AZ, late),
            pl.BlockSpec((1, V7X_LANES), lambda b, s: (0, 0)),
            pl.BlockSpec((1, BRANCH_W), lambda b, s: (0, 0)),
        ],
        out_specs=pl.BlockSpec((None, rows, BRANCH_W), lambda b, s: (b, late(s), 0)),
        out_shape=jax.ShapeDtypeStruct((bsz, seq, BRANCH_W), BF16),
        scratch_shapes=[
            pltpu.VMEM((seq, BRANCH_W), F32),
            pltpu.VMEM((A_DH, BRANCH_W), F32),
            pltpu.VMEM((V7X_SUBLANES, BRANCH_W), F32),
            pltpu.VMEM((V7X_SUBLANES, V7X_LANES), F32),
        ],
        compiler_params=pltpu.CompilerParams(
            dimension_semantics=("arbitrary", "arbitrary"), vmem_limit_bytes=t["vmem_big"]),
        name="mlstm",
    )(p3, p3, p3, g3, p3, p3, bg_pad, nw)


def _nattn_kernel(q_ref, kc_ref, kx_ref, vc_ref, vx_ref, z_ref, qw_ref, kw_ref, bias_ref, y_ref,
                  kn_ref, vn_ref, *, nblk):
    i = pl.program_id(1)

    def put(slot, k_src, v_src):
        dst = pl.ds(pl.multiple_of(slot * NA_BLOCK_TOK, NA_BLOCK_TOK), NA_BLOCK_TOK)
        for h in range(B_HEADS):
            hs = slice(h * B_DH, (h + 1) * B_DH)
            kn_ref[dst, hs] = _rms(k_src[:, hs].astype(F32), kw_ref[...]).astype(BF16)
        vn_ref[dst, :] = v_src[...].astype(BF16)

    @pl.when(i == 0)
    def _():
        put(0, kc_ref, vc_ref)

    nslot = (i + 1) % 3
    put(nslot, kx_ref, vx_ref)

    @pl.when((i == 0) | (nslot == 0))
    def _():
        mirror = slice(3 * NA_BLOCK_TOK, 4 * NA_BLOCK_TOK)
        kn_ref[mirror, :] = kn_ref[0:NA_BLOCK_TOK, :]
        vn_ref[mirror, :] = vn_ref[0:NA_BLOCK_TOK, :]

    qn = []
    for h in range(B_HEADS):
        hs = slice(h * B_DH, (h + 1) * B_DH)
        qn.append((_rms(q_ref[:, hs].astype(F32), qw_ref[...]) * (B_DH ** -0.5)).astype(BF16))

    half = NA_BLOCK_TOK // 2
    starts = (jnp.where(i == 0, 0, ((i + 2) % 3) * NA_BLOCK_TOK + half), (i % 3) * NA_BLOCK_TOK)
    variants = (jnp.where(i == 0, 1, 0), jnp.where(i == nblk - 1, 2, 0))
    for quad in range(NA_BLOCK_ROWS // NA_QUAD_ROWS):
        qrows = slice(quad * NA_QUAD_TOK, (quad + 1) * NA_QUAD_TOK)
        win = pl.ds(pl.multiple_of(starts[quad], half), NA_WIN_TOK)
        for h in range(B_HEADS):
            hs = slice(h * B_DH, (h + 1) * B_DH)
            sc = lax.dot_general(qn[h][qrows, :], kn_ref[win, hs], (((1,), (1,)), ((), ())),
                                 preferred_element_type=F32)
            sc = sc + bias_ref[variants[quad], h]
            m = jnp.max(sc, axis=-1, keepdims=True)
            p = jnp.exp(sc - m)
            l = jnp.sum(p, axis=-1, keepdims=True)
            o = jnp.dot(p.astype(BF16), vn_ref[win, hs], preferred_element_type=F32) / l
            y_ref[qrows, hs] = (o * _silu(z_ref[qrows, hs].astype(F32))).astype(BF16)


def _nattn_call(p3, qw, kw, bias_quads):
    t = _tiles()
    bsz, seq, _ = p3.shape
    rows = seq // GRID_W
    assert rows >= WIN_ROWS and rows % NA_BLOCK_ROWS == 0
    nblk = rows // NA_BLOCK_ROWS

    def blk(col, shift):
        cb = col // BRANCH_W
        return pl.BlockSpec((None, NA_BLOCK_TOK, BRANCH_W),
                            lambda b, i: (b, jnp.clip(i + shift, 0, nblk - 1), cb))

    return pl.pallas_call(
        functools.partial(_nattn_kernel, nblk=nblk),
        grid=(bsz, nblk),
        in_specs=[
            blk(COL_BQ, 0),
            blk(COL_BK, 0), blk(COL_BK, 1),
            blk(COL_BV, 0), blk(COL_BV, 1),
            blk(COL_BZ, 0),
            pl.BlockSpec((1, B_DH), lambda b, i: (0, 0)),
            pl.BlockSpec((1, B_DH), lambda b, i: (0, 0)),
            pl.BlockSpec(bias_quads.shape, lambda b, i: (0, 0, 0, 0)),
        ],
        out_specs=pl.BlockSpec((None, NA_BLOCK_TOK, BRANCH_W), lambda b, i: (b, i, 0)),
        out_shape=jax.ShapeDtypeStruct((bsz, seq, BRANCH_W), BF16),
        scratch_shapes=[
            pltpu.VMEM((NA_RING_SLOTS * NA_BLOCK_TOK, BRANCH_W), BF16),
            pltpu.VMEM((NA_RING_SLOTS * NA_BLOCK_TOK, BRANCH_W), BF16),
        ],
        compiler_params=pltpu.CompilerParams(
            dimension_semantics=("arbitrary", "arbitrary"), vmem_limit_bytes=t["vmem_big"]),
        name="nattn",
    )(p3, p3, p3, p3, p3, p3, qw, kw, bias_quads)


def _skew(w, nq, first, ncol):
    width = w.shape[-1]
    assert first - (nq - 1) >= 0 and first + ncol <= width - 1
    lead = w.shape[:-1]
    flat = jnp.tile(w, (1,) * len(lead) + (nq,))[..., :nq * (width - 1)]
    return flat.reshape(lead + (nq, width - 1))[..., first:first + ncol]


def _nattn_bias(rpb):
    heads = rpb.shape[0]
    reach = GRID_W - WIN_COLS
    w = jnp.pad(rpb.astype(F32), ((0, 0), (0, 0), (reach, reach + 1)))
    tbl = _skew(w, GRID_W, GRID_W - 1, GRID_W)
    qc = np.arange(GRID_W)[:, None]
    kc = np.arange(GRID_W)[None, :]
    qs = np.clip(qc - WIN_COLS // 2, 0, GRID_W - WIN_COLS)
    ok = (kc >= qs) & (kc < qs + WIN_COLS)
    tbl = jnp.where(ok[None, None], tbl, -jnp.inf)
    per_d = [jnp.concatenate([tbl[:, d + j] for j in range(WIN_ROWS)], axis=-1) for d in range(WIN_ROWS)]
    fill = lambda n: jnp.full((heads, GRID_W, n), -jnp.inf, F32)

    def quad(ds, shifts):
        rows = [jnp.concatenate([fill(GRID_W * sh), per_d[d], fill(NA_WIN_TOK - NA_BLOCK_TOK - GRID_W * sh)],
                                axis=-1) for d, sh in zip(ds, shifts)]
        return jnp.concatenate(rows, axis=1)

    mid = WIN_ROWS - 1 - WIN_ROWS // 2
    interior = quad((mid,) * NA_QUAD_ROWS, range(NA_QUAD_ROWS))
    top = quad(range(WIN_ROWS - 1, WIN_ROWS - 1 - NA_QUAD_ROWS, -1), (0,) * NA_QUAD_ROWS)
    bottom = quad(range(NA_QUAD_ROWS - 1, -1, -1), (0,) * NA_QUAD_ROWS)
    return jnp.stack([interior, top, bottom])


def _swa_kernel(q_ref, kp_ref, kc_ref, kx_ref, vp_ref, vc_ref, vx_ref, z_ref, qw_ref, kw_ref,
                sink_ref, bias_ref, y_ref, *, nsteps, nq):
    n = pl.program_id(1)
    band = 3 * C_BLOCK
    lane = lax.broadcasted_iota(jnp.int32, (1, V7X_LANES), 1)
    lo = lane < C_DH

    def pair_rms(x, w):
        x2 = x * x
        s_lo = jnp.sum(jnp.where(lo, x2, 0.0), axis=-1, keepdims=True)
        s_hi = jnp.sum(jnp.where(lo, 0.0, x2), axis=-1, keepdims=True)
        r = lax.rsqrt(jnp.where(lo, s_lo, s_hi) * (1.0 / C_DH) + EPS)
        return x * r * w

    kn = pair_rms(jnp.concatenate([kp_ref[...], kc_ref[...], kx_ref[...]], axis=0).astype(F32), kw_ref[...])
    vc = jnp.concatenate([vp_ref[...], vc_ref[...], vx_ref[...]], axis=0).astype(F32)
    kn_sw = pltpu.roll(kn, C_DH, axis=1)
    vc_sw = pltpu.roll(vc, C_DH, axis=1)
    stacks = []
    for g in range(C_KV_HEADS):
        k_lo, k_hi = (kn, kn_sw) if g == 0 else (kn_sw, kn)
        v_lo, v_hi = (vc, vc_sw) if g == 0 else (vc_sw, vc)
        stacks.append((jnp.where(lo, k_lo, 0.0).astype(BF16), jnp.where(lo, 0.0, k_hi).astype(BF16),
                       jnp.where(lo, v_lo, 0.0).astype(BF16), jnp.where(lo, 0.0, v_hi).astype(BF16)))

    ki = lax.broadcasted_iota(jnp.int32, (1, 2 * band), 1)
    ki = jnp.where(ki >= band, ki - band, ki)
    upper = lax.broadcasted_iota(jnp.int32, (2 * C_BLOCK, 1), 0) < C_BLOCK
    for qb in range(nq):
        qrows = slice(qb * C_BLOCK, (qb + 1) * C_BLOCK)
        krows = slice((nq + qb - 1) * C_BLOCK, (nq + qb + 2) * C_BLOCK)
        has_prev = (n > 0) if qb == 0 else True
        has_next = (n < nsteps - 1) if qb == nq - 1 else True
        valid = ((ki >= C_BLOCK) | has_prev) & ((ki < 2 * C_BLOCK) | has_next)
        for g in range(C_KV_HEADS):
            k_even, k_odd, v_even, v_odd = stacks[g]
            kk = jnp.concatenate([k_even[krows], k_odd[krows]], axis=0)
            vv = jnp.concatenate([v_even[krows], v_odd[krows]], axis=0)
            slabs = [slice(p * V7X_LANES, (p + 1) * V7X_LANES) for p in (2 * g, 2 * g + 1)]
            qn = jnp.concatenate(
                [(pair_rms(q_ref[qrows, sl].astype(F32), qw_ref[...]) * (C_DH ** -0.5)).astype(BF16)
                 for sl in slabs], axis=0)
            sc = lax.dot_general(qn, kk, (((1,), (1,)), ((), ())), preferred_element_type=F32)
            sc = jnp.where(valid, sc + bias_ref[g], -jnp.inf)
            probs, denoms = [], []
            for e in range(2):
                se = sc[:, e * band:(e + 1) * band]
                h0 = C_GROUP * g + e
                sink = jnp.where(upper, sink_ref[h0:h0 + 1, 0:1], sink_ref[h0 + 2:h0 + 3, 0:1])
                m = jnp.maximum(jnp.max(se, axis=-1, keepdims=True), sink)
                pe = jnp.exp(se - m)
                probs.append(pe.astype(BF16))
                denoms.append(jnp.sum(pe, axis=-1, keepdims=True) + jnp.exp(sink - m))
            o = jnp.dot(jnp.concatenate(probs, axis=1), vv, preferred_element_type=F32)
            o = o / jnp.where(lo, denoms[0], denoms[1])
            for j, sl in enumerate(slabs):
                rows = slice(j * C_BLOCK, (j + 1) * C_BLOCK)
                y_ref[qrows, sl] = (o[rows] * _silu(z_ref[qrows, sl].astype(F32))).astype(BF16)


def _swa_call(p3, qw, kw, sink_rows, bias):
    t = _tiles()
    bsz, seq, _ = p3.shape
    nq = SWA_STEP_BLOCKS
    rows = nq * C_BLOCK
    assert seq % rows == 0
    nsteps = seq // rows

    def wide(col):
        return pl.BlockSpec((None, rows, BRANCH_W), lambda b, n: (b, n, col // BRANCH_W))

    def kv(col, shift):
        return pl.BlockSpec((None, rows, C_KV_W),
                            lambda b, n: (b, jnp.clip(n + shift, 0, nsteps - 1), col // C_KV_W))

    return pl.pallas_call(
        functools.partial(_swa_kernel, nsteps=nsteps, nq=nq),
        grid=(bsz, nsteps),
        in_specs=[
            wide(COL_CQ),
            kv(COL_CK, -1), kv(COL_CK, 0), kv(COL_CK, 1),
            kv(COL_CV, -1), kv(COL_CV, 0), kv(COL_CV, 1),
            wide(COL_CZ),
            pl.BlockSpec((1, V7X_LANES), lambda b, n: (0, 0)),
            pl.BlockSpec((1, V7X_LANES), lambda b, n: (0, 0)),
            pl.BlockSpec(sink_rows.shape, lambda b, n: (0, 0)),
            pl.BlockSpec(bias.shape, lambda b, n: (0, 0, 0)),
        ],
        out_specs=pl.BlockSpec((None, rows, BRANCH_W), lambda b, n: (b, n, 0)),
        out_shape=jax.ShapeDtypeStruct((bsz, seq, BRANCH_W), BF16),
        compiler_params=pltpu.CompilerParams(
            dimension_semantics=("arbitrary", "arbitrary"), vmem_limit_bytes=t["vmem_small"]),
        name="swa",
    )(p3, p3, p3, p3, p3, p3, p3, p3, qw, kw, sink_rows, bias)


def _t5_bucket_rel():
    rel = np.arange(-(2 * C_BLOCK - 1), 2 * C_BLOCK)
    half = N_BUCKETS // 2
    max_exact = half // 2
    n = np.abs(rel)
    nf = np.maximum(n, 1).astype(np.float32)
    scale = np.float32(math.log(MAX_DIST / max_exact))
    large = max_exact + (np.log(nf / np.float32(max_exact)) / scale
                         * np.float32(half - max_exact)).astype(np.int32)
    large = np.minimum(large, half - 1)
    bucket = np.where(rel > 0, half, 0) + np.where(n < max_exact, n, large)
    return bucket, n <= C_WINDOW


def _swa_bias(rel_bias):
    bucket, in_window = _t5_bucket_rel()
    per_rel = jnp.where(in_window[:, None], rel_bias.astype(F32)[bucket], -jnp.inf)
    w = jnp.pad(per_rel.T, ((0, 0), (0, 1)))
    tbl = _skew(w, C_BLOCK, C_BLOCK - 1, 3 * C_BLOCK)
    slabs = jnp.concatenate([tbl[0::2], tbl[1::2]], axis=-1)
    return slabs.reshape(C_KV_HEADS, 2 * C_BLOCK, 2 * 3 * C_BLOCK)


def _conv_kernel(a_ref, g_ref, ap_ref, gp_ref, ax_ref, gx_ref, z_ref, cw_ref, cb_ref, lw_ref, lb_ref,
                 y_ref, ext_ref, sh_ref, *, ntile):
    i = pl.program_id(1)
    tt = a_ref.shape[0]
    glu = lambda a, g: a[...].astype(F32) * _sigmoid(g[...].astype(F32))
    prev = jnp.where(i > 0, glu(ap_ref, gp_ref), 0.0)
    nxt = jnp.where(i < ntile - 1, glu(ax_ref, gx_ref), 0.0)
    ext_ref[0:CONV_HALO, :] = prev
    ext_ref[CONV_HALO:CONV_HALO + tt, :] = glu(a_ref, g_ref)
    ext_ref[CONV_HALO + tt:2 * CONV_HALO + tt, :] = nxt
    span = sh_ref.shape[1]
    for r in range(1, V7X_SUBLANES):
        sh_ref[r - 1] = ext_ref[pl.ds(r, span), :]
    acc = None
    for w in range(CONV_W):
        a, r = divmod(CONV_HALO - CONV_W // 2 + w, V7X_SUBLANES)
        lo = a * V7X_SUBLANES
        src = ext_ref[lo:lo + tt, :] if r == 0 else sh_ref[r - 1, lo:lo + tt, :]
        term = src * cw_ref[w:w + 1, :]
        acc = term if acc is None else acc + term
    u = acc + cb_ref[...]
    mu = jnp.mean(u, axis=-1, keepdims=True)
    var = jnp.mean(jnp.square(u - mu), axis=-1, keepdims=True)
    u = (u - mu) * lax.rsqrt(var + EPS) * lw_ref[...] + lb_ref[...]
    y_ref[...] = (_silu(u) * _silu(z_ref[...].astype(F32))).astype(BF16)


def _conv_call(p3, cw_pad, cb, lw, lb):
    t = _tiles()
    bsz, seq, _ = p3.shape
    tt = CONV_TILE
    ntile = seq // tt
    per = tt // CONV_HALO
    nhalo = seq // CONV_HALO
    span = tt + (2 * CONV_HALO - 1) // V7X_SUBLANES * V7X_SUBLANES

    def cur(col):
        return pl.BlockSpec((None, tt, BRANCH_W), lambda b, i: (b, i, col // BRANCH_W))

    def halo(col, after):
        def idx(b, i):
            r = (i + 1) * per if after else i * per - 1
            return (b, jnp.clip(r, 0, nhalo - 1), col // BRANCH_W)
        return pl.BlockSpec((None, CONV_HALO, BRANCH_W), idx)

    vec = pl.BlockSpec((1, BRANCH_W), lambda b, i: (0, 0))
    return pl.pallas_call(
        functools.partial(_conv_kernel, ntile=ntile),
        grid=(bsz, ntile),
        in_specs=[
            cur(COL_DA), cur(COL_DG),
            halo(COL_DA, False), halo(COL_DG, False), halo(COL_DA, True), halo(COL_DG, True),
            cur(COL_DZ),
            pl.BlockSpec(cw_pad.shape, lambda b, i: (0, 0)),
            vec, vec, vec,
        ],
        out_specs=pl.BlockSpec((None, tt, BRANCH_W), lambda b, i: (b, i, 0)),
        out_shape=jax.ShapeDtypeStruct((bsz, seq, BRANCH_W), BF16),
        scratch_shapes=[
            pltpu.VMEM((tt + 2 * CONV_HALO, BRANCH_W), F32),
            pltpu.VMEM((V7X_SUBLANES - 1, span, BRANCH_W), F32),
        ],
        compiler_params=pltpu.CompilerParams(
            dimension_semantics=("arbitrary", "arbitrary"), vmem_limit_bytes=t["vmem_small"]),
        name="conv",
    )(p3, p3, p3, p3, p3, p3, p3, cw_pad, cb, lw, lb)


def _mergeout_kernel(x_ref, h_ref, gate_ref, ya_ref, yb_ref, yc_ref, yd_ref,
                     wma_ref, wmb_ref, wmc_ref, wmd_ref, wb_ref, wo_ref, o_ref, mg_ref):
    n = pl.program_id(1)
    h = h_ref[...]
    merged = None
    branches = ((ya_ref, wma_ref), (yb_ref, wmb_ref), (yc_ref, wmc_ref), (yd_ref, wmd_ref))
    for b, (y_ref, wm_ref) in enumerate(branches):
        gate = _sigmoid(jnp.dot(h, wm_ref[...], preferred_element_type=F32))
        term = gate * jnp.dot(y_ref[...], wb_ref[b], preferred_element_type=F32)
        merged = term if merged is None else merged + term
    mg_ref[n] = merged.astype(BF16)

    @pl.when(n == pl.num_programs(1) - 1)
    def _():
        full = jnp.concatenate([mg_ref[k] for k in range(mg_ref.shape[0])], axis=1)
        o_ref[...] = x_ref[...] + gate_ref[...] * jnp.dot(full, wo_ref[...], preferred_element_type=F32)


def _mergeout_call(x2d, h2d, mod_rows, ys, w_merge, w_branch, w_out, seq, first_row):
    t = _tiles()
    m, d = x2d.shape
    tm, tn = min(t["mo_tm"], seq), t["mo_tn"]
    row_of_tile = _row_of_tile(tm, seq, first_row)
    yspec = pl.BlockSpec((tm, BRANCH_W), lambda i, j: (i, 0))
    return pl.pallas_call(
        _mergeout_kernel,
        grid=(m // tm, d // tn),
        in_specs=[
            pl.BlockSpec((tm, d), lambda i, j: (i, 0)),
            pl.BlockSpec((tm, d), lambda i, j: (i, 0)),
            _mod_spec(d, row_of_tile, 2),
            yspec, yspec, yspec, yspec,
            *[pl.BlockSpec((d, tn), functools.partial(lambda i, j, b: (0, b * (d // tn) + j), b=b))
              for b in range(N_BRANCH)],
            pl.BlockSpec((N_BRANCH, BRANCH_W, tn), lambda i, j: (0, 0, j)),
            pl.BlockSpec((d, d), lambda i, j: (0, 0), pipeline_mode=pl.Buffered(1)),
        ],
        out_specs=pl.BlockSpec((tm, d), lambda i, j: (i, 0)),
        out_shape=jax.ShapeDtypeStruct((m, d), F32),
        scratch_shapes=[pltpu.VMEM((d // tn, tm, tn), BF16)],
        compiler_params=pltpu.CompilerParams(
            dimension_semantics=("arbitrary", "arbitrary"), vmem_limit_bytes=t["vmem_big"]),
        name="mergeout",
    )(x2d, h2d, mod_rows, *ys, *([w_merge] * N_BRANCH), w_branch, w_out)


def _pack_w_in(w):
    d = w.shape[0]
    sizes = (512, 512, 512, 512, 512, A_GATES, 512, 512, 512, 512, 512, C_KV_W, C_KV_W, 512, 1024, 512)
    offs = np.concatenate([[0], np.cumsum(sizes)])
    (aq, ak, av, ao, az, ag, bq, bk, bv, bz, cq, ck, cv, cz, dglu, dz) = [
        w[:, int(offs[i]):int(offs[i + 1])] for i in range(len(sizes))]
    pad = jnp.zeros((d, PACK_COLS - COL_AG - A_GATES), w.dtype)
    packed = jnp.concatenate([aq, ak, av, ao, az, bq, bk, bv, bz, dglu, dz, cq, cz, ck, cv, ag, pad], axis=1)
    return packed.astype(BF16), w[:, int(offs[-1]):].astype(BF16)


def kernel(x_prompt, x_sample, c_prompt, c_sample, rel_bias, norm_w, w_ada, b_ada, w_in, b_gate, mlstm_norm_w, na_q_norm, na_k_norm, na_rpb, swa_q_norm, swa_k_norm, swa_sink, conv_w, conv_b, conv_ln_w, conv_ln_b, w_branch, w_out):
    depth, d = norm_w.shape
    groups = ((x_prompt, 0), (x_sample, c_prompt.shape[0]))
    n_cond = c_prompt.shape[0] + c_sample.shape[0]
    cond_rows = -(-n_cond // V7X_SUBLANES) * V7X_SUBLANES
    c_all = jnp.concatenate([c_prompt, c_sample, jnp.zeros((cond_rows - n_cond, d), F32)], axis=0)
    mod = _ada_call(c_all, w_ada, b_ada)
    mod_rows = mod.reshape(depth * cond_rows * 3, 1, d)

    swa_bias = _swa_bias(rel_bias)
    layers = []
    for l in range(depth):
        w_pack, w_merge = _pack_w_in(w_in[l])
        layers.append(dict(
            w_pack=w_pack, w_merge=w_merge,
            w_branch=w_branch[l].astype(BF16), w_out=w_out[l].astype(BF16),
            nw=norm_w[l].reshape(1, d),
            bg=jnp.pad(b_gate[l], (0, V7X_LANES - A_GATES)).reshape(1, V7X_LANES),
            mnw=mlstm_norm_w[l].reshape(1, BRANCH_W),
            na_qw=na_q_norm[l].reshape(1, B_DH), na_kw=na_k_norm[l].reshape(1, B_DH),
            na_bias=_nattn_bias(na_rpb[l]),
            swa_qw=jnp.tile(swa_q_norm[l], 2).reshape(1, V7X_LANES),
            swa_kw=jnp.tile(swa_k_norm[l], 2).reshape(1, V7X_LANES),
            sink=jnp.broadcast_to(swa_sink[l].reshape(C_HEADS, 1), (C_HEADS, V7X_LANES)),
            cw=jnp.pad(conv_w[l], ((0, 1), (0, 0))),
            cb=conv_b[l].reshape(1, BRANCH_W), lw=conv_ln_w[l].reshape(1, BRANCH_W),
            lb=conv_ln_b[l].reshape(1, BRANCH_W),
        ))

    outs = []
    for x, cond_off in groups:
        bsz, seq, _ = x.shape
        x2d = x.reshape(bsz * seq, d)
        for l, lw in enumerate(layers):
            first_row = l * cond_rows + cond_off
            p, gates, h2d = _inproj_call(x2d, lw["nw"], mod_rows, lw["w_pack"], seq, first_row)
            p3 = p.reshape(bsz, seq, PACK_COLS)
            ya = _mlstm_call(p3, gates.reshape(bsz, seq, V7X_LANES), lw["bg"], lw["mnw"])
            yb = _nattn_call(p3, lw["na_qw"], lw["na_kw"], lw["na_bias"])
            yc = _swa_call(p3, lw["swa_qw"], lw["swa_kw"], lw["sink"], swa_bias)
            yd = _conv_call(p3, lw["cw"], lw["cb"], lw["lw"], lw["lb"])
            ys = [y.reshape(bsz * seq, BRANCH_W) for y in (ya, yb, yc, yd)]
            x2d = _mergeout_call(x2d, h2d, mod_rows, ys, lw["w_merge"], lw["w_branch"], lw["w_out"],
                                 seq, first_row)
        outs.append(x2d.reshape(bsz, seq, d))
    return tuple(outs)
```

```python
import functools
import math

import numpy as np
import jax
import jax.numpy as jnp
from jax import lax
from jax.experimental import pallas as pl
from jax.experimental.pallas import tpu as pltpu

F32 = jnp.float32
BF16 = jnp.bfloat16

V7X_VMEM_BYTES = 64 * 1024 * 1024
V7X_LANES = 128
V7X_SUBLANES = 8
MIB = 1024 * 1024

EPS = 1e-6
N_BRANCH = 4
BRANCH_W = 512
GRID_W = 64
A_HEADS = 4
A_DH = BRANCH_W // A_HEADS
A_GATES = 4 * A_HEADS
CHUNK = 128
MLSTM_BLOCK_CHUNKS = 8
M_INIT = -1e30
B_HEADS = 4
B_DH = BRANCH_W // B_HEADS
WIN_ROWS = 8
WIN_COLS = 16
NA_BLOCK_ROWS = 8
NA_BLOCK_TOK = NA_BLOCK_ROWS * GRID_W
NA_QUAD_ROWS = 4
NA_QUAD_TOK = NA_QUAD_ROWS * GRID_W
NA_WIN_TOK = (WIN_ROWS + NA_QUAD_ROWS) * GRID_W
NA_RING_SLOTS = 4
C_HEADS = 8
C_KV_HEADS = 2
C_GROUP = C_HEADS // C_KV_HEADS
C_DH = BRANCH_W // C_HEADS
C_KV_W = C_KV_HEADS * C_DH
C_WINDOW = 128
C_BLOCK = 128
SWA_STEP_BLOCKS = 1
N_BUCKETS = 32
MAX_DIST = 128
CONV_W = 31
CONV_HALO = 16
CONV_TILE = 512

COL_AQ, COL_AK, COL_AV, COL_AO, COL_AZ = 0, 512, 1024, 1536, 2048
COL_BQ, COL_BK, COL_BV, COL_BZ = 2560, 3072, 3584, 4096
COL_DA, COL_DG, COL_DZ = 4608, 5120, 5632
COL_CQ, COL_CZ = 6144, 6656
COL_CK, COL_CV, COL_AG = 7168, 7296, 7424
PACK_COLS = 7680


def _tiles():
    return dict(
        ada_tn=1536,
        in_tm=1024, in_tn=1280,
        mg_tm=1024, mg_tn=512,
        op_tm=1024, op_tn=1024,
        vmem_small=32 * MIB,
        vmem_big=V7X_VMEM_BYTES - 8 * MIB,
    )


def _sigmoid(x):
    return jax.nn.sigmoid(x)


def _silu(x):
    return x * jax.nn.sigmoid(x)


def _log_sigmoid(x):
    return jnp.minimum(x, 0.0) - jnp.log1p(jnp.exp(-jnp.abs(x)))


def _rms(x, w):
    r = lax.rsqrt(jnp.mean(x * x, axis=-1, keepdims=True) + EPS)
    return x * r * w


def _ada_kernel(c_ref, w_ref, b_ref, o_ref):
    c = c_ref[...]
    sc = _silu(c).astype(BF16)
    o_ref[...] = jnp.dot(sc, w_ref[...].astype(BF16), preferred_element_type=F32) + b_ref[...]


def _ada_call(c_all, w_ada, b_ada):
    t = _tiles()
    depth, d, n3 = w_ada.shape
    rows = c_all.shape[0]
    tn = t["ada_tn"]
    return pl.pallas_call(
        _ada_kernel,
        grid=(depth, n3 // tn),
        in_specs=[
            pl.BlockSpec((rows, d), lambda l, j: (0, 0)),
            pl.BlockSpec((None, d, tn), lambda l, j: (l, 0, j)),
            pl.BlockSpec((None, 1, tn), lambda l, j: (l, 0, j)),
        ],
        out_specs=pl.BlockSpec((None, rows, tn), lambda l, j: (l, 0, j)),
        out_shape=jax.ShapeDtypeStruct((depth, rows, n3), F32),
        compiler_params=pltpu.CompilerParams(
            dimension_semantics=("arbitrary", "arbitrary"), vmem_limit_bytes=t["vmem_small"]),
        name="ada",
    )(c_all, w_ada, b_ada.reshape(depth, 1, n3))


def _mod_spec(d, row_of_tile, kind):
    return pl.BlockSpec((None, 1, d), lambda i, j: (row_of_tile(i) * 3 + kind, 0, 0))


def _modulated(x, nw_ref, scale_ref, shift_ref):
    y = _rms(x, nw_ref[...])
    return (y * (1.0 + scale_ref[...]) + shift_ref[...]).astype(BF16)


def _inproj_kernel(xn_ref, nw_ref, scale0_ref, shift0_ref, scalen_ref, shiftn_ref, w_ref,
                   o_ref, g_ref, hout_ref, ha_ref, hb_ref, *, gate_tile, gate_col, nchunk):
    i = pl.program_id(0)
    j = pl.program_id(1)

    @pl.when((i == 0) & (j == 0))
    def _():
        ha_ref[...] = _modulated(xn_ref[...], nw_ref, scale0_ref, shift0_ref)

    def step(h_cur, h_nxt):
        rows = h_nxt.shape[0] // nchunk
        rs = pl.ds(pl.multiple_of(jnp.clip(j - 1, 0, nchunk - 1) * rows, rows), rows)
        h_nxt[rs, :] = _modulated(xn_ref[rs, :], nw_ref, scalen_ref, shiftn_ref)
        acc = jnp.dot(h_cur[...], w_ref[...], preferred_element_type=F32)
        o_ref[...] = acc.astype(o_ref.dtype)

        @pl.when(j == 0)
        def _():
            hout_ref[...] = h_cur[...]

        @pl.when(j == gate_tile)
        def _():
            g_ref[...] = acc[:, gate_col:gate_col + V7X_LANES]

    @pl.when(i % 2 == 0)
    def _():
        step(ha_ref, hb_ref)

    @pl.when(i % 2 == 1)
    def _():
        step(hb_ref, ha_ref)


def _row_of_tile(tm, seq, first_row):
    assert seq % tm == 0
    return lambda i: first_row + (i * tm) // seq


def _inproj_call(x2d, nw, mod_rows, w_pack, seq, first_row):
    t = _tiles()
    m, d = x2d.shape
    n = w_pack.shape[1]
    tm, tn = min(t["in_tm"], seq), t["in_tn"]
    ni, nj = m // tm, n // tn
    nchunk = max(c for c in (1, 2, 4, 8) if c < nj)
    row_of_tile = _row_of_tile(tm, seq, first_row)
    nxt = lambda i: jnp.minimum(i + 1, ni - 1)
    return pl.pallas_call(
        functools.partial(_inproj_kernel, gate_tile=COL_AG // tn, gate_col=COL_AG % tn, nchunk=nchunk),
        grid=(ni, nj),
        in_specs=[
            pl.BlockSpec((tm, d), lambda i, j: (jnp.where((i == 0) & (j == 0), 0, nxt(i)), 0)),
            pl.BlockSpec((1, d), lambda i, j: (0, 0)),
            _mod_spec(d, lambda i: row_of_tile(0), 1),
            _mod_spec(d, lambda i: row_of_tile(0), 0),
            _mod_spec(d, lambda i: row_of_tile(nxt(i)), 1),
            _mod_spec(d, lambda i: row_of_tile(nxt(i)), 0),
            pl.BlockSpec((d, tn), lambda i, j: (0, j)),
        ],
        out_specs=[pl.BlockSpec((tm, tn), lambda i, j: (i, j)),
                   pl.BlockSpec((tm, V7X_LANES), lambda i, j: (i, 0)),
                   pl.BlockSpec((tm, d), lambda i, j: (i, 0))],
        out_shape=[jax.ShapeDtypeStruct((m, n), BF16), jax.ShapeDtypeStruct((m, V7X_LANES), F32),
                   jax.ShapeDtypeStruct((m, d), BF16)],
        scratch_shapes=[pltpu.VMEM((tm, d), BF16), pltpu.VMEM((tm, d), BF16)],
        compiler_params=pltpu.CompilerParams(
            dimension_semantics=("arbitrary", "arbitrary"), vmem_limit_bytes=t["vmem_big"]),
        name="inproj",
    )(x2d, nw, mod_rows, mod_rows, mod_rows, mod_rows, w_pack)


def _mlstm_kernel(q_ref, k_ref, v_ref, g_ref, o_ref, z_ref, bg_ref, nw_ref, y_ref,
                  hfw_ref, c_ref, n_ref, m_ref, *, nblk, cb):
    s = pl.program_id(1)
    is_fwd = s < nblk
    blk = jnp.where(is_fwd, s, 2 * nblk - 1 - s)
    sgn = jnp.where(is_fwd, 1, -1)

    @pl.when((s == 0) | (s == nblk))
    def _():
        c_ref[...] = jnp.zeros_like(c_ref)
        n_ref[...] = jnp.zeros_like(n_ref)
        m_ref[...] = jnp.full_like(m_ref, M_INIT)

    ri = lax.broadcasted_iota(jnp.int32, (CHUNK, CHUNK), 0)
    ci = lax.broadcasted_iota(jnp.int32, (CHUNK, CHUNK), 1)
    keep = ((ri - ci) * sgn) >= 0
    tri = keep.astype(F32)
    bg = bg_ref[...]
    hsl = [slice(h * A_DH, (h + 1) * A_DH) for h in range(A_HEADS)]
    c_st = [c_ref[:, hsl[h]] for h in range(A_HEADS)]
    n_st = [n_ref[0:1, hsl[h]] for h in range(A_HEADS)]
    m_st = [m_ref[h:h + 1, 0:1] for h in range(A_HEADS)]

    rows, houts = [], []
    for jj in range(cb):
        r0 = pl.multiple_of(jnp.where(is_fwd, jj, cb - 1 - jj) * CHUNK, CHUNK)
        rows.append(r0)
        tsl = pl.ds(r0, CHUNK)
        gt = g_ref[tsl, :] + bg
        gt = jnp.where(is_fwd, gt, pltpu.roll(gt, V7X_LANES - A_HEADS, axis=1))
        logf = _log_sigmoid(gt)
        bc = jnp.dot(tri, logf, preferred_element_type=F32, precision=lax.Precision.HIGHEST)
        bc_t = bc.T
        gt_t = gt.T
        g_row = jnp.sum(logf, axis=0, keepdims=True)
        hrow = []
        for h in range(A_HEADS):
            fcol = 2 * A_HEADS + h
            q = q_ref[tsl, hsl[h]].astype(F32)
            k = k_ref[tsl, hsl[h]].astype(F32) * (A_DH ** -0.5)
            qb = q.astype(BF16)
            vb = v_ref[tsl, hsl[h]].astype(BF16)
            b_col = bc[:, fcol:fcol + 1]
            b_row = bc_t[fcol:fcol + 1, :]
            i_row = gt_t[h:h + 1, :]
            i_col = gt[:, h:h + 1]
            g_tot = g_row[:, fcol:fcol + 1]

            rmat = jnp.where(keep, i_row - b_row, -jnp.inf)
            big_m = jnp.maximum(jnp.max(rmat, axis=-1, keepdims=True), m_st[h])
            m_t = b_col + big_m
            qk = lax.dot_general(qb, k.astype(BF16), (((1,), (1,)), ((), ())), preferred_element_type=F32)
            smat = qk * jnp.exp(rmat - big_m)
            a_in = jnp.exp(m_st[h] - big_m)
            num = jnp.dot(smat.astype(BF16), vb, preferred_element_type=F32) \
                + a_in * jnp.dot(qb, c_st[h].astype(BF16), preferred_element_type=F32)
            den = jnp.sum(smat, axis=-1, keepdims=True) + a_in * jnp.sum(q * n_st[h], axis=-1, keepdims=True)
            hrow.append(num / jnp.maximum(jnp.abs(den), jnp.exp(-m_t)))

            w_end = g_tot - b_col + i_col
            m_loc = jnp.max(w_end, axis=0, keepdims=True)
            ka = k * jnp.exp(w_end - m_loc)
            c_loc = lax.dot_general(ka.astype(BF16), vb, (((0,), (0,)), ((), ())), preferred_element_type=F32)
            n_loc = jnp.sum(ka, axis=0, keepdims=True)
            m_new = jnp.maximum(g_tot + m_st[h], m_loc)
            a_old = jnp.exp(g_tot + m_st[h] - m_new)
            a_new = jnp.exp(m_loc - m_new)
            c_st[h] = a_old * c_st[h] + a_new * c_loc
            n_st[h] = a_old * n_st[h] + a_new * n_loc
            m_st[h] = m_new
        houts.append(hrow)

    for h in range(A_HEADS):
        c_ref[:, hsl[h]] = c_st[h]
        n_ref[0:1, hsl[h]] = n_st[h]
        m_ref[h:h + 1, :] = jnp.broadcast_to(m_st[h], (1, V7X_LANES))

    base = blk * (cb * CHUNK)

    @pl.when(is_fwd)
    def _():
        for jj in range(cb):
            for h in range(A_HEADS):
                hfw_ref[pl.ds(pl.multiple_of(base + rows[jj], CHUNK), CHUNK), h * A_DH:(h + 1) * A_DH] = houts[jj][h]

    @pl.when(jnp.logical_not(is_fwd))
    def _():
        for jj in range(cb):
            tsl = pl.ds(rows[jj], CHUNK)
            for h in range(A_HEADS):
                hs = slice(h * A_DH, (h + 1) * A_DH)
                tot = hfw_ref[pl.ds(pl.multiple_of(base + rows[jj], CHUNK), CHUNK), hs] + houts[jj][h]
                yh = _rms(tot, nw_ref[:, hs]) * _sigmoid(o_ref[tsl, hs].astype(F32))
                y_ref[tsl, hs] = (yh * _silu(z_ref[tsl, hs].astype(F32))).astype(BF16)


def _mlstm_call(p3, g3, bg_pad, nw):
    t = _tiles()
    bsz, seq, _ = p3.shape
    nc = seq // CHUNK
    cb = min(MLSTM_BLOCK_CHUNKS, nc)
    assert nc % cb == 0
    nblk = nc // cb
    rows = cb * CHUNK

    def cur(s):
        return jnp.where(s < nblk, s, 2 * nblk - 1 - s)

    def late(s):
        return jnp.where(s < nblk, nblk - 1, 2 * nblk - 1 - s)

    def blk(col, which):
        return pl.BlockSpec((None, rows, BRANCH_W), lambda b, s: (b, which(s), col // BRANCH_W))

    return pl.pallas_call(
        functools.partial(_mlstm_kernel, nblk=nblk, cb=cb),
        grid=(bsz, 2 * nblk),
        in_specs=[
            blk(COL_AQ, cur), blk(COL_AK, cur), blk(COL_AV, cur),
            pl.BlockSpec((None, rows, V7X_LANES), lambda b, s: (b, cur(s), 0)),
            blk(COL_AO, late), blk(COL_AZ, late),
            pl.BlockSpec((1, V7X_LANES), lambda b, s: (0, 0)),
            pl.BlockSpec((1, BRANCH_W), lambda b, s: (0, 0)),
        ],
        out_specs=pl.BlockSpec((None, rows, BRANCH_W), lambda b, s: (b, late(s), 0)),
        out_shape=jax.ShapeDtypeStruct((bsz, seq, BRANCH_W), BF16),
        scratch_shapes=[
            pltpu.VMEM((seq, BRANCH_W), F32),
            pltpu.VMEM((A_DH, BRANCH_W), F32),
            pltpu.VMEM((V7X_SUBLANES, BRANCH_W), F32),
            pltpu.VMEM((V7X_SUBLANES, V7X_LANES), F32),
        ],
        compiler_params=pltpu.CompilerParams(
            dimension_semantics=("arbitrary", "arbitrary"), vmem_limit_bytes=t["vmem_big"]),
        name="mlstm",
    )(p3, p3, p3, g3, p3, p3, bg_pad, nw)


def _nattn_kernel(q_ref, kc_ref, kx_ref, vc_ref, vx_ref, z_ref, qw_ref, kw_ref, bias_ref, y_ref,
                  kn_ref, vn_ref, *, nblk):
    i = pl.program_id(1)

    def put(slot, k_src, v_src):
        dst = pl.ds(pl.multiple_of(slot * NA_BLOCK_TOK, NA_BLOCK_TOK), NA_BLOCK_TOK)
        for h in range(B_HEADS):
            hs = slice(h * B_DH, (h + 1) * B_DH)
            kn_ref[dst, hs] = _rms(k_src[:, hs].astype(F32), kw_ref[...]).astype(BF16)
        vn_ref[dst, :] = v_src[...].astype(BF16)

    @pl.when(i == 0)
    def _():
        put(0, kc_ref, vc_ref)

    nslot = (i + 1) % 3
    put(nslot, kx_ref, vx_ref)

    @pl.when((i == 0) | (nslot == 0))
    def _():
        mirror = slice(3 * NA_BLOCK_TOK, 4 * NA_BLOCK_TOK)
        kn_ref[mirror, :] = kn_ref[0:NA_BLOCK_TOK, :]
        vn_ref[mirror, :] = vn_ref[0:NA_BLOCK_TOK, :]

    qn = []
    for h in range(B_HEADS):
        hs = slice(h * B_DH, (h + 1) * B_DH)
        qn.append((_rms(q_ref[:, hs].astype(F32), qw_ref[...]) * (B_DH ** -0.5)).astype(BF16))

    half = NA_BLOCK_TOK // 2
    starts = (jnp.where(i == 0, 0, ((i + 2) % 3) * NA_BLOCK_TOK + half), (i % 3) * NA_BLOCK_TOK)
    variants = (jnp.where(i == 0, 1, 0), jnp.where(i == nblk - 1, 2, 0))
    for quad in range(NA_BLOCK_ROWS // NA_QUAD_ROWS):
        qrows = slice(quad * NA_QUAD_TOK, (quad + 1) * NA_QUAD_TOK)
        win = pl.ds(pl.multiple_of(starts[quad], half), NA_WIN_TOK)
        for h in range(B_HEADS):
            hs = slice(h * B_DH, (h + 1) * B_DH)
            sc = lax.dot_general(qn[h][qrows, :], kn_ref[win, hs], (((1,), (1,)), ((), ())),
                                 preferred_element_type=F32)
            sc = sc + bias_ref[variants[quad], h]
            m = jnp.max(sc, axis=-1, keepdims=True)
            p = jnp.exp(sc - m)
            l = jnp.sum(p, axis=-1, keepdims=True)
            o = jnp.dot(p.astype(BF16), vn_ref[win, hs], preferred_element_type=F32) / l
            y_ref[qrows, hs] = (o * _silu(z_ref[qrows, hs].astype(F32))).astype(BF16)


def _nattn_call(p3, qw, kw, bias_quads):
    t = _tiles()
    bsz, seq, _ = p3.shape
    rows = seq // GRID_W
    assert rows >= WIN_ROWS and rows % NA_BLOCK_ROWS == 0
    nblk = rows // NA_BLOCK_ROWS

    def blk(col, shift):
        cb = col // BRANCH_W
        return pl.BlockSpec((None, NA_BLOCK_TOK, BRANCH_W),
                            lambda b, i: (b, jnp.clip(i + shift, 0, nblk - 1), cb))

    return pl.pallas_call(
        functools.partial(_nattn_kernel, nblk=nblk),
        grid=(bsz, nblk),
        in_specs=[
            blk(COL_BQ, 0),
            blk(COL_BK, 0), blk(COL_BK, 1),
            blk(COL_BV, 0), blk(COL_BV, 1),
            blk(COL_BZ, 0),
            pl.BlockSpec((1, B_DH), lambda b, i: (0, 0)),
            pl.BlockSpec((1, B_DH), lambda b, i: (0, 0)),
            pl.BlockSpec(bias_quads.shape, lambda b, i: (0, 0, 0, 0)),
        ],
        out_specs=pl.BlockSpec((None, NA_BLOCK_TOK, BRANCH_W), lambda b, i: (b, i, 0)),
        out_shape=jax.ShapeDtypeStruct((bsz, seq, BRANCH_W), BF16),
        scratch_shapes=[
            pltpu.VMEM((NA_RING_SLOTS * NA_BLOCK_TOK, BRANCH_W), BF16),
            pltpu.VMEM((NA_RING_SLOTS * NA_BLOCK_TOK, BRANCH_W), BF16),
        ],
        compiler_params=pltpu.CompilerParams(
            dimension_semantics=("arbitrary", "arbitrary"), vmem_limit_bytes=t["vmem_big"]),
        name="nattn",
    )(p3, p3, p3, p3, p3, p3, qw, kw, bias_quads)


def _skew(w, nq, first, ncol):
    width = w.shape[-1]
    assert first - (nq - 1) >= 0 and first + ncol <= width - 1
    lead = w.shape[:-1]
    flat = jnp.tile(w, (1,) * len(lead) + (nq,))[..., :nq * (width - 1)]
    return flat.reshape(lead + (nq, width - 1))[..., first:first + ncol]


def _nattn_bias(rpb):
    heads = rpb.shape[0]
    reach = GRID_W - WIN_COLS
    w = jnp.pad(rpb.astype(F32), ((0, 0), (0, 0), (reach, reach + 1)))
    tbl = _skew(w, GRID_W, GRID_W - 1, GRID_W)
    qc = np.arange(GRID_W)[:, None]
    kc = np.arange(GRID_W)[None, :]
    qs = np.clip(qc - WIN_COLS // 2, 0, GRID_W - WIN_COLS)
    ok = (kc >= qs) & (kc < qs + WIN_COLS)
    tbl = jnp.where(ok[None, None], tbl, -jnp.inf)
    per_d = [jnp.concatenate([tbl[:, d + j] for j in range(WIN_ROWS)], axis=-1) for d in range(WIN_ROWS)]
    fill = lambda n: jnp.full((heads, GRID_W, n), -jnp.inf, F32)

    def quad(ds, shifts):
        rows = [jnp.concatenate([fill(GRID_W * sh), per_d[d], fill(NA_WIN_TOK - NA_BLOCK_TOK - GRID_W * sh)],
                                axis=-1) for d, sh in zip(ds, shifts)]
        return jnp.concatenate(rows, axis=1)

    mid = WIN_ROWS - 1 - WIN_ROWS // 2
    interior = quad((mid,) * NA_QUAD_ROWS, range(NA_QUAD_ROWS))
    top = quad(range(WIN_ROWS - 1, WIN_ROWS - 1 - NA_QUAD_ROWS, -1), (0,) * NA_QUAD_ROWS)
    bottom = quad(range(NA_QUAD_ROWS - 1, -1, -1), (0,) * NA_QUAD_ROWS)
    return jnp.stack([interior, top, bottom])


def _swa_kernel(q_ref, kp_ref, kc_ref, kx_ref, vp_ref, vc_ref, vx_ref, z_ref, qw_ref, kw_ref,
                sink_ref, bias_ref, y_ref, *, nsteps, nq):
    n = pl.program_id(1)
    band = 3 * C_BLOCK
    lane = lax.broadcasted_iota(jnp.int32, (1, V7X_LANES), 1)
    lo = lane < C_DH

    def pair_rms(x, w):
        x2 = x * x
        s_lo = jnp.sum(jnp.where(lo, x2, 0.0), axis=-1, keepdims=True)
        s_hi = jnp.sum(jnp.where(lo, 0.0, x2), axis=-1, keepdims=True)
        r = lax.rsqrt(jnp.where(lo, s_lo, s_hi) * (1.0 / C_DH) + EPS)
        return x * r * w

    kn = pair_rms(jnp.concatenate([kp_ref[...], kc_ref[...], kx_ref[...]], axis=0).astype(F32), kw_ref[...])
    vc = jnp.concatenate([vp_ref[...], vc_ref[...], vx_ref[...]], axis=0).astype(F32)
    kn_sw = pltpu.roll(kn, C_DH, axis=1)
    vc_sw = pltpu.roll(vc, C_DH, axis=1)
    stacks = []
    for g in range(C_KV_HEADS):
        k_lo, k_hi = (kn, kn_sw) if g == 0 else (kn_sw, kn)
        v_lo, v_hi = (vc, vc_sw) if g == 0 else (vc_sw, vc)
        stacks.append((jnp.where(lo, k_lo, 0.0).astype(BF16), jnp.where(lo, 0.0, k_hi).astype(BF16),
                       jnp.where(lo, v_lo, 0.0).astype(BF16), jnp.where(lo, 0.0, v_hi).astype(BF16)))

    ki = lax.broadcasted_iota(jnp.int32, (1, 2 * band), 1)
    ki = jnp.where(ki >= band, ki - band, ki)
    upper = lax.broadcasted_iota(jnp.int32, (2 * C_BLOCK, 1), 0) < C_BLOCK
    for qb in range(nq):
        qrows = slice(qb * C_BLOCK, (qb + 1) * C_BLOCK)
        krows = slice((nq + qb - 1) * C_BLOCK, (nq + qb + 2) * C_BLOCK)
        has_prev = (n > 0) if qb == 0 else True
        has_next = (n < nsteps - 1) if qb == nq - 1 else True
        valid = ((ki >= C_BLOCK) | has_prev) & ((ki < 2 * C_BLOCK) | has_next)
        for g in range(C_KV_HEADS):
            k_even, k_odd, v_even, v_odd = stacks[g]
            kk = jnp.concatenate([k_even[krows], k_odd[krows]], axis=0)
            vv = jnp.concatenate([v_even[krows], v_odd[krows]], axis=0)
            slabs = [slice(p * V7X_LANES, (p + 1) * V7X_LANES) for p in (2 * g, 2 * g + 1)]
            qn = jnp.concatenate(
                [(pair_rms(q_ref[qrows, sl].astype(F32), qw_ref[...]) * (C_DH ** -0.5)).astype(BF16)
                 for sl in slabs], axis=0)
            sc = lax.dot_general(qn, kk, (((1,), (1,)), ((), ())), preferred_element_type=F32)
            sc = jnp.where(valid, sc + bias_ref[g], -jnp.inf)
            probs, denoms = [], []
            for e in range(2):
                se = sc[:, e * band:(e + 1) * band]
                h0 = C_GROUP * g + e
                sink = jnp.where(upper, sink_ref[h0:h0 + 1, 0:1], sink_ref[h0 + 2:h0 + 3, 0:1])
                m = jnp.maximum(jnp.max(se, axis=-1, keepdims=True), sink)
                pe = jnp.exp(se - m)
                probs.append(pe.astype(BF16))
                denoms.append(jnp.sum(pe, axis=-1, keepdims=True) + jnp.exp(sink - m))
            o = jnp.dot(jnp.concatenate(probs, axis=1), vv, preferred_element_type=F32)
            o = o / jnp.where(lo, denoms[0], denoms[1])
            for j, sl in enumerate(slabs):
                rows = slice(j * C_BLOCK, (j + 1) * C_BLOCK)
                y_ref[qrows, sl] = (o[rows] * _silu(z_ref[qrows, sl].astype(F32))).astype(BF16)


def _swa_call(p3, qw, kw, sink_rows, bias):
    t = _tiles()
    bsz, seq, _ = p3.shape
    nq = SWA_STEP_BLOCKS
    rows = nq * C_BLOCK
    assert seq % rows == 0
    nsteps = seq // rows

    def wide(col):
        return pl.BlockSpec((None, rows, BRANCH_W), lambda b, n: (b, n, col // BRANCH_W))

    def kv(col, shift):
        return pl.BlockSpec((None, rows, C_KV_W),
                            lambda b, n: (b, jnp.clip(n + shift, 0, nsteps - 1), col // C_KV_W))

    return pl.pallas_call(
        functools.partial(_swa_kernel, nsteps=nsteps, nq=nq),
        grid=(bsz, nsteps),
        in_specs=[
            wide(COL_CQ),
            kv(COL_CK, -1), kv(COL_CK, 0), kv(COL_CK, 1),
            kv(COL_CV, -1), kv(COL_CV, 0), kv(COL_CV, 1),
            wide(COL_CZ),
            pl.BlockSpec((1, V7X_LANES), lambda b, n: (0, 0)),
            pl.BlockSpec((1, V7X_LANES), lambda b, n: (0, 0)),
            pl.BlockSpec(sink_rows.shape, lambda b, n: (0, 0)),
            pl.BlockSpec(bias.shape, lambda b, n: (0, 0, 0)),
        ],
        out_specs=pl.BlockSpec((None, rows, BRANCH_W), lambda b, n: (b, n, 0)),
        out_shape=jax.ShapeDtypeStruct((bsz, seq, BRANCH_W), BF16),
        compiler_params=pltpu.CompilerParams(
            dimension_semantics=("arbitrary", "arbitrary"), vmem_limit_bytes=t["vmem_small"]),
        name="swa",
    )(p3, p3, p3, p3, p3, p3, p3, p3, qw, kw, sink_rows, bias)


def _t5_bucket_rel():
    rel = np.arange(-(2 * C_BLOCK - 1), 2 * C_BLOCK)
    half = N_BUCKETS // 2
    max_exact = half // 2
    n = np.abs(rel)
    nf = np.maximum(n, 1).astype(np.float32)
    scale = np.float32(math.log(MAX_DIST / max_exact))
    large = max_exact + (np.log(nf / np.float32(max_exact)) / scale
                         * np.float32(half - max_exact)).astype(np.int32)
    large = np.minimum(large, half - 1)
    bucket = np.where(rel > 0, half, 0) + np.where(n < max_exact, n, large)
    return bucket, n <= C_WINDOW


def _swa_bias(rel_bias):
    bucket, in_window = _t5_bucket_rel()
    per_rel = jnp.where(in_window[:, None], rel_bias.astype(F32)[bucket], -jnp.inf)
    w = jnp.pad(per_rel.T, ((0, 0), (0, 1)))
    tbl = _skew(w, C_BLOCK, C_BLOCK - 1, 3 * C_BLOCK)
    slabs = jnp.concatenate([tbl[0::2], tbl[1::2]], axis=-1)
    return slabs.reshape(C_KV_HEADS, 2 * C_BLOCK, 2 * 3 * C_BLOCK)


def _conv_kernel(a_ref, g_ref, ap_ref, gp_ref, ax_ref, gx_ref, z_ref, cw_ref, cb_ref, lw_ref, lb_ref,
                 y_ref, ext_ref, sh_ref, *, ntile):
    i = pl.program_id(1)
    tt = a_ref.shape[0]
    glu = lambda a, g: a[...].astype(F32) * _sigmoid(g[...].astype(F32))
    prev = jnp.where(i > 0, glu(ap_ref, gp_ref), 0.0)
    nxt = jnp.where(i < ntile - 1, glu(ax_ref, gx_ref), 0.0)
    ext_ref[0:CONV_HALO, :] = prev
    ext_ref[CONV_HALO:CONV_HALO + tt, :] = glu(a_ref, g_ref)
    ext_ref[CONV_HALO + tt:2 * CONV_HALO + tt, :] = nxt
    span = sh_ref.shape[1]
    for r in range(1, V7X_SUBLANES):
        sh_ref[r - 1] = ext_ref[pl.ds(r, span), :]
    acc = None
    for w in range(CONV_W):
        a, r = divmod(CONV_HALO - CONV_W // 2 + w, V7X_SUBLANES)
        lo = a * V7X_SUBLANES
        src = ext_ref[lo:lo + tt, :] if r == 0 else sh_ref[r - 1, lo:lo + tt, :]
        term = src * cw_ref[w:w + 1, :]
        acc = term if acc is None else acc + term
    u = acc + cb_ref[...]
    mu = jnp.mean(u, axis=-1, keepdims=True)
    var = jnp.mean(jnp.square(u - mu), axis=-1, keepdims=True)
    u = (u - mu) * lax.rsqrt(var + EPS) * lw_ref[...] + lb_ref[...]
    y_ref[...] = (_silu(u) * _silu(z_ref[...].astype(F32))).astype(BF16)


def _conv_call(p3, cw_pad, cb, lw, lb):
    t = _tiles()
    bsz, seq, _ = p3.shape
    tt = CONV_TILE
    ntile = seq // tt
    per = tt // CONV_HALO
    nhalo = seq // CONV_HALO
    span = tt + (2 * CONV_HALO - 1) // V7X_SUBLANES * V7X_SUBLANES

    def cur(col):
        return pl.BlockSpec((None, tt, BRANCH_W), lambda b, i: (b, i, col // BRANCH_W))

    def halo(col, after):
        def idx(b, i):
            r = (i + 1) * per if after else i * per - 1
            return (b, jnp.clip(r, 0, nhalo - 1), col // BRANCH_W)
        return pl.BlockSpec((None, CONV_HALO, BRANCH_W), idx)

    vec = pl.BlockSpec((1, BRANCH_W), lambda b, i: (0, 0))
    return pl.pallas_call(
        functools.partial(_conv_kernel, ntile=ntile),
        grid=(bsz, ntile),
        in_specs=[
            cur(COL_DA), cur(COL_DG),
            halo(COL_DA, False), halo(COL_DG, False), halo(COL_DA, True), halo(COL_DG, True),
            cur(COL_DZ),
            pl.BlockSpec(cw_pad.shape, lambda b, i: (0, 0)),
            vec, vec, vec,
        ],
        out_specs=pl.BlockSpec((None, tt, BRANCH_W), lambda b, i: (b, i, 0)),
        out_shape=jax.ShapeDtypeStruct((bsz, seq, BRANCH_W), BF16),
        scratch_shapes=[
            pltpu.VMEM((tt + 2 * CONV_HALO, BRANCH_W), F32),
            pltpu.VMEM((V7X_SUBLANES - 1, span, BRANCH_W), F32),
        ],
        compiler_params=pltpu.CompilerParams(
            dimension_semantics=("arbitrary", "arbitrary"), vmem_limit_bytes=t["vmem_small"]),
        name="conv",
    )(p3, p3, p3, p3, p3, p3, p3, cw_pad, cb, lw, lb)


def _merge_kernel(h_ref, ya_ref, yb_ref, yc_ref, yd_ref, wma_ref, wmb_ref, wmc_ref, wmd_ref, wb_ref, o_ref):
    h = h_ref[...]
    merged = None
    branches = ((ya_ref, wma_ref), (yb_ref, wmb_ref), (yc_ref, wmc_ref), (yd_ref, wmd_ref))
    for b, (y_ref, wm_ref) in enumerate(branches):
        gate = _sigmoid(jnp.dot(h, wm_ref[...], preferred_element_type=F32))
        term = gate * jnp.dot(y_ref[...], wb_ref[b], preferred_element_type=F32)
        merged = term if merged is None else merged + term
    o_ref[...] = merged.astype(BF16)


def _merge_call(h2d, ys, w_merge, w_branch, seq):
    t = _tiles()
    m, d = h2d.shape
    tm, tn = min(t["mg_tm"], seq), t["mg_tn"]
    yspec = pl.BlockSpec((tm, BRANCH_W), lambda i, j: (i, 0))
    return pl.pallas_call(
        _merge_kernel,
        grid=(m // tm, d // tn),
        in_specs=[
            pl.BlockSpec((tm, d), lambda i, j: (i, 0)),
            yspec, yspec, yspec, yspec,
            *[pl.BlockSpec((d, tn), functools.partial(lambda i, j, b: (0, b * (d // tn) + j), b=b))
              for b in range(N_BRANCH)],
            pl.BlockSpec((N_BRANCH, BRANCH_W, tn), lambda i, j: (0, 0, j)),
        ],
        out_specs=pl.BlockSpec((tm, tn), lambda i, j: (i, j)),
        out_shape=jax.ShapeDtypeStruct((m, d), BF16),
        compiler_params=pltpu.CompilerParams(
            dimension_semantics=("arbitrary", "arbitrary"), vmem_limit_bytes=t["vmem_big"]),
        name="merge",
    )(h2d, *ys, *([w_merge] * N_BRANCH), w_branch)


def _outproj_kernel(x_ref, mg_ref, gate_ref, wo_ref, o_ref):
    o_ref[...] = x_ref[...] + gate_ref[...] * jnp.dot(mg_ref[...], wo_ref[...], preferred_element_type=F32)


def _outproj_call(x2d, merged, mod_rows, w_out, seq, first_row):
    t = _tiles()
    m, d = x2d.shape
    tm, tn = min(t["op_tm"], seq), t["op_tn"]
    row_of_tile = _row_of_tile(tm, seq, first_row)
    return pl.pallas_call(
        _outproj_kernel,
        grid=(m // tm, d // tn),
        in_specs=[
            pl.BlockSpec((tm, tn), lambda i, j: (i, j)),
            pl.BlockSpec((tm, d), lambda i, j: (i, 0)),
            pl.BlockSpec((None, 1, tn), lambda i, j: (row_of_tile(i) * 3 + 2, 0, j)),
            pl.BlockSpec((d, tn), lambda i, j: (0, j)),
        ],
        out_specs=pl.BlockSpec((tm, tn), lambda i, j: (i, j)),
        out_shape=jax.ShapeDtypeStruct((m, d), F32),
        compiler_params=pltpu.CompilerParams(
            dimension_semantics=("arbitrary", "arbitrary"), vmem_limit_bytes=t["vmem_big"]),
        name="outproj",
    )(x2d, merged, mod_rows, w_out)


def _pack_w_in(w):
    d = w.shape[0]
    sizes = (512, 512, 512, 512, 512, A_GATES, 512, 512, 512, 512, 512, C_KV_W, C_KV_W, 512, 1024, 512)
    offs = np.concatenate([[0], np.cumsum(sizes)])
    (aq, ak, av, ao, az, ag, bq, bk, bv, bz, cq, ck, cv, cz, dglu, dz) = [
        w[:, int(offs[i]):int(offs[i + 1])] for i in range(len(sizes))]
    pad = jnp.zeros((d, PACK_COLS - COL_AG - A_GATES), w.dtype)
    packed = jnp.concatenate([aq, ak, av, ao, az, bq, bk, bv, bz, dglu, dz, cq, cz, ck, cv, ag, pad], axis=1)
    return packed.astype(BF16), w[:, int(offs[-1]):].astype(BF16)


def kernel(x_prompt, x_sample, c_prompt, c_sample, rel_bias, norm_w, w_ada, b_ada, w_in, b_gate, mlstm_norm_w, na_q_norm, na_k_norm, na_rpb, swa_q_norm, swa_k_norm, swa_sink, conv_w, conv_b, conv_ln_w, conv_ln_b, w_branch, w_out):
    depth, d = norm_w.shape
    groups = ((x_prompt, 0), (x_sample, c_prompt.shape[0]))
    n_cond = c_prompt.shape[0] + c_sample.shape[0]
    cond_rows = -(-n_cond // V7X_SUBLANES) * V7X_SUBLANES
    c_all = jnp.concatenate([c_prompt, c_sample, jnp.zeros((cond_rows - n_cond, d), F32)], axis=0)
    mod = _ada_call(c_all, w_ada, b_ada)
    mod_rows = mod.reshape(depth * cond_rows * 3, 1, d)

    swa_bias = _swa_bias(rel_bias)
    layers = []
    for l in range(depth):
        w_pack, w_merge = _pack_w_in(w_in[l])
        layers.append(dict(
            w_pack=w_pack, w_merge=w_merge,
            w_branch=w_branch[l].astype(BF16), w_out=w_out[l].astype(BF16),
            nw=norm_w[l].reshape(1, d),
            bg=jnp.pad(b_gate[l], (0, V7X_LANES - A_GATES)).reshape(1, V7X_LANES),
            mnw=mlstm_norm_w[l].reshape(1, BRANCH_W),
            na_qw=na_q_norm[l].reshape(1, B_DH), na_kw=na_k_norm[l].reshape(1, B_DH),
            na_bias=_nattn_bias(na_rpb[l]),
            swa_qw=jnp.tile(swa_q_norm[l], 2).reshape(1, V7X_LANES),
            swa_kw=jnp.tile(swa_k_norm[l], 2).reshape(1, V7X_LANES),
            sink=jnp.broadcast_to(swa_sink[l].reshape(C_HEADS, 1), (C_HEADS, V7X_LANES)),
            cw=jnp.pad(conv_w[l], ((0, 1), (0, 0))),
            cb=conv_b[l].reshape(1, BRANCH_W), lw=conv_ln_w[l].reshape(1, BRANCH_W),
            lb=conv_ln_b[l].reshape(1, BRANCH_W),
        ))

    outs = []
    for x, cond_off in groups:
        bsz, seq, _ = x.shape
        x2d = x.reshape(bsz * seq, d)
        for l, lw in enumerate(layers):
            first_row = l * cond_rows + cond_off
            p, gates, h2d = _inproj_call(x2d, lw["nw"], mod_rows, lw["w_pack"], seq, first_row)
            p3 = p.reshape(bsz, seq, PACK_COLS)
            ya = _mlstm_call(p3, gates.reshape(bsz, seq, V7X_LANES), lw["bg"], lw["mnw"])
            yb = _nattn_call(p3, lw["na_qw"], lw["na_kw"], lw["na_bias"])
            yc = _swa_call(p3, lw["swa_qw"], lw["swa_kw"], lw["sink"], swa_bias)
            yd = _conv_call(p3, lw["cw"], lw["cb"], lw["lw"], lw["lb"])
            ys = [y.reshape(bsz * seq, BRANCH_W) for y in (ya, yb, yc, yd)]
            merged = _merge_call(h2d, ys, lw["w_merge"], lw["w_branch"], seq)
            x2d = _outproj_call(x2d, merged, mod_rows, lw["w_out"], seq, first_row)
        outs.append(x2d.reshape(bsz, seq, d))
    return tuple(outs)
```

```python
import functools
import math

import numpy as np
import jax
import jax.numpy as jnp
from jax import lax
from jax.experimental import pallas as pl
from jax.experimental.pallas import tpu as pltpu

F32 = jnp.float32
BF16 = jnp.bfloat16

V7X_VMEM_BYTES = 64 * 1024 * 1024
V7X_LANES = 128
V7X_SUBLANES = 8
MIB = 1024 * 1024

EPS = 1e-6
N_BRANCH = 4
BRANCH_W = 512
GRID_W = 64
A_HEADS = 4
A_DH = BRANCH_W // A_HEADS
A_GATES = 4 * A_HEADS
CHUNK = 128
MLSTM_BLOCK_CHUNKS = 8
M_INIT = -1e30
LOG2E = math.log2(math.e)
B_HEADS = 4
B_DH = BRANCH_W // B_HEADS
WIN_ROWS = 8
WIN_COLS = 16
NA_BLOCK_ROWS = 8
NA_BLOCK_TOK = NA_BLOCK_ROWS * GRID_W
NA_QUAD_ROWS = 4
NA_QUAD_TOK = NA_QUAD_ROWS * GRID_W
NA_WIN_TOK = (WIN_ROWS + NA_QUAD_ROWS) * GRID_W
NA_RING_SLOTS = 4
C_HEADS = 8
C_KV_HEADS = 2
C_GROUP = C_HEADS // C_KV_HEADS
C_DH = BRANCH_W // C_HEADS
C_KV_W = C_KV_HEADS * C_DH
C_WINDOW = 128
C_BLOCK = 128
SWA_STEP_BLOCKS = 1
N_BUCKETS = 32
MAX_DIST = 128
CONV_W = 31
CONV_HALO = 16
CONV_TILE = 512

COL_AQ, COL_AK, COL_AV, COL_AO, COL_AZ = 0, 512, 1024, 1536, 2048
COL_BQ, COL_BK, COL_BV, COL_BZ = 2560, 3072, 3584, 4096
COL_DA, COL_DG, COL_DZ = 4608, 5120, 5632
COL_CQ, COL_CZ = 6144, 6656
COL_CK, COL_CV, COL_AG = 7168, 7296, 7424
PACK_COLS = 7680


def _tiles():
    return dict(
        ada_tn=1536,
        in_tm=1024, in_tn=1280,
        mg_tm=1024, mg_tn=512,
        op_tm=1024, op_tn=1024,
        vmem_small=32 * MIB,
        vmem_big=V7X_VMEM_BYTES - 8 * MIB,
    )


def _sigmoid(x):
    return jax.nn.sigmoid(x)


def _silu(x):
    return x * jax.nn.sigmoid(x)


def _log_sigmoid(x):
    return jnp.minimum(x, 0.0) - jnp.log1p(jnp.exp(-jnp.abs(x)))


def _rms(x, w):
    r = lax.rsqrt(jnp.mean(x * x, axis=-1, keepdims=True) + EPS)
    return x * r * w


def _ada_kernel(c_ref, w_ref, b_ref, o_ref):
    c = c_ref[...]
    sc = _silu(c).astype(BF16)
    o_ref[...] = jnp.dot(sc, w_ref[...].astype(BF16), preferred_element_type=F32) + b_ref[...]


def _ada_call(c_all, w_ada, b_ada):
    t = _tiles()
    depth, d, n3 = w_ada.shape
    rows = c_all.shape[0]
    tn = t["ada_tn"]
    return pl.pallas_call(
        _ada_kernel,
        grid=(depth, n3 // tn),
        in_specs=[
            pl.BlockSpec((rows, d), lambda l, j: (0, 0)),
            pl.BlockSpec((None, d, tn), lambda l, j: (l, 0, j)),
            pl.BlockSpec((None, 1, tn), lambda l, j: (l, 0, j)),
        ],
        out_specs=pl.BlockSpec((None, rows, tn), lambda l, j: (l, 0, j)),
        out_shape=jax.ShapeDtypeStruct((depth, rows, n3), F32),
        compiler_params=pltpu.CompilerParams(
            dimension_semantics=("arbitrary", "arbitrary"), vmem_limit_bytes=t["vmem_small"]),
        name="ada",
    )(c_all, w_ada, b_ada.reshape(depth, 1, n3))


def _mod_spec(d, row_of_tile, kind):
    return pl.BlockSpec((None, 1, d), lambda i, j: (row_of_tile(i) * 3 + kind, 0, 0))


def _modulated(x, nw_ref, scale_ref, shift_ref):
    y = _rms(x, nw_ref[...])
    return (y * (1.0 + scale_ref[...]) + shift_ref[...]).astype(BF16)


def _inproj_kernel(xn_ref, nw_ref, scale0_ref, shift0_ref, scalen_ref, shiftn_ref, w_ref,
                   o_ref, g_ref, hout_ref, ha_ref, hb_ref, *, gate_tile, gate_col, nchunk):
    i = pl.program_id(0)
    j = pl.program_id(1)

    @pl.when((i == 0) & (j == 0))
    def _():
        ha_ref[...] = _modulated(xn_ref[...], nw_ref, scale0_ref, shift0_ref)

    def step(h_cur, h_nxt):
        rows = h_nxt.shape[0] // nchunk
        rs = pl.ds(pl.multiple_of(jnp.clip(j - 1, 0, nchunk - 1) * rows, rows), rows)
        h_nxt[rs, :] = _modulated(xn_ref[rs, :], nw_ref, scalen_ref, shiftn_ref)
        acc = jnp.dot(h_cur[...], w_ref[...], preferred_element_type=F32)
        o_ref[...] = acc.astype(o_ref.dtype)

        @pl.when(j == 0)
        def _():
            hout_ref[...] = h_cur[...]

        @pl.when(j == gate_tile)
        def _():
            g_ref[...] = acc[:, gate_col:gate_col + V7X_LANES]

    @pl.when(i % 2 == 0)
    def _():
        step(ha_ref, hb_ref)

    @pl.when(i % 2 == 1)
    def _():
        step(hb_ref, ha_ref)


def _row_of_tile(tm, seq, first_row):
    assert seq % tm == 0
    return lambda i: first_row + (i * tm) // seq


def _inproj_call(x2d, nw, mod_rows, w_pack, seq, first_row):
    t = _tiles()
    m, d = x2d.shape
    n = w_pack.shape[1]
    tm, tn = min(t["in_tm"], seq), t["in_tn"]
    ni, nj = m // tm, n // tn
    nchunk = max(c for c in (1, 2, 4, 8) if c < nj)
    row_of_tile = _row_of_tile(tm, seq, first_row)
    nxt = lambda i: jnp.minimum(i + 1, ni - 1)
    return pl.pallas_call(
        functools.partial(_inproj_kernel, gate_tile=COL_AG // tn, gate_col=COL_AG % tn, nchunk=nchunk),
        grid=(ni, nj),
        in_specs=[
            pl.BlockSpec((tm, d), lambda i, j: (jnp.where((i == 0) & (j == 0), 0, nxt(i)), 0)),
            pl.BlockSpec((1, d), lambda i, j: (0, 0)),
            _mod_spec(d, lambda i: row_of_tile(0), 1),
            _mod_spec(d, lambda i: row_of_tile(0), 0),
            _mod_spec(d, lambda i: row_of_tile(nxt(i)), 1),
            _mod_spec(d, lambda i: row_of_tile(nxt(i)), 0),
            pl.BlockSpec((d, tn), lambda i, j: (0, j)),
        ],
        out_specs=[pl.BlockSpec((tm, tn), lambda i, j: (i, j)),
                   pl.BlockSpec((tm, V7X_LANES), lambda i, j: (i, 0)),
                   pl.BlockSpec((tm, d), lambda i, j: (i, 0))],
        out_shape=[jax.ShapeDtypeStruct((m, n), BF16), jax.ShapeDtypeStruct((m, V7X_LANES), F32),
                   jax.ShapeDtypeStruct((m, d), BF16)],
        scratch_shapes=[pltpu.VMEM((tm, d), BF16), pltpu.VMEM((tm, d), BF16)],
        compiler_params=pltpu.CompilerParams(
            dimension_semantics=("arbitrary", "arbitrary"), vmem_limit_bytes=t["vmem_big"]),
        name="inproj",
    )(x2d, nw, mod_rows, mod_rows, mod_rows, mod_rows, w_pack)


def _mlstm_kernel(q_ref, k_ref, v_ref, g_ref, o_ref, z_ref, bg_ref, nw_ref, y_ref,
                  hfw_ref, c_ref, n_ref, m_ref, *, nblk, cb):
    s = pl.program_id(1)
    is_fwd = s < nblk
    blk = jnp.where(is_fwd, s, 2 * nblk - 1 - s)
    sgn = jnp.where(is_fwd, 1, -1)

    @pl.when((s == 0) | (s == nblk))
    def _():
        c_ref[...] = jnp.zeros_like(c_ref)
        n_ref[...] = jnp.zeros_like(n_ref)
        m_ref[...] = jnp.full_like(m_ref, M_INIT)

    ri = lax.broadcasted_iota(jnp.int32, (CHUNK, CHUNK), 0)
    ci = lax.broadcasted_iota(jnp.int32, (CHUNK, CHUNK), 1)
    tri = (((ri - ci) * sgn) >= 0).astype(F32)
    keep_t = ((ci - ri) * sgn) >= 0
    eye = (ri == ci).astype(BF16)
    bg = bg_ref[...]
    hsl = [slice(h * A_DH, (h + 1) * A_DH) for h in range(A_HEADS)]
    ct_st = [c_ref[:, hsl[h]] for h in range(A_HEADS)]
    n_st = [n_ref[0:1, hsl[h]] for h in range(A_HEADS)]
    m_st = [m_ref[h:h + 1, 0:1] for h in range(A_HEADS)]
    n_pad = jnp.zeros((2 * V7X_SUBLANES - 1, A_DH), BF16)

    chunks, houts = [], []
    for jj in range(cb):
        lc = jnp.where(is_fwd, jj, cb - 1 - jj)
        chunks.append(lc)
        tsl = pl.ds(pl.multiple_of(lc * CHUNK, CHUNK), CHUNK)
        gt = g_ref[tsl, :] + bg
        gt = jnp.where(is_fwd, gt, pltpu.roll(gt, V7X_LANES - A_HEADS, axis=1))
        logf = _log_sigmoid(gt)
        bc = jnp.dot(tri, logf, preferred_element_type=F32, precision=lax.Precision.HIGHEST)
        bc_t = bc.T
        g_row = jnp.sum(logf, axis=0, keepdims=True)
        hrow = []
        for h in range(A_HEADS):
            fcol = 2 * A_HEADS + h
            qb = q_ref[tsl, hsl[h]].astype(BF16)
            k = k_ref[tsl, hsl[h]].astype(F32) * (A_DH ** -0.5)
            vb = v_ref[tsl, hsl[h]].astype(BF16)
            b_col = bc[:, fcol:fcol + 1]
            i_col = gt[:, h:h + 1]
            g_tot = g_row[:, fcol:fcol + 1]

            rmat = jnp.where(keep_t, i_col - b_col, -jnp.inf)
            big_m = jnp.maximum(jnp.max(rmat, axis=0, keepdims=True), m_st[h])
            kq = lax.dot_general(k.astype(BF16), qb, (((1,), (1,)), ((), ())), preferred_element_type=F32)
            smat = kq * jnp.exp(rmat - big_m)
            a_in = jnp.exp(m_st[h] - big_m)
            vt = lax.dot_general(eye, vb, (((1,), (1,)), ((), ())), preferred_element_type=F32).astype(BF16)
            num = jnp.dot(vt, smat.astype(BF16), preferred_element_type=F32)
            ctn = jnp.concatenate([ct_st[h].astype(BF16), n_st[h].astype(BF16), n_pad], axis=0)
            qcn = lax.dot_general(ctn, qb, (((1,), (1,)), ((), ())), preferred_element_type=F32)
            den = jnp.sum(smat, axis=0, keepdims=True) + a_in * qcn[A_DH:A_DH + 1, :]
            m_t = bc_t[fcol:fcol + 1, :] + big_m
            hrow.append((num + a_in * qcn[:A_DH, :]) / jnp.maximum(jnp.abs(den), jnp.exp(-m_t)))

            w_end = g_tot - b_col + i_col
            m_loc = jnp.max(w_end, axis=0, keepdims=True)
            ka = k * jnp.exp(w_end - m_loc)
            ct_loc = jnp.dot(vt, ka.astype(BF16), preferred_element_type=F32)
            n_loc = jnp.sum(ka, axis=0, keepdims=True)
            m_new = jnp.maximum(g_tot + m_st[h], m_loc)
            a_old = jnp.exp(g_tot + m_st[h] - m_new)
            a_new = jnp.exp(m_loc - m_new)
            ct_st[h] = a_old * ct_st[h] + a_new * ct_loc
            n_st[h] = a_old * n_st[h] + a_new * n_loc
            m_st[h] = m_new
        houts.append(hrow)

    for h in range(A_HEADS):
        c_ref[:, hsl[h]] = ct_st[h]
        n_ref[0:1, hsl[h]] = n_st[h]
        m_ref[h:h + 1, :] = jnp.broadcast_to(m_st[h], (1, V7X_LANES))

    @pl.when(is_fwd)
    def _():
        for jj in range(cb):
            for h in range(A_HEADS):
                hfw_ref[blk * cb + chunks[jj], hsl[h], :] = houts[jj][h]

    @pl.when(jnp.logical_not(is_fwd))
    def _():
        for jj in range(cb):
            tsl = pl.ds(pl.multiple_of(chunks[jj] * CHUNK, CHUNK), CHUNK)
            for h in range(A_HEADS):
                tot = hfw_ref[blk * cb + chunks[jj], hsl[h], :] + houts[jj][h]
                tot = tot * lax.rsqrt(jnp.mean(tot * tot, axis=0, keepdims=True) + EPS)
                yh = tot.T * nw_ref[:, hsl[h]] * _sigmoid(o_ref[tsl, hsl[h]].astype(F32))
                y_ref[tsl, hsl[h]] = (yh * _silu(z_ref[tsl, hsl[h]].astype(F32))).astype(BF16)


def _mlstm_call(p3, g3, bg_pad, nw):
    t = _tiles()
    bsz, seq, _ = p3.shape
    nc = seq // CHUNK
    cb = min(MLSTM_BLOCK_CHUNKS, nc)
    assert nc % cb == 0
    nblk = nc // cb
    rows = cb * CHUNK

    def cur(s):
        return jnp.where(s < nblk, s, 2 * nblk - 1 - s)

    def late(s):
        return jnp.where(s < nblk, nblk - 1, 2 * nblk - 1 - s)

    def blk(col, which):
        return pl.BlockSpec((None, rows, BRANCH_W), lambda b, s: (b, which(s), col // BRANCH_W))

    return pl.pallas_call(
        functools.partial(_mlstm_kernel, nblk=nblk, cb=cb),
        grid=(bsz, 2 * nblk),
        in_specs=[
            blk(COL_AQ, cur), blk(COL_AK, cur), blk(COL_AV, cur),
            pl.BlockSpec((None, rows, V7X_LANES), lambda b, s: (b, cur(s), 0)),
            blk(COL_AO, late), blk(COL_AZ, late),
            pl.BlockSpec((1, V7X_LANES), lambda b, s: (0, 0)),
            pl.BlockSpec((1, BRANCH_W), lambda b, s: (0, 0)),
        ],
        out_specs=pl.BlockSpec((None, rows, BRANCH_W), lambda b, s: (b, late(s), 0)),
        out_shape=jax.ShapeDtypeStruct((bsz, seq, BRANCH_W), BF16),
        scratch_shapes=[
            pltpu.VMEM((nc, BRANCH_W, CHUNK), F32),
            pltpu.VMEM((A_DH, BRANCH_W), F32),
            pltpu.VMEM((V7X_SUBLANES, BRANCH_W), F32),
            pltpu.VMEM((V7X_SUBLANES, V7X_LANES), F32),
        ],
        compiler_params=pltpu.CompilerParams(
            dimension_semantics=("arbitrary", "arbitrary"), vmem_limit_bytes=t["vmem_big"]),
        name="mlstm",
    )(p3, p3, p3, g3, p3, p3, bg_pad, nw)


def _nattn_kernel(q_ref, kc_ref, kx_ref, vc_ref, vx_ref, z_ref, qw_ref, kw_ref, bias_ref, y_ref,
                  kn_ref, vn_ref, *, nblk):
    i = pl.program_id(1)

    def put(slot, k_src, v_src):
        dst = pl.ds(pl.multiple_of(slot * NA_BLOCK_TOK, NA_BLOCK_TOK), NA_BLOCK_TOK)
        for h in range(B_HEADS):
            hs = slice(h * B_DH, (h + 1) * B_DH)
            kn_ref[dst, hs] = _rms(k_src[:, hs].astype(F32), kw_ref[...]).astype(BF16)
        vn_ref[dst, :] = v_src[...].astype(BF16)

    @pl.when(i == 0)
    def _():
        put(0, kc_ref, vc_ref)

    nslot = (i + 1) % 3
    put(nslot, kx_ref, vx_ref)

    @pl.when((i == 0) | (nslot == 0))
    def _():
        mirror = slice(3 * NA_BLOCK_TOK, 4 * NA_BLOCK_TOK)
        kn_ref[mirror, :] = kn_ref[0:NA_BLOCK_TOK, :]
        vn_ref[mirror, :] = vn_ref[0:NA_BLOCK_TOK, :]

    qn = []
    for h in range(B_HEADS):
        hs = slice(h * B_DH, (h + 1) * B_DH)
        qn.append((_rms(q_ref[:, hs].astype(F32), qw_ref[...]) * (B_DH ** -0.5 * LOG2E)).astype(BF16))

    half = NA_BLOCK_TOK // 2
    eye = (lax.broadcasted_iota(jnp.int32, (B_DH, B_DH), 0)
           == lax.broadcasted_iota(jnp.int32, (B_DH, B_DH), 1)).astype(BF16)
    starts = (jnp.where(i == 0, 0, ((i + 2) % 3) * NA_BLOCK_TOK + half), (i % 3) * NA_BLOCK_TOK)
    variants = (jnp.where(i == 0, 1, 0), jnp.where(i == nblk - 1, 2, 0))
    for quad in range(NA_BLOCK_ROWS // NA_QUAD_ROWS):
        qrows = slice(quad * NA_QUAD_TOK, (quad + 1) * NA_QUAD_TOK)
        win = pl.ds(pl.multiple_of(starts[quad], half), NA_WIN_TOK)
        for h in range(B_HEADS):
            hs = slice(h * B_DH, (h + 1) * B_DH)
            sc = lax.dot_general(kn_ref[win, hs], qn[h][qrows, :], (((1,), (1,)), ((), ())),
                                 preferred_element_type=F32)
            sc = sc + bias_ref[variants[quad], h]
            m = jnp.max(sc, axis=0, keepdims=True)
            p = jnp.exp2(sc - m)
            l = jnp.sum(p, axis=0, keepdims=True)
            vt = lax.dot_general(eye, vn_ref[win, hs], (((1,), (1,)), ((), ())),
                                 preferred_element_type=F32).astype(BF16)
            o = (jnp.dot(vt, p.astype(BF16), preferred_element_type=F32) / l).T
            y_ref[qrows, hs] = (o * _silu(z_ref[qrows, hs].astype(F32))).astype(BF16)


def _nattn_call(p3, qw, kw, bias_quads):
    t = _tiles()
    bsz, seq, _ = p3.shape
    rows = seq // GRID_W
    assert rows >= WIN_ROWS and rows % NA_BLOCK_ROWS == 0
    nblk = rows // NA_BLOCK_ROWS

    def blk(col, shift):
        cb = col // BRANCH_W
        return pl.BlockSpec((None, NA_BLOCK_TOK, BRANCH_W),
                            lambda b, i: (b, jnp.clip(i + shift, 0, nblk - 1), cb))

    return pl.pallas_call(
        functools.partial(_nattn_kernel, nblk=nblk),
        grid=(bsz, nblk),
        in_specs=[
            blk(COL_BQ, 0),
            blk(COL_BK, 0), blk(COL_BK, 1),
            blk(COL_BV, 0), blk(COL_BV, 1),
            blk(COL_BZ, 0),
            pl.BlockSpec((1, B_DH), lambda b, i: (0, 0)),
            pl.BlockSpec((1, B_DH), lambda b, i: (0, 0)),
            pl.BlockSpec(bias_quads.shape, lambda b, i: (0, 0, 0, 0)),
        ],
        out_specs=pl.BlockSpec((None, NA_BLOCK_TOK, BRANCH_W), lambda b, i: (b, i, 0)),
        out_shape=jax.ShapeDtypeStruct((bsz, seq, BRANCH_W), BF16),
        scratch_shapes=[
            pltpu.VMEM((NA_RING_SLOTS * NA_BLOCK_TOK, BRANCH_W), BF16),
            pltpu.VMEM((NA_RING_SLOTS * NA_BLOCK_TOK, BRANCH_W), BF16),
        ],
        compiler_params=pltpu.CompilerParams(
            dimension_semantics=("arbitrary", "arbitrary"), vmem_limit_bytes=t["vmem_big"]),
        name="nattn",
    )(p3, p3, p3, p3, p3, p3, qw, kw, bias_quads)


def _skew(w, nq, first, ncol):
    width = w.shape[-1]
    assert first - (nq - 1) >= 0 and first + ncol <= width - 1
    lead = w.shape[:-1]
    flat = jnp.tile(w, (1,) * len(lead) + (nq,))[..., :nq * (width - 1)]
    return flat.reshape(lead + (nq, width - 1))[..., first:first + ncol]


def _nattn_bias(rpb):
    heads = rpb.shape[0]
    reach = GRID_W - WIN_COLS
    w = jnp.pad(rpb.astype(F32), ((0, 0), (0, 0), (reach, reach + 1)))
    tbl = _skew(w, GRID_W, GRID_W - 1, GRID_W)
    qc = np.arange(GRID_W)[:, None]
    kc = np.arange(GRID_W)[None, :]
    qs = np.clip(qc - WIN_COLS // 2, 0, GRID_W - WIN_COLS)
    ok = (kc >= qs) & (kc < qs + WIN_COLS)
    tbl = jnp.where(ok[None, None], tbl, -jnp.inf)
    per_d = [jnp.concatenate([tbl[:, d + j] for j in range(WIN_ROWS)], axis=-1) for d in range(WIN_ROWS)]
    fill = lambda n: jnp.full((heads, GRID_W, n), -jnp.inf, F32)

    def quad(ds, shifts):
        rows = [jnp.concatenate([fill(GRID_W * sh), per_d[d], fill(NA_WIN_TOK - NA_BLOCK_TOK - GRID_W * sh)],
                                axis=-1) for d, sh in zip(ds, shifts)]
        return jnp.concatenate(rows, axis=1)

    mid = WIN_ROWS - 1 - WIN_ROWS // 2
    interior = quad((mid,) * NA_QUAD_ROWS, range(NA_QUAD_ROWS))
    top = quad(range(WIN_ROWS - 1, WIN_ROWS - 1 - NA_QUAD_ROWS, -1), (0,) * NA_QUAD_ROWS)
    bottom = quad(range(NA_QUAD_ROWS - 1, -1, -1), (0,) * NA_QUAD_ROWS)
    return jnp.swapaxes(jnp.stack([interior, top, bottom]), 2, 3) * LOG2E


def _swa_kernel(q_ref, kp_ref, kc_ref, kx_ref, vp_ref, vc_ref, vx_ref, z_ref, qw_ref, kw_ref,
                sink_ref, bias_ref, y_ref, *, nsteps, nq):
    n = pl.program_id(1)
    band = 3 * C_BLOCK
    lane = lax.broadcasted_iota(jnp.int32, (1, V7X_LANES), 1)
    lo = lane < C_DH

    def pair_rms(x, w):
        x2 = x * x
        s_lo = jnp.sum(jnp.where(lo, x2, 0.0), axis=-1, keepdims=True)
        s_hi = jnp.sum(jnp.where(lo, 0.0, x2), axis=-1, keepdims=True)
        r = lax.rsqrt(jnp.where(lo, s_lo, s_hi) * (1.0 / C_DH) + EPS)
        return x * r * w

    kn = pair_rms(jnp.concatenate([kp_ref[...], kc_ref[...], kx_ref[...]], axis=0).astype(F32), kw_ref[...])
    vc = jnp.concatenate([vp_ref[...], vc_ref[...], vx_ref[...]], axis=0).astype(F32)
    kn_sw = pltpu.roll(kn, C_DH, axis=1)
    vc_sw = pltpu.roll(vc, C_DH, axis=1)
    stacks = []
    for g in range(C_KV_HEADS):
        k_lo, k_hi = (kn, kn_sw) if g == 0 else (kn_sw, kn)
        v_lo, v_hi = (vc, vc_sw) if g == 0 else (vc_sw, vc)
        stacks.append((jnp.where(lo, k_lo, 0.0).astype(BF16), jnp.where(lo, 0.0, k_hi).astype(BF16),
                       jnp.where(lo, v_lo, 0.0).astype(BF16), jnp.where(lo, 0.0, v_hi).astype(BF16)))

    ki = lax.broadcasted_iota(jnp.int32, (2 * band, 1), 0)
    ki = jnp.where(ki >= band, ki - band, ki)
    first_slab = lax.broadcasted_iota(jnp.int32, (1, 2 * C_BLOCK), 1) < C_BLOCK
    even_rows = lax.broadcasted_iota(jnp.int32, (V7X_LANES, 1), 0) < C_DH
    eye = (lax.broadcasted_iota(jnp.int32, (V7X_LANES, V7X_LANES), 0)
           == lax.broadcasted_iota(jnp.int32, (V7X_LANES, V7X_LANES), 1)).astype(BF16)
    for qb in range(nq):
        qrows = slice(qb * C_BLOCK, (qb + 1) * C_BLOCK)
        krows = slice((nq + qb - 1) * C_BLOCK, (nq + qb + 2) * C_BLOCK)
        has_prev = (n > 0) if qb == 0 else True
        has_next = (n < nsteps - 1) if qb == nq - 1 else True
        valid = ((ki >= C_BLOCK) | has_prev) & ((ki < 2 * C_BLOCK) | has_next)
        for g in range(C_KV_HEADS):
            k_even, k_odd, v_even, v_odd = stacks[g]
            kk = jnp.concatenate([k_even[krows], k_odd[krows]], axis=0)
            vv = jnp.concatenate([v_even[krows], v_odd[krows]], axis=0)
            slabs = [slice(p * V7X_LANES, (p + 1) * V7X_LANES) for p in (2 * g, 2 * g + 1)]
            qn = jnp.concatenate(
                [(pair_rms(q_ref[qrows, sl].astype(F32), qw_ref[...]) * (C_DH ** -0.5 * LOG2E)).astype(BF16)
                 for sl in slabs], axis=0)
            sc = lax.dot_general(kk, qn, (((1,), (1,)), ((), ())), preferred_element_type=F32)
            sc = jnp.where(valid, sc + bias_ref[g], -jnp.inf)
            probs, denoms = [], []
            for e in range(2):
                se = sc[e * band:(e + 1) * band, :]
                h0 = C_GROUP * g + e
                sink = jnp.where(first_slab, sink_ref[h0:h0 + 1, 0:1], sink_ref[h0 + 2:h0 + 3, 0:1])
                m = jnp.maximum(jnp.max(se, axis=0, keepdims=True), sink)
                pe = jnp.exp2(se - m)
                probs.append(pe.astype(BF16))
                denoms.append(jnp.sum(pe, axis=0, keepdims=True) + jnp.exp2(sink - m))
            vvt = lax.dot_general(eye, vv, (((1,), (1,)), ((), ())), preferred_element_type=F32).astype(BF16)
            ot = jnp.dot(vvt, jnp.concatenate(probs, axis=0), preferred_element_type=F32)
            o = (ot / jnp.where(even_rows, denoms[0], denoms[1])).T
            for j, sl in enumerate(slabs):
                rows = slice(j * C_BLOCK, (j + 1) * C_BLOCK)
                y_ref[qrows, sl] = (o[rows] * _silu(z_ref[qrows, sl].astype(F32))).astype(BF16)


def _swa_call(p3, qw, kw, sink_rows, bias):
    t = _tiles()
    bsz, seq, _ = p3.shape
    nq = SWA_STEP_BLOCKS
    rows = nq * C_BLOCK
    assert seq % rows == 0
    nsteps = seq // rows

    def wide(col):
        return pl.BlockSpec((None, rows, BRANCH_W), lambda b, n: (b, n, col // BRANCH_W))

    def kv(col, shift):
        return pl.BlockSpec((None, rows, C_KV_W),
                            lambda b, n: (b, jnp.clip(n + shift, 0, nsteps - 1), col // C_KV_W))

    return pl.pallas_call(
        functools.partial(_swa_kernel, nsteps=nsteps, nq=nq),
        grid=(bsz, nsteps),
        in_specs=[
            wide(COL_CQ),
            kv(COL_CK, -1), kv(COL_CK, 0), kv(COL_CK, 1),
            kv(COL_CV, -1), kv(COL_CV, 0), kv(COL_CV, 1),
            wide(COL_CZ),
            pl.BlockSpec((1, V7X_LANES), lambda b, n: (0, 0)),
            pl.BlockSpec((1, V7X_LANES), lambda b, n: (0, 0)),
            pl.BlockSpec(sink_rows.shape, lambda b, n: (0, 0)),
            pl.BlockSpec(bias.shape, lambda b, n: (0, 0, 0)),
        ],
        out_specs=pl.BlockSpec((None, rows, BRANCH_W), lambda b, n: (b, n, 0)),
        out_shape=jax.ShapeDtypeStruct((bsz, seq, BRANCH_W), BF16),
        compiler_params=pltpu.CompilerParams(
            dimension_semantics=("arbitrary", "arbitrary"), vmem_limit_bytes=t["vmem_small"]),
        name="swa",
    )(p3, p3, p3, p3, p3, p3, p3, p3, qw, kw, sink_rows, bias)


def _t5_bucket_rel():
    rel = np.arange(-(2 * C_BLOCK - 1), 2 * C_BLOCK)
    half = N_BUCKETS // 2
    max_exact = half // 2
    n = np.abs(rel)
    nf = np.maximum(n, 1).astype(np.float32)
    scale = np.float32(math.log(MAX_DIST / max_exact))
    large = max_exact + (np.log(nf / np.float32(max_exact)) / scale
                         * np.float32(half - max_exact)).astype(np.int32)
    large = np.minimum(large, half - 1)
    bucket = np.where(rel > 0, half, 0) + np.where(n < max_exact, n, large)
    return bucket, n <= C_WINDOW


def _swa_bias(rel_bias):
    bucket, in_window = _t5_bucket_rel()
    per_rel = jnp.where(in_window[:, None], rel_bias.astype(F32)[bucket], -jnp.inf)
    w = jnp.pad(per_rel.T, ((0, 0), (0, 1)))
    tbl = _skew(w, C_BLOCK, C_BLOCK - 1, 3 * C_BLOCK)
    slabs = jnp.concatenate([tbl[0::2], tbl[1::2]], axis=-1)
    return jnp.swapaxes(slabs.reshape(C_KV_HEADS, 2 * C_BLOCK, 2 * 3 * C_BLOCK), 1, 2) * LOG2E


def _conv_kernel(a_ref, g_ref, ap_ref, gp_ref, ax_ref, gx_ref, z_ref, cw_ref, cb_ref, lw_ref, lb_ref,
                 y_ref, ext_ref, sh_ref, *, ntile):
    i = pl.program_id(1)
    tt = a_ref.shape[0]
    glu = lambda a, g: a[...].astype(F32) * _sigmoid(g[...].astype(F32))
    prev = jnp.where(i > 0, glu(ap_ref, gp_ref), 0.0)
    nxt = jnp.where(i < ntile - 1, glu(ax_ref, gx_ref), 0.0)
    ext_ref[0:CONV_HALO, :] = prev
    ext_ref[CONV_HALO:CONV_HALO + tt, :] = glu(a_ref, g_ref)
    ext_ref[CONV_HALO + tt:2 * CONV_HALO + tt, :] = nxt
    span = sh_ref.shape[1]
    for r in range(1, V7X_SUBLANES):
        sh_ref[r - 1] = ext_ref[pl.ds(r, span), :]
    acc = None
    for w in range(CONV_W):
        a, r = divmod(CONV_HALO - CONV_W // 2 + w, V7X_SUBLANES)
        lo = a * V7X_SUBLANES
        src = ext_ref[lo:lo + tt, :] if r == 0 else sh_ref[r - 1, lo:lo + tt, :]
        term = src * cw_ref[w:w + 1, :]
        acc = term if acc is None else acc + term
    u = acc + cb_ref[...]
    mu = jnp.mean(u, axis=-1, keepdims=True)
    var = jnp.mean(jnp.square(u - mu), axis=-1, keepdims=True)
    u = (u - mu) * lax.rsqrt(var + EPS) * lw_ref[...] + lb_ref[...]
    y_ref[...] = (_silu(u) * _silu(z_ref[...].astype(F32))).astype(BF16)


def _conv_call(p3, cw_pad, cb, lw, lb):
    t = _tiles()
    bsz, seq, _ = p3.shape
    tt = CONV_TILE
    ntile = seq // tt
    per = tt // CONV_HALO
    nhalo = seq // CONV_HALO
    span = tt + (2 * CONV_HALO - 1) // V7X_SUBLANES * V7X_SUBLANES

    def cur(col):
        return pl.BlockSpec((None, tt, BRANCH_W), lambda b, i: (b, i, col // BRANCH_W))

    def halo(col, after):
        def idx(b, i):
            r = (i + 1) * per if after else i * per - 1
            return (b, jnp.clip(r, 0, nhalo - 1), col // BRANCH_W)
        return pl.BlockSpec((None, CONV_HALO, BRANCH_W), idx)

    vec = pl.BlockSpec((1, BRANCH_W), lambda b, i: (0, 0))
    return pl.pallas_call(
        functools.partial(_conv_kernel, ntile=ntile),
        grid=(bsz, ntile),
        in_specs=[
            cur(COL_DA), cur(COL_DG),
            halo(COL_DA, False), halo(COL_DG, False), halo(COL_DA, True), halo(COL_DG, True),
            cur(COL_DZ),
            pl.BlockSpec(cw_pad.shape, lambda b, i: (0, 0)),
            vec, vec, vec,
        ],
        out_specs=pl.BlockSpec((None, tt, BRANCH_W), lambda b, i: (b, i, 0)),
        out_shape=jax.ShapeDtypeStruct((bsz, seq, BRANCH_W), BF16),
        scratch_shapes=[
            pltpu.VMEM((tt + 2 * CONV_HALO, BRANCH_W), F32),
            pltpu.VMEM((V7X_SUBLANES - 1, span, BRANCH_W), F32),
        ],
        compiler_params=pltpu.CompilerParams(
            dimension_semantics=("arbitrary", "arbitrary"), vmem_limit_bytes=t["vmem_small"]),
        name="conv",
    )(p3, p3, p3, p3, p3, p3, p3, cw_pad, cb, lw, lb)


def _merge_kernel(h_ref, ya_ref, yb_ref, yc_ref, yd_ref, wma_ref, wmb_ref, wmc_ref, wmd_ref, wb_ref, o_ref):
    h = h_ref[...]
    merged = None
    branches = ((ya_ref, wma_ref), (yb_ref, wmb_ref), (yc_ref, wmc_ref), (yd_ref, wmd_ref))
    for b, (y_ref, wm_ref) in enumerate(branches):
        gate = _sigmoid(jnp.dot(h, wm_ref[...], preferred_element_type=F32))
        term = gate * jnp.dot(y_ref[...], wb_ref[b], preferred_element_type=F32)
        merged = term if merged is None else merged + term
    o_ref[...] = merged.astype(BF16)


def _merge_call(h2d, ys, w_merge, w_branch, seq):
    t = _tiles()
    m, d = h2d.shape
    tm, tn = min(t["mg_tm"], seq), t["mg_tn"]
    yspec = pl.BlockSpec((tm, BRANCH_W), lambda i, j: (i, 0))
    return pl.pallas_call(
        _merge_kernel,
        grid=(m // tm, d // tn),
        in_specs=[
            pl.BlockSpec((tm, d), lambda i, j: (i, 0)),
            yspec, yspec, yspec, yspec,
            *[pl.BlockSpec((d, tn), functools.partial(lambda i, j, b: (0, b * (d // tn) + j), b=b))
              for b in range(N_BRANCH)],
            pl.BlockSpec((N_BRANCH, BRANCH_W, tn), lambda i, j: (0, 0, j)),
        ],
        out_specs=pl.BlockSpec((tm, tn), lambda i, j: (i, j)),
        out_shape=jax.ShapeDtypeStruct((m, d), BF16),
        compiler_params=pltpu.CompilerParams(
            dimension_semantics=("arbitrary", "arbitrary"), vmem_limit_bytes=t["vmem_big"]),
        name="merge",
    )(h2d, *ys, *([w_merge] * N_BRANCH), w_branch)


def _outproj_kernel(x_ref, mg_ref, gate_ref, wo_ref, o_ref):
    o_ref[...] = x_ref[...] + gate_ref[...] * jnp.dot(mg_ref[...], wo_ref[...], preferred_element_type=F32)


def _outproj_call(x2d, merged, mod_rows, w_out, seq, first_row):
    t = _tiles()
    m, d = x2d.shape
    tm, tn = min(t["op_tm"], seq), t["op_tn"]
    row_of_tile = _row_of_tile(tm, seq, first_row)
    return pl.pallas_call(
        _outproj_kernel,
        grid=(m // tm, d // tn),
        in_specs=[
            pl.BlockSpec((tm, tn), lambda i, j: (i, j)),
            pl.BlockSpec((tm, d), lambda i, j: (i, 0)),
            pl.BlockSpec((None, 1, tn), lambda i, j: (row_of_tile(i) * 3 + 2, 0, j)),
            pl.BlockSpec((d, tn), lambda i, j: (0, j)),
        ],
        out_specs=pl.BlockSpec((tm, tn), lambda i, j: (i, j)),
        out_shape=jax.ShapeDtypeStruct((m, d), F32),
        compiler_params=pltpu.CompilerParams(
            dimension_semantics=("arbitrary", "arbitrary"), vmem_limit_bytes=t["vmem_big"]),
        name="outproj",
    )(x2d, merged, mod_rows, w_out)


def _pack_w_in(w):
    d = w.shape[0]
    sizes = (512, 512, 512, 512, 512, A_GATES, 512, 512, 512, 512, 512, C_KV_W, C_KV_W, 512, 1024, 512)
    offs = np.concatenate([[0], np.cumsum(sizes)])
    (aq, ak, av, ao, az, ag, bq, bk, bv, bz, cq, ck, cv, cz, dglu, dz) = [
        w[:, int(offs[i]):int(offs[i + 1])] for i in range(len(sizes))]
    pad = jnp.zeros((d, PACK_COLS - COL_AG - A_GATES), w.dtype)
    packed = jnp.concatenate([aq, ak, av, ao, az, bq, bk, bv, bz, dglu, dz, cq, cz, ck, cv, ag, pad], axis=1)
    return packed.astype(BF16), w[:, int(offs[-1]):].astype(BF16)


def kernel(x_prompt, x_sample, c_prompt, c_sample, rel_bias, norm_w, w_ada, b_ada, w_in, b_gate, mlstm_norm_w, na_q_norm, na_k_norm, na_rpb, swa_q_norm, swa_k_norm, swa_sink, conv_w, conv_b, conv_ln_w, conv_ln_b, w_branch, w_out):
    depth, d = norm_w.shape
    groups = ((x_prompt, 0), (x_sample, c_prompt.shape[0]))
    n_cond = c_prompt.shape[0] + c_sample.shape[0]
    cond_rows = -(-n_cond // V7X_SUBLANES) * V7X_SUBLANES
    c_all = jnp.concatenate([c_prompt, c_sample, jnp.zeros((cond_rows - n_cond, d), F32)], axis=0)
    mod = _ada_call(c_all, w_ada, b_ada)
    mod_rows = mod.reshape(depth * cond_rows * 3, 1, d)

    swa_bias = _swa_bias(rel_bias)
    layers = []
    for l in range(depth):
        w_pack, w_merge = _pack_w_in(w_in[l])
        layers.append(dict(
            w_pack=w_pack, w_merge=w_merge,
            w_branch=w_branch[l].astype(BF16), w_out=w_out[l].astype(BF16),
            nw=norm_w[l].reshape(1, d),
            bg=jnp.pad(b_gate[l], (0, V7X_LANES - A_GATES)).reshape(1, V7X_LANES),
            mnw=mlstm_norm_w[l].reshape(1, BRANCH_W),
            na_qw=na_q_norm[l].reshape(1, B_DH), na_kw=na_k_norm[l].reshape(1, B_DH),
            na_bias=_nattn_bias(na_rpb[l]),
            swa_qw=jnp.tile(swa_q_norm[l], 2).reshape(1, V7X_LANES),
            swa_kw=jnp.tile(swa_k_norm[l], 2).reshape(1, V7X_LANES),
            sink=jnp.broadcast_to(swa_sink[l].reshape(C_HEADS, 1) * LOG2E, (C_HEADS, V7X_LANES)),
            cw=jnp.pad(conv_w[l], ((0, 1), (0, 0))),
            cb=conv_b[l].reshape(1, BRANCH_W), lw=conv_ln_w[l].reshape(1, BRANCH_W),
            lb=conv_ln_b[l].reshape(1, BRANCH_W),
        ))

    outs = []
    for x, cond_off in groups:
        bsz, seq, _ = x.shape
        x2d = x.reshape(bsz * seq, d)
        for l, lw in enumerate(layers):
            first_row = l * cond_rows + cond_off
            p, gates, h2d = _inproj_call(x2d, lw["nw"], mod_rows, lw["w_pack"], seq, first_row)
            p3 = p.reshape(bsz, seq, PACK_COLS)
            ya = _mlstm_call(p3, gates.reshape(bsz, seq, V7X_LANES), lw["bg"], lw["mnw"])
            yb = _nattn_call(p3, lw["na_qw"], lw["na_kw"], lw["na_bias"])
            yc = _swa_call(p3, lw["swa_qw"], lw["swa_kw"], lw["sink"], swa_bias)
            yd = _conv_call(p3, lw["cw"], lw["cb"], lw["lw"], lw["lb"])
            ys = [y.reshape(bsz * seq, BRANCH_W) for y in (ya, yb, yc, yd)]
            merged = _merge_call(h2d, ys, lw["w_merge"], lw["w_branch"], seq)
            x2d = _outproj_call(x2d, merged, mod_rows, lw["w_out"], seq, first_row)
        outs.append(x2d.reshape(bsz, seq, d))
    return tuple(outs)
```

```python
import functools
import math

import numpy as np
import jax
import jax.numpy as jnp
from jax import lax
from jax.experimental import pallas as pl
from jax.experimental.pallas import tpu as pltpu

F32 = jnp.float32
BF16 = jnp.bfloat16

V7X_VMEM_BYTES = 64 * 1024 * 1024
V7X_LANES = 128
V7X_SUBLANES = 8
MIB = 1024 * 1024

EPS = 1e-6
N_BRANCH = 4
BRANCH_W = 512
GRID_W = 64
A_HEADS = 4
A_DH = BRANCH_W // A_HEADS
A_GATES = 4 * A_HEADS
CHUNK = 128
MLSTM_BLOCK_CHUNKS = 16
M_INIT = -1e30
LOG_K_SCALE = -0.5 * math.log(A_DH)
LOG2E = math.log2(math.e)
B_HEADS = 4
B_DH = BRANCH_W // B_HEADS
WIN_ROWS = 8
WIN_COLS = 16
NA_BLOCK_ROWS = 8
NA_BLOCK_TOK = NA_BLOCK_ROWS * GRID_W
NA_QUAD_ROWS = 4
NA_QUAD_TOK = NA_QUAD_ROWS * GRID_W
NA_WIN_TOK = (WIN_ROWS + NA_QUAD_ROWS) * GRID_W
NA_RING_SLOTS = 4
C_HEADS = 8
C_KV_HEADS = 2
C_GROUP = C_HEADS // C_KV_HEADS
C_DH = BRANCH_W // C_HEADS
C_KV_W = C_KV_HEADS * C_DH
C_WINDOW = 128
C_BLOCK = 128
SWA_STEP_BLOCKS = 1
N_BUCKETS = 32
MAX_DIST = 128
CONV_W = 31
CONV_HALO = 16
CONV_TILE = 512

COL_AQ, COL_AK, COL_AV, COL_AO, COL_AZ = 0, 512, 1024, 1536, 2048
COL_BQ, COL_BK, COL_BV, COL_BZ = 2560, 3072, 3584, 4096
COL_DA, COL_DG, COL_DZ = 4608, 5120, 5632
COL_CQ, COL_CZ = 6144, 6656
COL_CK, COL_CV, COL_AG = 7168, 7296, 7424
PACK_COLS = 7680


def _tiles():
    return dict(
        ada_tn=1536,
        in_tm=1024, in_tn=1536,
        mg_tm=1024, mg_tn=512,
        op_tm=1024, op_tn=1024,
        vmem_small=32 * MIB,
        vmem_big=V7X_VMEM_BYTES - 4 * MIB,
    )


def _sigmoid(x):
    return jax.nn.sigmoid(x)


def _silu(x):
    return x * jax.nn.sigmoid(x)


def _log_sigmoid(x):
    return jnp.minimum(x, 0.0) - jnp.log1p(jnp.exp(-jnp.abs(x)))


def _rms(x, w):
    r = lax.rsqrt(jnp.mean(x * x, axis=-1, keepdims=True) + EPS)
    return x * r * w


def _ada_kernel(c_ref, w_ref, b_ref, o_ref):
    c = c_ref[...]
    sc = _silu(c).astype(BF16)
    o_ref[...] = jnp.dot(sc, w_ref[...].astype(BF16), preferred_element_type=F32) + b_ref[...]


def _ada_call(c_all, w_ada, b_ada):
    t = _tiles()
    depth, d, n3 = w_ada.shape
    rows = c_all.shape[0]
    tn = t["ada_tn"]
    return pl.pallas_call(
        _ada_kernel,
        grid=(depth, n3 // tn),
        in_specs=[
            pl.BlockSpec((rows, d), lambda l, j: (0, 0)),
            pl.BlockSpec((None, d, tn), lambda l, j: (l, 0, j)),
            pl.BlockSpec((None, 1, tn), lambda l, j: (l, 0, j)),
        ],
        out_specs=pl.BlockSpec((None, rows, tn), lambda l, j: (l, 0, j)),
        out_shape=jax.ShapeDtypeStruct((depth, rows, n3), F32),
        compiler_params=pltpu.CompilerParams(
            dimension_semantics=("arbitrary", "arbitrary"), vmem_limit_bytes=t["vmem_small"]),
        name="ada",
    )(c_all, w_ada, b_ada.reshape(depth, 1, n3))


def _mod_spec(d, row_of_tile, kind):
    return pl.BlockSpec((None, 1, d), lambda i, j: (row_of_tile(i) * 3 + kind, 0, 0))


def _modulated(x, nw_ref, scale_ref, shift_ref):
    y = _rms(x, nw_ref[...])
    return (y * (1.0 + scale_ref[...]) + shift_ref[...]).astype(BF16)


def _inproj_kernel(xn_ref, nw_ref, scale0_ref, shift0_ref, scalen_ref, shiftn_ref, w_ref,
                   o_ref, g_ref, hout_ref, ha_ref, hb_ref, *, gate_tile, gate_col, nchunk):
    i = pl.program_id(0)
    j = pl.program_id(1)

    @pl.when((i == 0) & (j == 0))
    def _():
        ha_ref[...] = _modulated(xn_ref[...], nw_ref, scale0_ref, shift0_ref)

    def step(h_cur, h_nxt):
        rows = h_nxt.shape[0] // nchunk
        rs = pl.ds(pl.multiple_of(jnp.clip(j - 1, 0, nchunk - 1) * rows, rows), rows)
        h_nxt[rs, :] = _modulated(xn_ref[rs, :], nw_ref, scalen_ref, shiftn_ref)
        acc = jnp.dot(h_cur[...], w_ref[...], preferred_element_type=F32)
        o_ref[...] = acc.astype(o_ref.dtype)

        @pl.when(j == 0)
        def _():
            hout_ref[...] = h_cur[...]

        @pl.when(j == gate_tile)
        def _():
            g_ref[...] = acc[:, gate_col:gate_col + V7X_LANES]

    @pl.when(i % 2 == 0)
    def _():
        step(ha_ref, hb_ref)

    @pl.when(i % 2 == 1)
    def _():
        step(hb_ref, ha_ref)


def _row_of_tile(tm, seq, first_row):
    assert seq % tm == 0
    return lambda i: first_row + (i * tm) // seq


def _inproj_call(x2d, nw, mod_rows, w_pack, seq, first_row):
    t = _tiles()
    m, d = x2d.shape
    n = w_pack.shape[1]
    tm, tn = min(t["in_tm"], seq), t["in_tn"]
    ni, nj = m // tm, n // tn
    nchunk = max(c for c in (1, 2, 4, 8) if c < nj)
    row_of_tile = _row_of_tile(tm, seq, first_row)
    nxt = lambda i: jnp.minimum(i + 1, ni - 1)
    return pl.pallas_call(
        functools.partial(_inproj_kernel, gate_tile=COL_AG // tn, gate_col=COL_AG % tn, nchunk=nchunk),
        grid=(ni, nj),
        in_specs=[
            pl.BlockSpec((tm, d), lambda i, j: (jnp.where((i == 0) & (j == 0), 0, nxt(i)), 0)),
            pl.BlockSpec((1, d), lambda i, j: (0, 0)),
            _mod_spec(d, lambda i: row_of_tile(0), 1),
            _mod_spec(d, lambda i: row_of_tile(0), 0),
            _mod_spec(d, lambda i: row_of_tile(nxt(i)), 1),
            _mod_spec(d, lambda i: row_of_tile(nxt(i)), 0),
            pl.BlockSpec((d, tn), lambda i, j: (0, j)),
        ],
        out_specs=[pl.BlockSpec((tm, tn), lambda i, j: (i, j)),
                   pl.BlockSpec((tm, V7X_LANES), lambda i, j: (i, 0)),
                   pl.BlockSpec((tm, d), lambda i, j: (i, 0))],
        out_shape=[jax.ShapeDtypeStruct((m, n), BF16), jax.ShapeDtypeStruct((m, V7X_LANES), F32),
                   jax.ShapeDtypeStruct((m, d), BF16)],
        scratch_shapes=[pltpu.VMEM((tm, d), BF16), pltpu.VMEM((tm, d), BF16)],
        compiler_params=pltpu.CompilerParams(
            dimension_semantics=("arbitrary", "arbitrary"), vmem_limit_bytes=t["vmem_big"]),
        name="inproj",
    )(x2d, nw, mod_rows, mod_rows, mod_rows, mod_rows, w_pack)


def _mlstm_kernel(q_ref, k_ref, v_ref, g_ref, o_ref, z_ref, bg_ref, nw_ref, y_ref,
                  hfw_ref, c_ref, n_ref, m_ref, *, nblk, cb):
    s = pl.program_id(1)
    is_fwd = s < nblk
    blk = jnp.where(is_fwd, s, 2 * nblk - 1 - s)
    sgn = jnp.where(is_fwd, 1, -1)

    @pl.when((s == 0) | (s == nblk))
    def _():
        c_ref[...] = jnp.zeros_like(c_ref)
        n_ref[...] = jnp.zeros_like(n_ref)
        m_ref[...] = jnp.full_like(m_ref, M_INIT)

    ri = lax.broadcasted_iota(jnp.int32, (CHUNK, CHUNK), 0)
    ci = lax.broadcasted_iota(jnp.int32, (CHUNK, CHUNK), 1)
    tri = (((ri - ci) * sgn) >= 0).astype(F32)
    keep_t = ((ci - ri) * sgn) >= 0
    rw = lax.broadcasted_iota(jnp.int32, (BRANCH_W, BRANCH_W), 0)
    cw = lax.broadcasted_iota(jnp.int32, (BRANCH_W, BRANCH_W), 1)
    eye = (rw == cw).astype(BF16)
    bg = bg_ref[...]
    hsl = [slice(h * A_DH, (h + 1) * A_DH) for h in range(A_HEADS)]
    ct_st = [c_ref[:, hsl[h]] for h in range(A_HEADS)]
    n_st = [n_ref[0:1, hsl[h]] for h in range(A_HEADS)]
    m_st = [m_ref[h:h + 1, 0:1] for h in range(A_HEADS)]
    n_pad = jnp.zeros((2 * V7X_SUBLANES - 1, A_DH), BF16)

    chunks, houts = [], []
    for jj in range(cb):
        lc = jnp.where(is_fwd, jj, cb - 1 - jj)
        chunks.append(lc)
        tsl = pl.ds(pl.multiple_of(lc * CHUNK, CHUNK), CHUNK)
        gt = g_ref[tsl, :] + bg
        gt = jnp.where(is_fwd, gt, pltpu.roll(gt, V7X_LANES - A_HEADS, axis=1))
        logf = _log_sigmoid(gt)
        bc = jnp.dot(tri, logf, preferred_element_type=F32, precision=lax.Precision.HIGHEST)
        bc_t = bc.T
        g_row = jnp.where(is_fwd, bc[CHUNK - 1:CHUNK, :], bc[0:1, :])
        q_all = q_ref[tsl, :].astype(BF16)
        k_all = k_ref[tsl, :].astype(BF16)
        vt_all = lax.dot_general(eye, v_ref[tsl, :].astype(BF16), (((1,), (1,)), ((), ())),
                                 preferred_element_type=F32).astype(BF16)
        kq_pair, vt_bd, qcn_pair = [], [], []
        for p in range(A_HEADS // 2):
            ha, hb = 2 * p, 2 * p + 1
            psl = slice(ha * A_DH, (hb + 1) * A_DH)
            first = lax.broadcasted_iota(jnp.int32, (1, 2 * A_DH), 1) < A_DH
            k_pair, q_pair = k_all[:, psl], q_all[:, psl]
            zk = jnp.zeros_like(k_pair)
            k_bd = jnp.concatenate([jnp.where(first, k_pair, zk), jnp.where(first, zk, k_pair)], axis=0)
            kq_pair.append(lax.dot_general(k_bd, q_pair, (((1,), (1,)), ((), ())),
                                           preferred_element_type=F32))
            zblk = jnp.zeros((A_DH, CHUNK), BF16)
            vt_bd.append(jnp.concatenate([jnp.concatenate([vt_all[hsl[ha], :], zblk], axis=1),
                                          jnp.concatenate([zblk, vt_all[hsl[hb], :]], axis=1)], axis=0))
            ctn_a = jnp.concatenate([ct_st[ha].astype(BF16), n_st[ha].astype(BF16), n_pad], axis=0)
            ctn_b = jnp.concatenate([ct_st[hb].astype(BF16), n_st[hb].astype(BF16), n_pad], axis=0)
            zc = jnp.zeros_like(ctn_a)
            ctn_bd = jnp.concatenate([jnp.concatenate([ctn_a, zc], axis=1),
                                      jnp.concatenate([zc, ctn_b], axis=1)], axis=0)
            qcn_pair.append(lax.dot_general(ctn_bd, q_pair, (((1,), (1,)), ((), ())),
                                            preferred_element_type=F32))
        ctn_h = A_DH + 2 * V7X_SUBLANES

        smats, a_ins, m_ts, kas, a_olds, a_news = [], [], [], [], [], []
        for h in range(A_HEADS):
            fcol = 2 * A_HEADS + h
            b_col = bc[:, fcol:fcol + 1]
            i_col = gt[:, h:h + 1]
            g_tot = g_row[:, fcol:fcol + 1]
            rmat = jnp.where(keep_t, i_col - b_col, -jnp.inf)
            big_m = jnp.maximum(jnp.max(rmat, axis=0, keepdims=True), m_st[h])
            kq = kq_pair[h // 2][(h % 2) * CHUNK:(h % 2 + 1) * CHUNK, :]
            smats.append(kq * jnp.exp(rmat - (big_m - LOG_K_SCALE)))
            a_ins.append(jnp.exp(m_st[h] - big_m))
            m_ts.append(bc_t[fcol:fcol + 1, :] + big_m)

            w_end = g_tot - b_col + i_col
            m_loc = jnp.max(w_end, axis=0, keepdims=True)
            kas.append(k_all[:, hsl[h]].astype(F32) * jnp.exp(w_end - (m_loc - LOG_K_SCALE)))
            m_new = jnp.maximum(g_tot + m_st[h], m_loc)
            a_olds.append(jnp.exp(g_tot + m_st[h] - m_new))
            a_news.append(jnp.exp(m_loc - m_new))
            m_st[h] = m_new

        num_pair = [jnp.dot(vt_bd[p], jnp.concatenate(smats[2 * p:2 * p + 2], axis=0).astype(BF16),
                            preferred_element_type=F32) for p in range(A_HEADS // 2)]
        ct_loc_pair = [jnp.dot(vt_bd[p], jnp.concatenate(kas[2 * p:2 * p + 2], axis=0).astype(BF16),
                               preferred_element_type=F32) for p in range(A_HEADS // 2)]
        hrow = []
        for h in range(A_HEADS):
            p, j = h // 2, h % 2
            qcn = qcn_pair[p][j * ctn_h:(j + 1) * ctn_h, :]
            den = jnp.sum(smats[h], axis=0, keepdims=True) + a_ins[h] * qcn[A_DH:A_DH + 1, :]
            hrow.append((num_pair[p][j * A_DH:(j + 1) * A_DH, :] + a_ins[h] * qcn[:A_DH, :])
                        / jnp.maximum(jnp.abs(den), jnp.exp(-m_ts[h])))
            ct_st[h] = a_olds[h] * ct_st[h] + a_news[h] * ct_loc_pair[p][j * A_DH:(j + 1) * A_DH, :]
            n_st[h] = a_olds[h] * n_st[h] + a_news[h] * jnp.sum(kas[h], axis=0, keepdims=True)
        houts.append(hrow)

    for h in range(A_HEADS):
        c_ref[:, hsl[h]] = ct_st[h]
        n_ref[0:1, hsl[h]] = n_st[h]
        m_ref[h:h + 1, :] = jnp.broadcast_to(m_st[h], (1, V7X_LANES))

    @pl.when(is_fwd)
    def _():
        for jj in range(cb):
            for h in range(A_HEADS):
                hfw_ref[blk * cb + chunks[jj], hsl[h], :] = houts[jj][h]

    @pl.when(jnp.logical_not(is_fwd))
    def _():
        for jj in range(cb):
            tsl = pl.ds(pl.multiple_of(chunks[jj] * CHUNK, CHUNK), CHUNK)
            for h in range(A_HEADS):
                tot = hfw_ref[blk * cb + chunks[jj], hsl[h], :] + houts[jj][h]
                tot = tot * lax.rsqrt(jnp.mean(tot * tot, axis=0, keepdims=True) + EPS)
                yh = tot.T * nw_ref[:, hsl[h]] * _sigmoid(o_ref[tsl, hsl[h]].astype(F32))
                y_ref[tsl, hsl[h]] = (yh * _silu(z_ref[tsl, hsl[h]].astype(F32))).astype(BF16)


def _mlstm_call(p3, g3, bg_pad, nw):
    t = _tiles()
    bsz, seq, _ = p3.shape
    nc = seq // CHUNK
    cb = min(MLSTM_BLOCK_CHUNKS, nc)
    assert nc % cb == 0
    nblk = nc // cb
    rows = cb * CHUNK

    def cur(s):
        return jnp.where(s < nblk, s, 2 * nblk - 1 - s)

    def late(s):
        return jnp.where(s < nblk, nblk - 1, 2 * nblk - 1 - s)

    def blk(col, which):
        return pl.BlockSpec((None, rows, BRANCH_W), lambda b, s: (b, which(s), col // BRANCH_W))

    return pl.pallas_call(
        functools.partial(_mlstm_kernel, nblk=nblk, cb=cb),
        grid=(bsz, 2 * nblk),
        in_specs=[
            blk(COL_AQ, cur), blk(COL_AK, cur), blk(COL_AV, cur),
            pl.BlockSpec((None, rows, V7X_LANES), lambda b, s: (b, cur(s), 0)),
            blk(COL_AO, late), blk(COL_AZ, late),
            pl.BlockSpec((1, V7X_LANES), lambda b, s: (0, 0)),
            pl.BlockSpec((1, BRANCH_W), lambda b, s: (0, 0)),
        ],
        out_specs=pl.BlockSpec((None, rows, BRANCH_W), lambda b, s: (b, late(s), 0)),
        out_shape=jax.ShapeDtypeStruct((bsz, seq, BRANCH_W), BF16),
        scratch_shapes=[
            pltpu.VMEM((nc, BRANCH_W, CHUNK), F32),
            pltpu.VMEM((A_DH, BRANCH_W), F32),
            pltpu.VMEM((V7X_SUBLANES, BRANCH_W), F32),
            pltpu.VMEM((V7X_SUBLANES, V7X_LANES), F32),
        ],
        compiler_params=pltpu.CompilerParams(
            dimension_semantics=("arbitrary", "arbitrary"), vmem_limit_bytes=t["vmem_big"]),
        name="mlstm",
    )(p3, p3, p3, g3, p3, p3, bg_pad, nw)


def _nattn_kernel(q_ref, kc_ref, kx_ref, vc_ref, vx_ref, z_ref, qw_ref, kw_ref, bias_ref, y_ref,
                  kn_ref, vn_ref, *, nblk):
    i = pl.program_id(1)

    def put(slot, k_src, v_src):
        dst = pl.ds(pl.multiple_of(slot * NA_BLOCK_TOK, NA_BLOCK_TOK), NA_BLOCK_TOK)
        for h in range(B_HEADS):
            hs = slice(h * B_DH, (h + 1) * B_DH)
            kn_ref[dst, hs] = _rms(k_src[:, hs].astype(F32), kw_ref[...]).astype(BF16)
        vn_ref[dst, :] = v_src[...].astype(BF16)

    @pl.when(i == 0)
    def _():
        put(0, kc_ref, vc_ref)

    nslot = (i + 1) % 3
    put(nslot, kx_ref, vx_ref)

    @pl.when((i == 0) | (nslot == 0))
    def _():
        mirror = slice(3 * NA_BLOCK_TOK, 4 * NA_BLOCK_TOK)
        kn_ref[mirror, :] = kn_ref[0:NA_BLOCK_TOK, :]
        vn_ref[mirror, :] = vn_ref[0:NA_BLOCK_TOK, :]

    qn = []
    for h in range(B_HEADS):
        hs = slice(h * B_DH, (h + 1) * B_DH)
        qn.append((_rms(q_ref[:, hs].astype(F32), qw_ref[...]) * (B_DH ** -0.5 * LOG2E)).astype(BF16))

    half = NA_BLOCK_TOK // 2
    eye = (lax.broadcasted_iota(jnp.int32, (B_DH, B_DH), 0)
           == lax.broadcasted_iota(jnp.int32, (B_DH, B_DH), 1)).astype(BF16)
    starts = (jnp.where(i == 0, 0, ((i + 2) % 3) * NA_BLOCK_TOK + half), (i % 3) * NA_BLOCK_TOK)
    variants = (jnp.where(i == 0, 1, 0), jnp.where(i == nblk - 1, 2, 0))
    for quad in range(NA_BLOCK_ROWS // NA_QUAD_ROWS):
        qrows = slice(quad * NA_QUAD_TOK, (quad + 1) * NA_QUAD_TOK)
        win = pl.ds(pl.multiple_of(starts[quad], half), NA_WIN_TOK)
        for h in range(B_HEADS):
            hs = slice(h * B_DH, (h + 1) * B_DH)
            sc = lax.dot_general(kn_ref[win, hs], qn[h][qrows, :], (((1,), (1,)), ((), ())),
                                 preferred_element_type=F32)
            sc = sc + bias_ref[variants[quad], h]
            m = jnp.max(sc, axis=0, keepdims=True)
            p = jnp.exp2(sc - m)
            l = jnp.sum(p, axis=0, keepdims=True)
            vt = lax.dot_general(eye, vn_ref[win, hs], (((1,), (1,)), ((), ())),
                                 preferred_element_type=F32).astype(BF16)
            o = (jnp.dot(vt, p.astype(BF16), preferred_element_type=F32) / l).T
            y_ref[qrows, hs] = (o * _silu(z_ref[qrows, hs].astype(F32))).astype(BF16)


def _nattn_call(p3, qw, kw, bias_quads):
    t = _tiles()
    bsz, seq, _ = p3.shape
    rows = seq // GRID_W
    assert rows >= WIN_ROWS and rows % NA_BLOCK_ROWS == 0
    nblk = rows // NA_BLOCK_ROWS

    def blk(col, shift):
        cb = col // BRANCH_W
        return pl.BlockSpec((None, NA_BLOCK_TOK, BRANCH_W),
                            lambda b, i: (b, jnp.clip(i + shift, 0, nblk - 1), cb))

    return pl.pallas_call(
        functools.partial(_nattn_kernel, nblk=nblk),
        grid=(bsz, nblk),
        in_specs=[
            blk(COL_BQ, 0),
            blk(COL_BK, 0), blk(COL_BK, 1),
            blk(COL_BV, 0), blk(COL_BV, 1),
            blk(COL_BZ, 0),
            pl.BlockSpec((1, B_DH), lambda b, i: (0, 0)),
            pl.BlockSpec((1, B_DH), lambda b, i: (0, 0)),
            pl.BlockSpec(bias_quads.shape, lambda b, i: (0, 0, 0, 0)),
        ],
        out_specs=pl.BlockSpec((None, NA_BLOCK_TOK, BRANCH_W), lambda b, i: (b, i, 0)),
        out_shape=jax.ShapeDtypeStruct((bsz, seq, BRANCH_W), BF16),
        scratch_shapes=[
            pltpu.VMEM((NA_RING_SLOTS * NA_BLOCK_TOK, BRANCH_W), BF16),
            pltpu.VMEM((NA_RING_SLOTS * NA_BLOCK_TOK, BRANCH_W), BF16),
        ],
        compiler_params=pltpu.CompilerParams(
            dimension_semantics=("arbitrary", "arbitrary"), vmem_limit_bytes=t["vmem_big"]),
        name="nattn",
    )(p3, p3, p3, p3, p3, p3, qw, kw, bias_quads)


def _skew(w, nq, first, ncol):
    width = w.shape[-1]
    assert first - (nq - 1) >= 0 and first + ncol <= width - 1
    lead = w.shape[:-1]
    flat = jnp.tile(w, (1,) * len(lead) + (nq,))[..., :nq * (width - 1)]
    return flat.reshape(lead + (nq, width - 1))[..., first:first + ncol]


def _nattn_bias(rpb):
    heads = rpb.shape[0]
    reach = GRID_W - WIN_COLS
    w = jnp.pad(rpb.astype(F32), ((0, 0), (0, 0), (reach, reach + 1)))
    tbl = _skew(w, GRID_W, GRID_W - 1, GRID_W)
    qc = np.arange(GRID_W)[:, None]
    kc = np.arange(GRID_W)[None, :]
    qs = np.clip(qc - WIN_COLS // 2, 0, GRID_W - WIN_COLS)
    ok = (kc >= qs) & (kc < qs + WIN_COLS)
    tbl = jnp.where(ok[None, None], tbl, -jnp.inf)
    per_d = [jnp.concatenate([tbl[:, d + j] for j in range(WIN_ROWS)], axis=-1) for d in range(WIN_ROWS)]
    fill = lambda n: jnp.full((heads, GRID_W, n), -jnp.inf, F32)

    def quad(ds, shifts):
        rows = [jnp.concatenate([fill(GRID_W * sh), per_d[d], fill(NA_WIN_TOK - NA_BLOCK_TOK - GRID_W * sh)],
                                axis=-1) for d, sh in zip(ds, shifts)]
        return jnp.concatenate(rows, axis=1)

    mid = WIN_ROWS - 1 - WIN_ROWS // 2
    interior = quad((mid,) * NA_QUAD_ROWS, range(NA_QUAD_ROWS))
    top = quad(range(WIN_ROWS - 1, WIN_ROWS - 1 - NA_QUAD_ROWS, -1), (0,) * NA_QUAD_ROWS)
    bottom = quad(range(NA_QUAD_ROWS - 1, -1, -1), (0,) * NA_QUAD_ROWS)
    return jnp.swapaxes(jnp.stack([interior, top, bottom]), 2, 3) * LOG2E


def _swa_kernel(q_ref, kp_ref, kc_ref, kx_ref, vp_ref, vc_ref, vx_ref, z_ref, qw_ref, kw_ref,
                sink_ref, bias_ref, y_ref, *, nsteps, nq):
    n = pl.program_id(1)
    band = 3 * C_BLOCK
    lane = lax.broadcasted_iota(jnp.int32, (1, V7X_LANES), 1)
    lo = lane < C_DH

    def pair_rms(x, w):
        x2 = x * x
        s_lo = jnp.sum(jnp.where(lo, x2, 0.0), axis=-1, keepdims=True)
        s_hi = jnp.sum(jnp.where(lo, 0.0, x2), axis=-1, keepdims=True)
        r = lax.rsqrt(jnp.where(lo, s_lo, s_hi) * (1.0 / C_DH) + EPS)
        return x * r * w

    kn = pair_rms(jnp.concatenate([kp_ref[...], kc_ref[...], kx_ref[...]], axis=0).astype(F32), kw_ref[...])
    vc = jnp.concatenate([vp_ref[...], vc_ref[...], vx_ref[...]], axis=0).astype(F32)
    kn_sw = pltpu.roll(kn, C_DH, axis=1)
    vc_sw = pltpu.roll(vc, C_DH, axis=1)
    stacks = []
    for g in range(C_KV_HEADS):
        k_lo, k_hi = (kn, kn_sw) if g == 0 else (kn_sw, kn)
        v_lo, v_hi = (vc, vc_sw) if g == 0 else (vc_sw, vc)
        stacks.append((jnp.where(lo, k_lo, 0.0).astype(BF16), jnp.where(lo, 0.0, k_hi).astype(BF16),
                       jnp.where(lo, v_lo, 0.0).astype(BF16), jnp.where(lo, 0.0, v_hi).astype(BF16)))

    ki = lax.broadcasted_iota(jnp.int32, (2 * band, 1), 0)
    ki = jnp.where(ki >= band, ki - band, ki)
    first_slab = lax.broadcasted_iota(jnp.int32, (1, 2 * C_BLOCK), 1) < C_BLOCK
    even_rows = lax.broadcasted_iota(jnp.int32, (V7X_LANES, 1), 0) < C_DH
    eye = (lax.broadcasted_iota(jnp.int32, (V7X_LANES, V7X_LANES), 0)
           == lax.broadcasted_iota(jnp.int32, (V7X_LANES, V7X_LANES), 1)).astype(BF16)
    for qb in range(nq):
        qrows = slice(qb * C_BLOCK, (qb + 1) * C_BLOCK)
        krows = slice((nq + qb - 1) * C_BLOCK, (nq + qb + 2) * C_BLOCK)
        has_prev = (n > 0) if qb == 0 else True
        has_next = (n < nsteps - 1) if qb == nq - 1 else True
        valid = ((ki >= C_BLOCK) | has_prev) & ((ki < 2 * C_BLOCK) | has_next)
        for g in range(C_KV_HEADS):
            k_even, k_odd, v_even, v_odd = stacks[g]
            kk = jnp.concatenate([k_even[krows], k_odd[krows]], axis=0)
            vv = jnp.concatenate([v_even[krows], v_odd[krows]], axis=0)
            slabs = [slice(p * V7X_LANES, (p + 1) * V7X_LANES) for p in (2 * g, 2 * g + 1)]
            qn = jnp.concatenate(
                [(pair_rms(q_ref[qrows, sl].astype(F32), qw_ref[...]) * (C_DH ** -0.5 * LOG2E)).astype(BF16)
                 for sl in slabs], axis=0)
            sc = lax.dot_general(kk, qn, (((1,), (1,)), ((), ())), preferred_element_type=F32)
            sc = jnp.where(valid, sc + bias_ref[g], -jnp.inf)
            probs, denoms = [], []
            for e in range(2):
                se = sc[e * band:(e + 1) * band, :]
                h0 = C_GROUP * g + e
                sink = jnp.where(first_slab, sink_ref[h0:h0 + 1, 0:1], sink_ref[h0 + 2:h0 + 3, 0:1])
                m = jnp.maximum(jnp.max(se, axis=0, keepdims=True), sink)
                pe = jnp.exp2(se - m)
                probs.append(pe.astype(BF16))
                denoms.append(jnp.sum(pe, axis=0, keepdims=True) + jnp.exp2(sink - m))
            vvt = lax.dot_general(eye, vv, (((1,), (1,)), ((), ())), preferred_element_type=F32).astype(BF16)
            ot = jnp.dot(vvt, jnp.concatenate(probs, axis=0), preferred_element_type=F32)
            o = (ot / jnp.where(even_rows, denoms[0], denoms[1])).T
            for j, sl in enumerate(slabs):
                rows = slice(j * C_BLOCK, (j + 1) * C_BLOCK)
                y_ref[qrows, sl] = (o[rows] * _silu(z_ref[qrows, sl].astype(F32))).astype(BF16)


def _swa_call(p3, qw, kw, sink_rows, bias):
    t = _tiles()
    bsz, seq, _ = p3.shape
    nq = SWA_STEP_BLOCKS
    rows = nq * C_BLOCK
    assert seq % rows == 0
    nsteps = seq // rows

    def wide(col):
        return pl.BlockSpec((None, rows, BRANCH_W), lambda b, n: (b, n, col // BRANCH_W))

    def kv(col, shift):
        return pl.BlockSpec((None, rows, C_KV_W),
                            lambda b, n: (b, jnp.clip(n + shift, 0, nsteps - 1), col // C_KV_W))

    return pl.pallas_call(
        functools.partial(_swa_kernel, nsteps=nsteps, nq=nq),
        grid=(bsz, nsteps),
        in_specs=[
            wide(COL_CQ),
            kv(COL_CK, -1), kv(COL_CK, 0), kv(COL_CK, 1),
            kv(COL_CV, -1), kv(COL_CV, 0), kv(COL_CV, 1),
            wide(COL_CZ),
            pl.BlockSpec((1, V7X_LANES), lambda b, n: (0, 0)),
            pl.BlockSpec((1, V7X_LANES), lambda b, n: (0, 0)),
            pl.BlockSpec(sink_rows.shape, lambda b, n: (0, 0)),
            pl.BlockSpec(bias.shape, lambda b, n: (0, 0, 0)),
        ],
        out_specs=pl.BlockSpec((None, rows, BRANCH_W), lambda b, n: (b, n, 0)),
        out_shape=jax.ShapeDtypeStruct((bsz, seq, BRANCH_W), BF16),
        compiler_params=pltpu.CompilerParams(
            dimension_semantics=("arbitrary", "arbitrary"), vmem_limit_bytes=t["vmem_small"]),
        name="swa",
    )(p3, p3, p3, p3, p3, p3, p3, p3, qw, kw, sink_rows, bias)


def _t5_bucket_rel():
    rel = np.arange(-(2 * C_BLOCK - 1), 2 * C_BLOCK)
    half = N_BUCKETS // 2
    max_exact = half // 2
    n = np.abs(rel)
    nf = np.maximum(n, 1).astype(np.float32)
    scale = np.float32(math.log(MAX_DIST / max_exact))
    large = max_exact + (np.log(nf / np.float32(max_exact)) / scale
                         * np.float32(half - max_exact)).astype(np.int32)
    large = np.minimum(large, half - 1)
    bucket = np.where(rel > 0, half, 0) + np.where(n < max_exact, n, large)
    return bucket, n <= C_WINDOW


def _swa_bias(rel_bias):
    bucket, in_window = _t5_bucket_rel()
    per_rel = jnp.where(in_window[:, None], rel_bias.astype(F32)[bucket], -jnp.inf)
    w = jnp.pad(per_rel.T, ((0, 0), (0, 1)))
    tbl = _skew(w, C_BLOCK, C_BLOCK - 1, 3 * C_BLOCK)
    slabs = jnp.concatenate([tbl[0::2], tbl[1::2]], axis=-1)
    return jnp.swapaxes(slabs.reshape(C_KV_HEADS, 2 * C_BLOCK, 2 * 3 * C_BLOCK), 1, 2) * LOG2E


def _conv_kernel(a_ref, g_ref, ap_ref, gp_ref, ax_ref, gx_ref, z_ref, cw_ref, cb_ref, lw_ref, lb_ref,
                 y_ref, ext_ref, sh_ref, *, ntile):
    i = pl.program_id(1)
    tt = a_ref.shape[0]
    glu = lambda a, g: a[...].astype(F32) * _sigmoid(g[...].astype(F32))
    prev = jnp.where(i > 0, glu(ap_ref, gp_ref), 0.0)
    nxt = jnp.where(i < ntile - 1, glu(ax_ref, gx_ref), 0.0)
    ext_ref[0:CONV_HALO, :] = prev
    ext_ref[CONV_HALO:CONV_HALO + tt, :] = glu(a_ref, g_ref)
    ext_ref[CONV_HALO + tt:2 * CONV_HALO + tt, :] = nxt
    span = sh_ref.shape[1]
    for r in range(1, V7X_SUBLANES):
        sh_ref[r - 1] = ext_ref[pl.ds(r, span), :]
    acc = None
    for w in range(CONV_W):
        a, r = divmod(CONV_HALO - CONV_W // 2 + w, V7X_SUBLANES)
        lo = a * V7X_SUBLANES
        src = ext_ref[lo:lo + tt, :] if r == 0 else sh_ref[r - 1, lo:lo + tt, :]
        term = src * cw_ref[w:w + 1, :]
        acc = term if acc is None else acc + term
    u = acc + cb_ref[...]
    mu = jnp.mean(u, axis=-1, keepdims=True)
    var = jnp.mean(jnp.square(u - mu), axis=-1, keepdims=True)
    u = (u - mu) * lax.rsqrt(var + EPS) * lw_ref[...] + lb_ref[...]
    y_ref[...] = (_silu(u) * _silu(z_ref[...].astype(F32))).astype(BF16)


def _conv_call(p3, cw_pad, cb, lw, lb):
    t = _tiles()
    bsz, seq, _ = p3.shape
    tt = CONV_TILE
    ntile = seq // tt
    per = tt // CONV_HALO
    nhalo = seq // CONV_HALO
    span = tt + (2 * CONV_HALO - 1) // V7X_SUBLANES * V7X_SUBLANES

    def cur(col):
        return pl.BlockSpec((None, tt, BRANCH_W), lambda b, i: (b, i, col // BRANCH_W))

    def halo(col, after):
        def idx(b, i):
            r = (i + 1) * per if after else i * per - 1
            return (b, jnp.clip(r, 0, nhalo - 1), col // BRANCH_W)
        return pl.BlockSpec((None, CONV_HALO, BRANCH_W), idx)

    vec = pl.BlockSpec((1, BRANCH_W), lambda b, i: (0, 0))
    return pl.pallas_call(
        functools.partial(_conv_kernel, ntile=ntile),
        grid=(bsz, ntile),
        in_specs=[
            cur(COL_DA), cur(COL_DG),
            halo(COL_DA, False), halo(COL_DG, False), halo(COL_DA, True), halo(COL_DG, True),
            cur(COL_DZ),
            pl.BlockSpec(cw_pad.shape, lambda b, i: (0, 0)),
            vec, vec, vec,
        ],
        out_specs=pl.BlockSpec((None, tt, BRANCH_W), lambda b, i: (b, i, 0)),
        out_shape=jax.ShapeDtypeStruct((bsz, seq, BRANCH_W), BF16),
        scratch_shapes=[
            pltpu.VMEM((tt + 2 * CONV_HALO, BRANCH_W), F32),
            pltpu.VMEM((V7X_SUBLANES - 1, span, BRANCH_W), F32),
        ],
        compiler_params=pltpu.CompilerParams(
            dimension_semantics=("arbitrary", "arbitrary"), vmem_limit_bytes=t["vmem_small"]),
        name="conv",
    )(p3, p3, p3, p3, p3, p3, p3, cw_pad, cb, lw, lb)


def _merge_kernel(h_ref, ya_ref, yb_ref, yc_ref, yd_ref, wma_ref, wmb_ref, wmc_ref, wmd_ref, wb_ref, o_ref):
    h = h_ref[...]
    merged = None
    branches = ((ya_ref, wma_ref), (yb_ref, wmb_ref), (yc_ref, wmc_ref), (yd_ref, wmd_ref))
    for b, (y_ref, wm_ref) in enumerate(branches):
        gate = _sigmoid(jnp.dot(h, wm_ref[...], preferred_element_type=F32))
        term = gate * jnp.dot(y_ref[...], wb_ref[b], preferred_element_type=F32)
        merged = term if merged is None else merged + term
    o_ref[...] = merged.astype(BF16)


def _merge_call(h2d, ys, w_merge, w_branch, seq):
    t = _tiles()
    m, d = h2d.shape
    tm, tn = min(t["mg_tm"], seq), t["mg_tn"]
    yspec = pl.BlockSpec((tm, BRANCH_W), lambda i, j: (i, 0))
    return pl.pallas_call(
        _merge_kernel,
        grid=(m // tm, d // tn),
        in_specs=[
            pl.BlockSpec((tm, d), lambda i, j: (i, 0)),
            yspec, yspec, yspec, yspec,
            *[pl.BlockSpec((d, tn), functools.partial(lambda i, j, b: (0, b * (d // tn) + j), b=b))
              for b in range(N_BRANCH)],
            pl.BlockSpec((N_BRANCH, BRANCH_W, tn), lambda i, j: (0, 0, j)),
        ],
        out_specs=pl.BlockSpec((tm, tn), lambda i, j: (i, j)),
        out_shape=jax.ShapeDtypeStruct((m, d), BF16),
        compiler_params=pltpu.CompilerParams(
            dimension_semantics=("arbitrary", "arbitrary"), vmem_limit_bytes=t["vmem_big"]),
        name="merge",
    )(h2d, *ys, *([w_merge] * N_BRANCH), w_branch)


def _outproj_kernel(x_ref, mg_ref, gate_ref, wo_ref, o_ref):
    o_ref[...] = x_ref[...] + gate_ref[...] * jnp.dot(mg_ref[...], wo_ref[...], preferred_element_type=F32)


def _outproj_call(x2d, merged, mod_rows, w_out, seq, first_row):
    t = _tiles()
    m, d = x2d.shape
    tm, tn = min(t["op_tm"], seq), t["op_tn"]
    row_of_tile = _row_of_tile(tm, seq, first_row)
    return pl.pallas_call(
        _outproj_kernel,
        grid=(m // tm, d // tn),
        in_specs=[
            pl.BlockSpec((tm, tn), lambda i, j: (i, j)),
            pl.BlockSpec((tm, d), lambda i, j: (i, 0)),
            pl.BlockSpec((None, 1, tn), lambda i, j: (row_of_tile(i) * 3 + 2, 0, j)),
            pl.BlockSpec((d, tn), lambda i, j: (0, j)),
        ],
        out_specs=pl.BlockSpec((tm, tn), lambda i, j: (i, j)),
        out_shape=jax.ShapeDtypeStruct((m, d), F32),
        compiler_params=pltpu.CompilerParams(
            dimension_semantics=("arbitrary", "arbitrary"), vmem_limit_bytes=t["vmem_big"]),
        name="outproj",
    )(x2d, merged, mod_rows, w_out)


def _pack_w_in(w):
    d = w.shape[0]
    sizes = (512, 512, 512, 512, 512, A_GATES, 512, 512, 512, 512, 512, C_KV_W, C_KV_W, 512, 1024, 512)
    offs = np.concatenate([[0], np.cumsum(sizes)])
    (aq, ak, av, ao, az, ag, bq, bk, bv, bz, cq, ck, cv, cz, dglu, dz) = [
        w[:, int(offs[i]):int(offs[i + 1])] for i in range(len(sizes))]
    pad = jnp.zeros((d, PACK_COLS - COL_AG - A_GATES), w.dtype)
    packed = jnp.concatenate([aq, ak, av, ao, az, bq, bk, bv, bz, dglu, dz, cq, cz, ck, cv, ag, pad], axis=1)
    return packed.astype(BF16), w[:, int(offs[-1]):].astype(BF16)


def kernel(x_prompt, x_sample, c_prompt, c_sample, rel_bias, norm_w, w_ada, b_ada, w_in, b_gate, mlstm_norm_w, na_q_norm, na_k_norm, na_rpb, swa_q_norm, swa_k_norm, swa_sink, conv_w, conv_b, conv_ln_w, conv_ln_b, w_branch, w_out):
    depth, d = norm_w.shape
    groups = ((x_prompt, 0), (x_sample, c_prompt.shape[0]))
    n_cond = c_prompt.shape[0] + c_sample.shape[0]
    cond_rows = -(-n_cond // V7X_SUBLANES) * V7X_SUBLANES
    c_all = jnp.concatenate([c_prompt, c_sample, jnp.zeros((cond_rows - n_cond, d), F32)], axis=0)
    mod = _ada_call(c_all, w_ada, b_ada)
    mod_rows = mod.reshape(depth * cond_rows * 3, 1, d)

    swa_bias = _swa_bias(rel_bias)
    layers = []
    for l in range(depth):
        w_pack, w_merge = _pack_w_in(w_in[l])
        layers.append(dict(
            w_pack=w_pack, w_merge=w_merge,
            w_branch=w_branch[l].astype(BF16), w_out=w_out[l].astype(BF16),
            nw=norm_w[l].reshape(1, d),
            bg=jnp.pad(b_gate[l], (0, V7X_LANES - A_GATES)).reshape(1, V7X_LANES),
            mnw=mlstm_norm_w[l].reshape(1, BRANCH_W),
            na_qw=na_q_norm[l].reshape(1, B_DH), na_kw=na_k_norm[l].reshape(1, B_DH),
            na_bias=_nattn_bias(na_rpb[l]),
            swa_qw=jnp.tile(swa_q_norm[l], 2).reshape(1, V7X_LANES),
            swa_kw=jnp.tile(swa_k_norm[l], 2).reshape(1, V7X_LANES),
            sink=jnp.broadcast_to(swa_sink[l].reshape(C_HEADS, 1) * LOG2E, (C_HEADS, V7X_LANES)),
            cw=jnp.pad(conv_w[l], ((0, 1), (0, 0))),
            cb=conv_b[l].reshape(1, BRANCH_W), lw=conv_ln_w[l].reshape(1, BRANCH_W),
            lb=conv_ln_b[l].reshape(1, BRANCH_W),
        ))

    outs = []
    for x, cond_off in groups:
        bsz, seq, _ = x.shape
        x2d = x.reshape(bsz * seq, d)
        for l, lw in enumerate(layers):
            first_row = l * cond_rows + cond_off
            p, gates, h2d = _inproj_call(x2d, lw["nw"], mod_rows, lw["w_pack"], seq, first_row)
            p3 = p.reshape(bsz, seq, PACK_COLS)
            ya = _mlstm_call(p3, gates.reshape(bsz, seq, V7X_LANES), lw["bg"], lw["mnw"])
            yb = _nattn_call(p3, lw["na_qw"], lw["na_kw"], lw["na_bias"])
            yc = _swa_call(p3, lw["swa_qw"], lw["swa_kw"], lw["sink"], swa_bias)
            yd = _conv_call(p3, lw["cw"], lw["cb"], lw["lw"], lw["lb"])
            ys = [y.reshape(bsz * seq, BRANCH_W) for y in (ya, yb, yc, yd)]
            merged = _merge_call(h2d, ys, lw["w_merge"], lw["w_branch"], seq)
            x2d = _outproj_call(x2d, merged, mod_rows, lw["w_out"], seq, first_row)
        outs.append(x2d.reshape(bsz, seq, d))
    return tuple(outs)
```

```python
import functools
import math

import numpy as np
import jax
import jax.numpy as jnp
from jax import lax
from jax.experimental import pallas as pl
from jax.experimental.pallas import tpu as pltpu

F32 = jnp.float32
BF16 = jnp.bfloat16

V7X_VMEM_BYTES = 64 * 1024 * 1024
V7X_LANES = 128
V7X_SUBLANES = 8
MIB = 1024 * 1024

EPS = 1e-6
N_BRANCH = 4
BRANCH_W = 512
GRID_W = 64
A_HEADS = 4
A_DH = BRANCH_W // A_HEADS
A_GATES = 4 * A_HEADS
CHUNK = 128
MLSTM_BLOCK_CHUNKS = 16
M_INIT = -1e30
LOG_K_SCALE = -0.5 * math.log(A_DH)
LOG2E = math.log2(math.e)
B_HEADS = 4
B_DH = BRANCH_W // B_HEADS
WIN_ROWS = 8
WIN_COLS = 16
NA_BLOCK_ROWS = 8
NA_BLOCK_TOK = NA_BLOCK_ROWS * GRID_W
NA_QUAD_ROWS = 4
NA_QUAD_TOK = NA_QUAD_ROWS * GRID_W
NA_WIN_TOK = (WIN_ROWS + NA_QUAD_ROWS) * GRID_W
NA_RING_SLOTS = 4
C_HEADS = 8
C_KV_HEADS = 2
C_GROUP = C_HEADS // C_KV_HEADS
C_DH = BRANCH_W // C_HEADS
C_KV_W = C_KV_HEADS * C_DH
C_WINDOW = 128
C_BLOCK = 128
SWA_STEP_BLOCKS = 1
N_BUCKETS = 32
MAX_DIST = 128
CONV_W = 31
CONV_HALO = 16
CONV_TILE = 512

COL_AQ, COL_AK, COL_AV, COL_AO, COL_AZ = 0, 512, 1024, 1536, 2048
COL_BQ, COL_BK, COL_BV, COL_BZ = 2560, 3072, 3584, 4096
COL_DA, COL_DG, COL_DZ = 4608, 5120, 5632
COL_CQ, COL_CZ = 6144, 6656
COL_CK, COL_CV, COL_AG = 7168, 7296, 7424
PACK_COLS = 7680


def _tiles():
    return dict(
        ada_tn=1536,
        in_tm=1024, in_tn=1536,
        mg_tm=1024, mg_tn=512,
        op_tm=2048, op_tn=512,
        vmem_small=32 * MIB,
        vmem_big=V7X_VMEM_BYTES - 4 * MIB,
    )


def _sigmoid(x):
    return jax.nn.sigmoid(x)


def _silu(x):
    return x * jax.nn.sigmoid(x)


def _log_sigmoid(x):
    return jnp.minimum(x, 0.0) - jnp.log1p(jnp.exp(-jnp.abs(x)))


def _rms(x, w):
    r = lax.rsqrt(jnp.mean(x * x, axis=-1, keepdims=True) + EPS)
    return x * r * w


def _ada_kernel(c_ref, w_ref, b_ref, o_ref):
    c = c_ref[...]
    sc = _silu(c).astype(BF16)
    o_ref[...] = jnp.dot(sc, w_ref[...].astype(BF16), preferred_element_type=F32) + b_ref[...]


def _ada_call(c_all, w_ada, b_ada):
    t = _tiles()
    depth, d, n3 = w_ada.shape
    rows = c_all.shape[0]
    tn = t["ada_tn"]
    return pl.pallas_call(
        _ada_kernel,
        grid=(depth, n3 // tn),
        in_specs=[
            pl.BlockSpec((rows, d), lambda l, j: (0, 0)),
            pl.BlockSpec((None, d, tn), lambda l, j: (l, 0, j)),
            pl.BlockSpec((None, 1, tn), lambda l, j: (l, 0, j)),
        ],
        out_specs=pl.BlockSpec((None, rows, tn), lambda l, j: (l, 0, j)),
        out_shape=jax.ShapeDtypeStruct((depth, rows, n3), F32),
        compiler_params=pltpu.CompilerParams(
            dimension_semantics=("arbitrary", "arbitrary"), vmem_limit_bytes=t["vmem_small"]),
        name="ada",
    )(c_all, w_ada, b_ada.reshape(depth, 1, n3))


def _mod_spec(d, row_of_tile, kind):
    return pl.BlockSpec((None, 1, d), lambda i, j: (row_of_tile(i) * 3 + kind, 0, 0))


def _modulated(x, nw_ref, scale_ref, shift_ref):
    y = _rms(x, nw_ref[...])
    return (y * (1.0 + scale_ref[...]) + shift_ref[...]).astype(BF16)


def _inproj_kernel(xn_ref, nw_ref, scale0_ref, shift0_ref, scalen_ref, shiftn_ref, w_ref,
                   o_ref, g_ref, hout_ref, ha_ref, hb_ref, *, gate_tile, gate_col, nchunk):
    i = pl.program_id(0)
    j = pl.program_id(1)

    @pl.when((i == 0) & (j == 0))
    def _():
        ha_ref[...] = _modulated(xn_ref[...], nw_ref, scale0_ref, shift0_ref)

    def step(h_cur, h_nxt):
        rows = h_nxt.shape[0] // nchunk
        rs = pl.ds(pl.multiple_of(jnp.clip(j - 1, 0, nchunk - 1) * rows, rows), rows)
        h_nxt[rs, :] = _modulated(xn_ref[rs, :], nw_ref, scalen_ref, shiftn_ref)
        acc = jnp.dot(h_cur[...], w_ref[...], preferred_element_type=F32)
        o_ref[...] = acc.astype(o_ref.dtype)

        @pl.when(j == 0)
        def _():
            hout_ref[...] = h_cur[...]

        @pl.when(j == gate_tile)
        def _():
            g_ref[...] = acc[:, gate_col:gate_col + V7X_LANES]

    @pl.when(i % 2 == 0)
    def _():
        step(ha_ref, hb_ref)

    @pl.when(i % 2 == 1)
    def _():
        step(hb_ref, ha_ref)


def _row_of_tile(tm, seq, first_row):
    assert seq % tm == 0
    return lambda i: first_row + (i * tm) // seq


def _inproj_call(x2d, nw, mod_rows, w_all, seq, first_row):
    t = _tiles()
    m, d = x2d.shape
    n = PACK_COLS
    tm, tn = min(t["in_tm"], seq), t["in_tn"]
    ni, nj = m // tm, n // tn
    nchunk = max(c for c in (1, 2, 4, 8) if c < nj)
    row_of_tile = _row_of_tile(tm, seq, first_row)
    nxt = lambda i: jnp.minimum(i + 1, ni - 1)
    return pl.pallas_call(
        functools.partial(_inproj_kernel, gate_tile=COL_AG // tn, gate_col=COL_AG % tn, nchunk=nchunk),
        grid=(ni, nj),
        in_specs=[
            pl.BlockSpec((tm, d), lambda i, j: (jnp.where((i == 0) & (j == 0), 0, nxt(i)), 0)),
            pl.BlockSpec((1, d), lambda i, j: (0, 0)),
            _mod_spec(d, lambda i: row_of_tile(0), 1),
            _mod_spec(d, lambda i: row_of_tile(0), 0),
            _mod_spec(d, lambda i: row_of_tile(nxt(i)), 1),
            _mod_spec(d, lambda i: row_of_tile(nxt(i)), 0),
            pl.BlockSpec((d, tn), lambda i, j: (0, j)),
        ],
        out_specs=[pl.BlockSpec((tm, tn), lambda i, j: (i, j)),
                   pl.BlockSpec((tm, V7X_LANES), lambda i, j: (i, 0)),
                   pl.BlockSpec((tm, d), lambda i, j: (i, 0))],
        out_shape=[jax.ShapeDtypeStruct((m, n), BF16), jax.ShapeDtypeStruct((m, V7X_LANES), F32),
                   jax.ShapeDtypeStruct((m, d), BF16)],
        scratch_shapes=[pltpu.VMEM((tm, d), BF16), pltpu.VMEM((tm, d), BF16)],
        compiler_params=pltpu.CompilerParams(
            dimension_semantics=("arbitrary", "arbitrary"), vmem_limit_bytes=t["vmem_big"]),
        name="inproj",
    )(x2d, nw, mod_rows, mod_rows, mod_rows, mod_rows, w_all)


def _mlstm_kernel(q_ref, k_ref, v_ref, g_ref, o_ref, z_ref, bg_ref, nw_ref, y_ref,
                  hfw_ref, c_ref, n_ref, m_ref, *, nblk, cb):
    s = pl.program_id(1)
    is_fwd = s < nblk
    blk = jnp.where(is_fwd, s, 2 * nblk - 1 - s)
    sgn = jnp.where(is_fwd, 1, -1)

    @pl.when((s == 0) | (s == nblk))
    def _():
        c_ref[...] = jnp.zeros_like(c_ref)
        n_ref[...] = jnp.zeros_like(n_ref)
        m_ref[...] = jnp.full_like(m_ref, M_INIT)

    ri = lax.broadcasted_iota(jnp.int32, (CHUNK, CHUNK), 0)
    ci = lax.broadcasted_iota(jnp.int32, (CHUNK, CHUNK), 1)
    tri = (((ri - ci) * sgn) >= 0).astype(F32)
    keep_t = ((ci - ri) * sgn) >= 0
    rw = lax.broadcasted_iota(jnp.int32, (BRANCH_W, BRANCH_W), 0)
    cw = lax.broadcasted_iota(jnp.int32, (BRANCH_W, BRANCH_W), 1)
    eye = (rw == cw).astype(BF16)
    bg = bg_ref[...]
    hsl = [slice(h * A_DH, (h + 1) * A_DH) for h in range(A_HEADS)]
    ct_st = [c_ref[:, hsl[h]] for h in range(A_HEADS)]
    n_st = [n_ref[0:1, hsl[h]] for h in range(A_HEADS)]
    m_st = [m_ref[h:h + 1, 0:1] for h in range(A_HEADS)]
    n_pad = jnp.zeros((2 * V7X_SUBLANES - 1, A_DH), BF16)

    chunks, houts = [], []
    for jj in range(cb):
        lc = jnp.where(is_fwd, jj, cb - 1 - jj)
        chunks.append(lc)
        tsl = pl.ds(pl.multiple_of(lc * CHUNK, CHUNK), CHUNK)
        gt = g_ref[tsl, :] + bg
        gt = jnp.where(is_fwd, gt, pltpu.roll(gt, V7X_LANES - A_HEADS, axis=1))
        logf = _log_sigmoid(gt)
        bc = jnp.dot(tri, logf, preferred_element_type=F32, precision=lax.Precision.HIGHEST)
        bc_t = bc.T
        g_row = jnp.where(is_fwd, bc[CHUNK - 1:CHUNK, :], bc[0:1, :])
        q_all = q_ref[tsl, :].astype(BF16)
        k_all = k_ref[tsl, :].astype(BF16)
        vt_all = lax.dot_general(eye, v_ref[tsl, :].astype(BF16), (((1,), (1,)), ((), ())),
                                 preferred_element_type=F32).astype(BF16)
        kq_pair, vt_bd, qcn_pair = [], [], []
        for p in range(A_HEADS // 2):
            ha, hb = 2 * p, 2 * p + 1
            psl = slice(ha * A_DH, (hb + 1) * A_DH)
            first = lax.broadcasted_iota(jnp.int32, (1, 2 * A_DH), 1) < A_DH
            k_pair, q_pair = k_all[:, psl], q_all[:, psl]
            zk = jnp.zeros_like(k_pair)
            k_bd = jnp.concatenate([jnp.where(first, k_pair, zk), jnp.where(first, zk, k_pair)], axis=0)
            kq_pair.append(lax.dot_general(k_bd, q_pair, (((1,), (1,)), ((), ())),
                                           preferred_element_type=F32))
            zblk = jnp.zeros((A_DH, CHUNK), BF16)
            vt_bd.append(jnp.concatenate([jnp.concatenate([vt_all[hsl[ha], :], zblk], axis=1),
                                          jnp.concatenate([zblk, vt_all[hsl[hb], :]], axis=1)], axis=0))
            ctn_a = jnp.concatenate([ct_st[ha].astype(BF16), n_st[ha].astype(BF16), n_pad], axis=0)
            ctn_b = jnp.concatenate([ct_st[hb].astype(BF16), n_st[hb].astype(BF16), n_pad], axis=0)
            zc = jnp.zeros_like(ctn_a)
            ctn_bd = jnp.concatenate([jnp.concatenate([ctn_a, zc], axis=1),
                                      jnp.concatenate([zc, ctn_b], axis=1)], axis=0)
            qcn_pair.append(lax.dot_general(ctn_bd, q_pair, (((1,), (1,)), ((), ())),
                                            preferred_element_type=F32))
        ctn_h = A_DH + 2 * V7X_SUBLANES

        smats, a_ins, m_ts, kas, a_olds, a_news = [], [], [], [], [], []
        for h in range(A_HEADS):
            fcol = 2 * A_HEADS + h
            b_col = bc[:, fcol:fcol + 1]
            i_col = gt[:, h:h + 1]
            g_tot = g_row[:, fcol:fcol + 1]
            rmat = jnp.where(keep_t, i_col - b_col, -jnp.inf)
            big_m = jnp.maximum(jnp.max(rmat, axis=0, keepdims=True), m_st[h])
            kq = kq_pair[h // 2][(h % 2) * CHUNK:(h % 2 + 1) * CHUNK, :]
            smats.append(kq * jnp.exp(rmat - (big_m - LOG_K_SCALE)))
            a_ins.append(jnp.exp(m_st[h] - big_m))
            m_ts.append(bc_t[fcol:fcol + 1, :] + big_m)

            w_end = g_tot - b_col + i_col
            m_loc = jnp.max(w_end, axis=0, keepdims=True)
            kas.append(k_all[:, hsl[h]].astype(F32) * jnp.exp(w_end - (m_loc - LOG_K_SCALE)))
            m_new = jnp.maximum(g_tot + m_st[h], m_loc)
            a_olds.append(jnp.exp(g_tot + m_st[h] - m_new))
            a_news.append(jnp.exp(m_loc - m_new))
            m_st[h] = m_new

        num_pair = [jnp.dot(vt_bd[p], jnp.concatenate(smats[2 * p:2 * p + 2], axis=0).astype(BF16),
                            preferred_element_type=F32) for p in range(A_HEADS // 2)]
        ct_loc_pair = [jnp.dot(vt_bd[p], jnp.concatenate(kas[2 * p:2 * p + 2], axis=0).astype(BF16),
                               preferred_element_type=F32) for p in range(A_HEADS // 2)]
        hrow = []
        for h in range(A_HEADS):
            p, j = h // 2, h % 2
            qcn = qcn_pair[p][j * ctn_h:(j + 1) * ctn_h, :]
            den = jnp.sum(smats[h], axis=0, keepdims=True) + a_ins[h] * qcn[A_DH:A_DH + 1, :]
            hrow.append((num_pair[p][j * A_DH:(j + 1) * A_DH, :] + a_ins[h] * qcn[:A_DH, :])
                        / jnp.maximum(jnp.abs(den), jnp.exp(-m_ts[h])))
            ct_st[h] = a_olds[h] * ct_st[h] + a_news[h] * ct_loc_pair[p][j * A_DH:(j + 1) * A_DH, :]
            n_st[h] = a_olds[h] * n_st[h] + a_news[h] * jnp.sum(kas[h], axis=0, keepdims=True)
        houts.append(hrow)

    for h in range(A_HEADS):
        c_ref[:, hsl[h]] = ct_st[h]
        n_ref[0:1, hsl[h]] = n_st[h]
        m_ref[h:h + 1, :] = jnp.broadcast_to(m_st[h], (1, V7X_LANES))

    @pl.when(is_fwd)
    def _():
        for jj in range(cb):
            for h in range(A_HEADS):
                hfw_ref[blk * cb + chunks[jj], hsl[h], :] = houts[jj][h]

    @pl.when(jnp.logical_not(is_fwd))
    def _():
        for jj in range(cb):
            tsl = pl.ds(pl.multiple_of(chunks[jj] * CHUNK, CHUNK), CHUNK)
            for h in range(A_HEADS):
                tot = hfw_ref[blk * cb + chunks[jj], hsl[h], :] + houts[jj][h]
                tot = tot * lax.rsqrt(jnp.mean(tot * tot, axis=0, keepdims=True) + EPS)
                yh = tot.T * nw_ref[:, hsl[h]] * _sigmoid(o_ref[tsl, hsl[h]].astype(F32))
                y_ref[tsl, hsl[h]] = (yh * _silu(z_ref[tsl, hsl[h]].astype(F32))).astype(BF16)


def _mlstm_call(p3, g3, bg_pad, nw):
    t = _tiles()
    bsz, seq, _ = p3.shape
    nc = seq // CHUNK
    cb = min(MLSTM_BLOCK_CHUNKS, nc)
    assert nc % cb == 0
    nblk = nc // cb
    rows = cb * CHUNK

    def cur(s):
        return jnp.where(s < nblk, s, 2 * nblk - 1 - s)

    def late(s):
        return jnp.where(s < nblk, nblk - 1, 2 * nblk - 1 - s)

    def blk(col, which):
        return pl.BlockSpec((None, rows, BRANCH_W), lambda b, s: (b, which(s), col // BRANCH_W))

    return pl.pallas_call(
        functools.partial(_mlstm_kernel, nblk=nblk, cb=cb),
        grid=(bsz, 2 * nblk),
        in_specs=[
            blk(COL_AQ, cur), blk(COL_AK, cur), blk(COL_AV, cur),
            pl.BlockSpec((None, rows, V7X_LANES), lambda b, s: (b, cur(s), 0)),
            blk(COL_AO, late), blk(COL_AZ, late),
            pl.BlockSpec((1, V7X_LANES), lambda b, s: (0, 0)),
            pl.BlockSpec((1, BRANCH_W), lambda b, s: (0, 0)),
        ],
        out_specs=pl.BlockSpec((None, rows, BRANCH_W), lambda b, s: (b, late(s), 0)),
        out_shape=jax.ShapeDtypeStruct((bsz, seq, BRANCH_W), BF16),
        scratch_shapes=[
            pltpu.VMEM((nc, BRANCH_W, CHUNK), F32),
            pltpu.VMEM((A_DH, BRANCH_W), F32),
            pltpu.VMEM((V7X_SUBLANES, BRANCH_W), F32),
            pltpu.VMEM((V7X_SUBLANES, V7X_LANES), F32),
        ],
        compiler_params=pltpu.CompilerParams(
            dimension_semantics=("arbitrary", "arbitrary"), vmem_limit_bytes=t["vmem_big"]),
        name="mlstm",
    )(p3, p3, p3, g3, p3, p3, bg_pad, nw)


def _nattn_kernel(q_ref, kc_ref, kx_ref, vc_ref, vx_ref, z_ref, qw_ref, kw_ref, bias_ref, y_ref,
                  kn_ref, vn_ref, *, nblk):
    i = pl.program_id(1)

    def put(slot, k_src, v_src):
        dst = pl.ds(pl.multiple_of(slot * NA_BLOCK_TOK, NA_BLOCK_TOK), NA_BLOCK_TOK)
        for h in range(B_HEADS):
            hs = slice(h * B_DH, (h + 1) * B_DH)
            kn_ref[dst, hs] = _rms(k_src[:, hs].astype(F32), kw_ref[...]).astype(BF16)
        vn_ref[dst, :] = v_src[...].astype(BF16)

    @pl.when(i == 0)
    def _():
        put(0, kc_ref, vc_ref)

    nslot = (i + 1) % 3
    put(nslot, kx_ref, vx_ref)

    @pl.when((i == 0) | (nslot == 0))
    def _():
        mirror = slice(3 * NA_BLOCK_TOK, 4 * NA_BLOCK_TOK)
        kn_ref[mirror, :] = kn_ref[0:NA_BLOCK_TOK, :]
        vn_ref[mirror, :] = vn_ref[0:NA_BLOCK_TOK, :]

    qn = []
    for h in range(B_HEADS):
        hs = slice(h * B_DH, (h + 1) * B_DH)
        qn.append((_rms(q_ref[:, hs].astype(F32), qw_ref[...]) * (B_DH ** -0.5 * LOG2E)).astype(BF16))

    half = NA_BLOCK_TOK // 2
    eye = (lax.broadcasted_iota(jnp.int32, (B_DH, B_DH), 0)
           == lax.broadcasted_iota(jnp.int32, (B_DH, B_DH), 1)).astype(BF16)
    starts = (jnp.where(i == 0, 0, ((i + 2) % 3) * NA_BLOCK_TOK + half), (i % 3) * NA_BLOCK_TOK)
    variants = (jnp.where(i == 0, 1, 0), jnp.where(i == nblk - 1, 2, 0))
    for quad in range(NA_BLOCK_ROWS // NA_QUAD_ROWS):
        qrows = slice(quad * NA_QUAD_TOK, (quad + 1) * NA_QUAD_TOK)
        win = pl.ds(pl.multiple_of(starts[quad], half), NA_WIN_TOK)
        for h in range(B_HEADS):
            hs = slice(h * B_DH, (h + 1) * B_DH)
            sc = lax.dot_general(kn_ref[win, hs], qn[h][qrows, :], (((1,), (1,)), ((), ())),
                                 preferred_element_type=F32)
            sc = sc + bias_ref[variants[quad], h]
            m = jnp.max(sc, axis=0, keepdims=True)
            p = jnp.exp2(sc - m)
            l = jnp.sum(p, axis=0, keepdims=True)
            vt = lax.dot_general(eye, vn_ref[win, hs], (((1,), (1,)), ((), ())),
                                 preferred_element_type=F32).astype(BF16)
            o = (jnp.dot(vt, p.astype(BF16), preferred_element_type=F32) / l).T
            y_ref[qrows, hs] = (o * _silu(z_ref[qrows, hs].astype(F32))).astype(BF16)


def _nattn_call(p3, qw, kw, bias_quads):
    t = _tiles()
    bsz, seq, _ = p3.shape
    rows = seq // GRID_W
    assert rows >= WIN_ROWS and rows % NA_BLOCK_ROWS == 0
    nblk = rows // NA_BLOCK_ROWS

    def blk(col, shift):
        cb = col // BRANCH_W
        return pl.BlockSpec((None, NA_BLOCK_TOK, BRANCH_W),
                            lambda b, i: (b, jnp.clip(i + shift, 0, nblk - 1), cb))

    return pl.pallas_call(
        functools.partial(_nattn_kernel, nblk=nblk),
        grid=(bsz, nblk),
        in_specs=[
            blk(COL_BQ, 0),
            blk(COL_BK, 0), blk(COL_BK, 1),
            blk(COL_BV, 0), blk(COL_BV, 1),
            blk(COL_BZ, 0),
            pl.BlockSpec((1, B_DH), lambda b, i: (0, 0)),
            pl.BlockSpec((1, B_DH), lambda b, i: (0, 0)),
            pl.BlockSpec(bias_quads.shape, lambda b, i: (0, 0, 0, 0)),
        ],
        out_specs=pl.BlockSpec((None, NA_BLOCK_TOK, BRANCH_W), lambda b, i: (b, i, 0)),
        out_shape=jax.ShapeDtypeStruct((bsz, seq, BRANCH_W), BF16),
        scratch_shapes=[
            pltpu.VMEM((NA_RING_SLOTS * NA_BLOCK_TOK, BRANCH_W), BF16),
            pltpu.VMEM((NA_RING_SLOTS * NA_BLOCK_TOK, BRANCH_W), BF16),
        ],
        compiler_params=pltpu.CompilerParams(
            dimension_semantics=("arbitrary", "arbitrary"), vmem_limit_bytes=t["vmem_big"]),
        name="nattn",
    )(p3, p3, p3, p3, p3, p3, qw, kw, bias_quads)


def _skew(w, nq, first, ncol):
    width = w.shape[-1]
    assert first - (nq - 1) >= 0 and first + ncol <= width - 1
    lead = w.shape[:-1]
    flat = jnp.tile(w, (1,) * len(lead) + (nq,))[..., :nq * (width - 1)]
    return flat.reshape(lead + (nq, width - 1))[..., first:first + ncol]


def _nattn_bias(rpb):
    heads = rpb.shape[0]
    reach = GRID_W - WIN_COLS
    w = jnp.pad(rpb.astype(F32), ((0, 0), (0, 0), (reach, reach + 1)))
    tbl = _skew(w, GRID_W, GRID_W - 1, GRID_W)
    qc = np.arange(GRID_W)[:, None]
    kc = np.arange(GRID_W)[None, :]
    qs = np.clip(qc - WIN_COLS // 2, 0, GRID_W - WIN_COLS)
    ok = (kc >= qs) & (kc < qs + WIN_COLS)
    tbl = jnp.where(ok[None, None], tbl, -jnp.inf)
    per_d = [jnp.concatenate([tbl[:, d + j] for j in range(WIN_ROWS)], axis=-1) for d in range(WIN_ROWS)]
    fill = lambda n: jnp.full((heads, GRID_W, n), -jnp.inf, F32)

    def quad(ds, shifts):
        rows = [jnp.concatenate([fill(GRID_W * sh), per_d[d], fill(NA_WIN_TOK - NA_BLOCK_TOK - GRID_W * sh)],
                                axis=-1) for d, sh in zip(ds, shifts)]
        return jnp.concatenate(rows, axis=1)

    mid = WIN_ROWS - 1 - WIN_ROWS // 2
    interior = quad((mid,) * NA_QUAD_ROWS, range(NA_QUAD_ROWS))
    top = quad(range(WIN_ROWS - 1, WIN_ROWS - 1 - NA_QUAD_ROWS, -1), (0,) * NA_QUAD_ROWS)
    bottom = quad(range(NA_QUAD_ROWS - 1, -1, -1), (0,) * NA_QUAD_ROWS)
    return jnp.swapaxes(jnp.stack([interior, top, bottom]), 2, 3) * LOG2E


def _swa_kernel(q_ref, kp_ref, kc_ref, kx_ref, vp_ref, vc_ref, vx_ref, z_ref, qw_ref, kw_ref,
                sink_ref, bias_ref, y_ref, *, nsteps, nq):
    n = pl.program_id(1)
    band = 3 * C_BLOCK
    lane = lax.broadcasted_iota(jnp.int32, (1, V7X_LANES), 1)
    lo = lane < C_DH

    def pair_rms(x, w):
        x2 = x * x
        s_lo = jnp.sum(jnp.where(lo, x2, 0.0), axis=-1, keepdims=True)
        s_hi = jnp.sum(jnp.where(lo, 0.0, x2), axis=-1, keepdims=True)
        r = lax.rsqrt(jnp.where(lo, s_lo, s_hi) * (1.0 / C_DH) + EPS)
        return x * r * w

    kn = pair_rms(jnp.concatenate([kp_ref[...], kc_ref[...], kx_ref[...]], axis=0).astype(F32), kw_ref[...])
    vc = jnp.concatenate([vp_ref[...], vc_ref[...], vx_ref[...]], axis=0).astype(F32)
    kn_sw = pltpu.roll(kn, C_DH, axis=1)
    vc_sw = pltpu.roll(vc, C_DH, axis=1)
    stacks = []
    for g in range(C_KV_HEADS):
        k_lo, k_hi = (kn, kn_sw) if g == 0 else (kn_sw, kn)
        v_lo, v_hi = (vc, vc_sw) if g == 0 else (vc_sw, vc)
        stacks.append((jnp.where(lo, k_lo, 0.0).astype(BF16), jnp.where(lo, 0.0, k_hi).astype(BF16),
                       jnp.where(lo, v_lo, 0.0).astype(BF16), jnp.where(lo, 0.0, v_hi).astype(BF16)))

    ki = lax.broadcasted_iota(jnp.int32, (2 * band, 1), 0)
    ki = jnp.where(ki >= band, ki - band, ki)
    first_slab = lax.broadcasted_iota(jnp.int32, (1, 2 * C_BLOCK), 1) < C_BLOCK
    even_rows = lax.broadcasted_iota(jnp.int32, (V7X_LANES, 1), 0) < C_DH
    eye = (lax.broadcasted_iota(jnp.int32, (V7X_LANES, V7X_LANES), 0)
           == lax.broadcasted_iota(jnp.int32, (V7X_LANES, V7X_LANES), 1)).astype(BF16)
    for qb in range(nq):
        qrows = slice(qb * C_BLOCK, (qb + 1) * C_BLOCK)
        krows = slice((nq + qb - 1) * C_BLOCK, (nq + qb + 2) * C_BLOCK)
        has_prev = (n > 0) if qb == 0 else True
        has_next = (n < nsteps - 1) if qb == nq - 1 else True
        valid = ((ki >= C_BLOCK) | has_prev) & ((ki < 2 * C_BLOCK) | has_next)
        for g in range(C_KV_HEADS):
            k_even, k_odd, v_even, v_odd = stacks[g]
            kk = jnp.concatenate([k_even[krows], k_odd[krows]], axis=0)
            vv = jnp.concatenate([v_even[krows], v_odd[krows]], axis=0)
            slabs = [slice(p * V7X_LANES, (p + 1) * V7X_LANES) for p in (2 * g, 2 * g + 1)]
            qn = jnp.concatenate(
                [(pair_rms(q_ref[qrows, sl].astype(F32), qw_ref[...]) * (C_DH ** -0.5 * LOG2E)).astype(BF16)
                 for sl in slabs], axis=0)
            sc = lax.dot_general(kk, qn, (((1,), (1,)), ((), ())), preferred_element_type=F32)
            sc = jnp.where(valid, sc + bias_ref[g], -jnp.inf)
            probs, denoms = [], []
            for e in range(2):
                se = sc[e * band:(e + 1) * band, :]
                h0 = C_GROUP * g + e
                sink = jnp.where(first_slab, sink_ref[h0:h0 + 1, 0:1], sink_ref[h0 + 2:h0 + 3, 0:1])
                m = jnp.maximum(jnp.max(se, axis=0, keepdims=True), sink)
                pe = jnp.exp2(se - m)
                probs.append(pe.astype(BF16))
                denoms.append(jnp.sum(pe, axis=0, keepdims=True) + jnp.exp2(sink - m))
            vvt = lax.dot_general(eye, vv, (((1,), (1,)), ((), ())), preferred_element_type=F32).astype(BF16)
            ot = jnp.dot(vvt, jnp.concatenate(probs, axis=0), preferred_element_type=F32)
            o = (ot / jnp.where(even_rows, denoms[0], denoms[1])).T
            for j, sl in enumerate(slabs):
                rows = slice(j * C_BLOCK, (j + 1) * C_BLOCK)
                y_ref[qrows, sl] = (o[rows] * _silu(z_ref[qrows, sl].astype(F32))).astype(BF16)


def _swa_call(p3, qw, kw, sink_rows, bias):
    t = _tiles()
    bsz, seq, _ = p3.shape
    nq = SWA_STEP_BLOCKS
    rows = nq * C_BLOCK
    assert seq % rows == 0
    nsteps = seq // rows

    def wide(col):
        return pl.BlockSpec((None, rows, BRANCH_W), lambda b, n: (b, n, col // BRANCH_W))

    def kv(col, shift):
        return pl.BlockSpec((None, rows, C_KV_W),
                            lambda b, n: (b, jnp.clip(n + shift, 0, nsteps - 1), col // C_KV_W))

    return pl.pallas_call(
        functools.partial(_swa_kernel, nsteps=nsteps, nq=nq),
        grid=(bsz, nsteps),
        in_specs=[
            wide(COL_CQ),
            kv(COL_CK, -1), kv(COL_CK, 0), kv(COL_CK, 1),
            kv(COL_CV, -1), kv(COL_CV, 0), kv(COL_CV, 1),
            wide(COL_CZ),
            pl.BlockSpec((1, V7X_LANES), lambda b, n: (0, 0)),
            pl.BlockSpec((1, V7X_LANES), lambda b, n: (0, 0)),
            pl.BlockSpec(sink_rows.shape, lambda b, n: (0, 0)),
            pl.BlockSpec(bias.shape, lambda b, n: (0, 0, 0)),
        ],
        out_specs=pl.BlockSpec((None, rows, BRANCH_W), lambda b, n: (b, n, 0)),
        out_shape=jax.ShapeDtypeStruct((bsz, seq, BRANCH_W), BF16),
        compiler_params=pltpu.CompilerParams(
            dimension_semantics=("arbitrary", "arbitrary"), vmem_limit_bytes=t["vmem_small"]),
        name="swa",
    )(p3, p3, p3, p3, p3, p3, p3, p3, qw, kw, sink_rows, bias)


def _t5_bucket_rel():
    rel = np.arange(-(2 * C_BLOCK - 1), 2 * C_BLOCK)
    half = N_BUCKETS // 2
    max_exact = half // 2
    n = np.abs(rel)
    nf = np.maximum(n, 1).astype(np.float32)
    scale = np.float32(math.log(MAX_DIST / max_exact))
    large = max_exact + (np.log(nf / np.float32(max_exact)) / scale
                         * np.float32(half - max_exact)).astype(np.int32)
    large = np.minimum(large, half - 1)
    bucket = np.where(rel > 0, half, 0) + np.where(n < max_exact, n, large)
    return bucket, n <= C_WINDOW


def _swa_bias(rel_bias):
    bucket, in_window = _t5_bucket_rel()
    per_rel = jnp.where(in_window[:, None], rel_bias.astype(F32)[bucket], -jnp.inf)
    w = jnp.pad(per_rel.T, ((0, 0), (0, 1)))
    tbl = _skew(w, C_BLOCK, C_BLOCK - 1, 3 * C_BLOCK)
    slabs = jnp.concatenate([tbl[0::2], tbl[1::2]], axis=-1)
    return jnp.swapaxes(slabs.reshape(C_KV_HEADS, 2 * C_BLOCK, 2 * 3 * C_BLOCK), 1, 2) * LOG2E


def _conv_kernel(a_ref, g_ref, ap_ref, gp_ref, ax_ref, gx_ref, z_ref, cw_ref, cb_ref, lw_ref, lb_ref,
                 y_ref, ext_ref, sh_ref, *, ntile):
    i = pl.program_id(1)
    tt = a_ref.shape[0]
    glu = lambda a, g: a[...].astype(F32) * _sigmoid(g[...].astype(F32))
    prev = jnp.where(i > 0, glu(ap_ref, gp_ref), 0.0)
    nxt = jnp.where(i < ntile - 1, glu(ax_ref, gx_ref), 0.0)
    ext_ref[0:CONV_HALO, :] = prev
    ext_ref[CONV_HALO:CONV_HALO + tt, :] = glu(a_ref, g_ref)
    ext_ref[CONV_HALO + tt:2 * CONV_HALO + tt, :] = nxt
    span = sh_ref.shape[1]
    for r in range(1, V7X_SUBLANES):
        sh_ref[r - 1] = ext_ref[pl.ds(r, span), :]
    acc = None
    for w in range(CONV_W):
        a, r = divmod(CONV_HALO - CONV_W // 2 + w, V7X_SUBLANES)
        lo = a * V7X_SUBLANES
        src = ext_ref[lo:lo + tt, :] if r == 0 else sh_ref[r - 1, lo:lo + tt, :]
        term = src * cw_ref[w:w + 1, :]
        acc = term if acc is None else acc + term
    u = acc + cb_ref[...]
    mu = jnp.mean(u, axis=-1, keepdims=True)
    var = jnp.mean(jnp.square(u - mu), axis=-1, keepdims=True)
    u = (u - mu) * lax.rsqrt(var + EPS) * lw_ref[...] + lb_ref[...]
    y_ref[...] = (_silu(u) * _silu(z_ref[...].astype(F32))).astype(BF16)


def _conv_call(p3, cw_pad, cb, lw, lb):
    t = _tiles()
    bsz, seq, _ = p3.shape
    tt = CONV_TILE
    ntile = seq // tt
    per = tt // CONV_HALO
    nhalo = seq // CONV_HALO
    span = tt + (2 * CONV_HALO - 1) // V7X_SUBLANES * V7X_SUBLANES

    def cur(col):
        return pl.BlockSpec((None, tt, BRANCH_W), lambda b, i: (b, i, col // BRANCH_W))

    def halo(col, after):
        def idx(b, i):
            r = (i + 1) * per if after else i * per - 1
            return (b, jnp.clip(r, 0, nhalo - 1), col // BRANCH_W)
        return pl.BlockSpec((None, CONV_HALO, BRANCH_W), idx)

    vec = pl.BlockSpec((1, BRANCH_W), lambda b, i: (0, 0))
    return pl.pallas_call(
        functools.partial(_conv_kernel, ntile=ntile),
        grid=(bsz, ntile),
        in_specs=[
            cur(COL_DA), cur(COL_DG),
            halo(COL_DA, False), halo(COL_DG, False), halo(COL_DA, True), halo(COL_DG, True),
            cur(COL_DZ),
            pl.BlockSpec(cw_pad.shape, lambda b, i: (0, 0)),
            vec, vec, vec,
        ],
        out_specs=pl.BlockSpec((None, tt, BRANCH_W), lambda b, i: (b, i, 0)),
        out_shape=jax.ShapeDtypeStruct((bsz, seq, BRANCH_W), BF16),
        scratch_shapes=[
            pltpu.VMEM((tt + 2 * CONV_HALO, BRANCH_W), F32),
            pltpu.VMEM((V7X_SUBLANES - 1, span, BRANCH_W), F32),
        ],
        compiler_params=pltpu.CompilerParams(
            dimension_semantics=("arbitrary", "arbitrary"), vmem_limit_bytes=t["vmem_small"]),
        name="conv",
    )(p3, p3, p3, p3, p3, p3, p3, cw_pad, cb, lw, lb)


def _merge_kernel(h_ref, ya_ref, yb_ref, yc_ref, yd_ref, wma_ref, wmb_ref, wmc_ref, wmd_ref, wb_ref, o_ref):
    h = h_ref[...]
    merged = None
    branches = ((ya_ref, wma_ref), (yb_ref, wmb_ref), (yc_ref, wmc_ref), (yd_ref, wmd_ref))
    for b, (y_ref, wm_ref) in enumerate(branches):
        gate = _sigmoid(jnp.dot(h, wm_ref[...], preferred_element_type=F32))
        term = gate * jnp.dot(y_ref[...], wb_ref[b], preferred_element_type=F32)
        merged = term if merged is None else merged + term
    o_ref[...] = merged.astype(BF16)


def _merge_call(h2d, ys, w_all, w_branch, seq):
    t = _tiles()
    m, d = h2d.shape
    tm, tn = min(t["mg_tm"], seq), t["mg_tn"]
    yspec = pl.BlockSpec((tm, BRANCH_W), lambda i, j: (i, 0))
    return pl.pallas_call(
        _merge_kernel,
        grid=(m // tm, d // tn),
        in_specs=[
            pl.BlockSpec((tm, d), lambda i, j: (i, 0)),
            yspec, yspec, yspec, yspec,
            *[pl.BlockSpec((d, tn), functools.partial(lambda i, j, b: (0, (PACK_COLS + b * d) // tn + j), b=b))
              for b in range(N_BRANCH)],
            pl.BlockSpec((N_BRANCH, BRANCH_W, tn), lambda i, j: (0, 0, j)),
        ],
        out_specs=pl.BlockSpec((tm, tn), lambda i, j: (i, j)),
        out_shape=jax.ShapeDtypeStruct((m, d), BF16),
        compiler_params=pltpu.CompilerParams(
            dimension_semantics=("arbitrary", "arbitrary"), vmem_limit_bytes=t["vmem_big"]),
        name="merge",
    )(h2d, *ys, *([w_all] * N_BRANCH), w_branch)


def _outproj_kernel(x_ref, mg_ref, gate_ref, wo_ref, o_ref):
    o_ref[...] = x_ref[...] + gate_ref[...] * jnp.dot(mg_ref[...], wo_ref[...], preferred_element_type=F32)


def _outproj_call(x2d, merged, mod_rows, w_out, seq, first_row):
    t = _tiles()
    m, d = x2d.shape
    tm, tn = min(t["op_tm"], seq), t["op_tn"]
    row_of_tile = _row_of_tile(tm, seq, first_row)
    return pl.pallas_call(
        _outproj_kernel,
        grid=(m // tm, d // tn),
        in_specs=[
            pl.BlockSpec((tm, tn), lambda i, j: (i, j)),
            pl.BlockSpec((tm, d), lambda i, j: (i, 0)),
            pl.BlockSpec((None, 1, tn), lambda i, j: (row_of_tile(i) * 3 + 2, 0, j)),
            pl.BlockSpec((d, tn), lambda i, j: (0, j)),
        ],
        out_specs=pl.BlockSpec((tm, tn), lambda i, j: (i, j)),
        out_shape=jax.ShapeDtypeStruct((m, d), F32),
        compiler_params=pltpu.CompilerParams(
            dimension_semantics=("arbitrary", "arbitrary"), vmem_limit_bytes=t["vmem_big"]),
        name="outproj",
    )(x2d, merged, mod_rows, w_out)


REPACK_TILE = 256
GATE_SHIFT = A_GATES


def _repack_plan(d):
    sizes = (512, 512, 512, 512, 512, A_GATES, 512, 512, 512, 512, 512, C_KV_W, C_KV_W, 512, 1024, 512,
             N_BRANCH * d)
    names = ("aq", "ak", "av", "ao", "az", "ag", "bq", "bk", "bv", "bz", "cq", "ck", "cv", "cz", "dglu", "dz", "mg")
    src = dict(zip(names, np.concatenate([[0], np.cumsum(sizes)])[:-1]))
    groups = [
        (COL_AQ, src["aq"], 5 * BRANCH_W), (COL_BQ, src["bq"], 4 * BRANCH_W), (COL_DA, src["dglu"], 3 * BRANCH_W),
        (COL_CQ, src["cq"], BRANCH_W), (COL_CZ, src["cz"], BRANCH_W), (COL_CK, src["ck"], 2 * C_KV_W),
        (COL_AG, src["ag"], REPACK_TILE), (PACK_COLS, src["mg"], N_BRANCH * d)]
    ntile = (PACK_COLS + N_BRANCH * d) // REPACK_TILE
    a_idx, b_idx, mode = np.zeros(ntile, np.int32), np.zeros(ntile, np.int32), np.zeros(ntile, np.int32)
    for dst, s0, width in groups:
        for k in range(width // REPACK_TILE):
            t = dst // REPACK_TILE + k
            start = int(s0) + k * REPACK_TILE
            off = start % REPACK_TILE
            assert off in (0, GATE_SHIFT) and dst % REPACK_TILE == 0
            a_idx[t] = start // REPACK_TILE
            b_idx[t] = (start - off + REPACK_TILE) // V7X_LANES
            mode[t] = 2 if dst == COL_AG else (1 if off else 0)
    return jnp.asarray(a_idx), jnp.asarray(b_idx), jnp.asarray(mode)


def _repack_kernel(a_idx, b_idx, mode, a_ref, b_ref, o_ref):
    t = pl.program_id(0)
    a = a_ref[...]
    lane = lax.broadcasted_iota(jnp.int32, (1, REPACK_TILE), 1)
    shifted = jnp.concatenate([a[:, GATE_SHIFT:], b_ref[:, :GATE_SHIFT]], axis=1)
    gates = jnp.where(lane < A_GATES, a, 0.0)
    m = mode[t]
    o_ref[...] = jnp.where(m == 1, shifted, jnp.where(m == 2, gates, a)).astype(BF16)


def _repack_call(w_in, layer):
    t = _tiles()
    _, d, _ = w_in.shape
    a_idx, b_idx, mode = _repack_plan(d)
    ntile = a_idx.shape[0]
    return pl.pallas_call(
        _repack_kernel,
        grid_spec=pltpu.PrefetchScalarGridSpec(
            num_scalar_prefetch=3,
            grid=(ntile,),
            in_specs=[
                pl.BlockSpec((None, d, REPACK_TILE), lambda i, a, b, m: (layer, 0, a[i])),
                pl.BlockSpec((None, d, V7X_LANES), lambda i, a, b, m: (layer, 0, b[i])),
            ],
            out_specs=pl.BlockSpec((d, REPACK_TILE), lambda i, a, b, m: (0, i)),
        ),
        out_shape=jax.ShapeDtypeStruct((d, ntile * REPACK_TILE), BF16),
        compiler_params=pltpu.CompilerParams(
            dimension_semantics=("arbitrary",), vmem_limit_bytes=t["vmem_small"]),
        name="repack",
    )(a_idx, b_idx, mode, w_in, w_in)


def kernel(x_prompt, x_sample, c_prompt, c_sample, rel_bias, norm_w, w_ada, b_ada, w_in, b_gate, mlstm_norm_w, na_q_norm, na_k_norm, na_rpb, swa_q_norm, swa_k_norm, swa_sink, conv_w, conv_b, conv_ln_w, conv_ln_b, w_branch, w_out):
    depth, d = norm_w.shape
    groups = ((x_prompt, 0), (x_sample, c_prompt.shape[0]))
    n_cond = c_prompt.shape[0] + c_sample.shape[0]
    cond_rows = -(-n_cond // V7X_SUBLANES) * V7X_SUBLANES
    c_all = jnp.concatenate([c_prompt, c_sample, jnp.zeros((cond_rows - n_cond, d), F32)], axis=0)
    mod = _ada_call(c_all, w_ada, b_ada)
    mod_rows = mod.reshape(depth * cond_rows * 3, 1, d)

    swa_bias = _swa_bias(rel_bias)
    layers = []
    for l in range(depth):
        layers.append(dict(
            w_all=_repack_call(w_in, l),
            w_branch=w_branch[l].astype(BF16), w_out=w_out[l].astype(BF16),
            nw=norm_w[l].reshape(1, d),
            bg=jnp.pad(b_gate[l], (0, V7X_LANES - A_GATES)).reshape(1, V7X_LANES),
            mnw=mlstm_norm_w[l].reshape(1, BRANCH_W),
            na_qw=na_q_norm[l].reshape(1, B_DH), na_kw=na_k_norm[l].reshape(1, B_DH),
            na_bias=_nattn_bias(na_rpb[l]),
            swa_qw=jnp.tile(swa_q_norm[l], 2).reshape(1, V7X_LANES),
            swa_kw=jnp.tile(swa_k_norm[l], 2).reshape(1, V7X_LANES),
            sink=jnp.broadcast_to(swa_sink[l].reshape(C_HEADS, 1) * LOG2E, (C_HEADS, V7X_LANES)),
            cw=jnp.pad(conv_w[l], ((0, 1), (0, 0))),
            cb=conv_b[l].reshape(1, BRANCH_W), lw=conv_ln_w[l].reshape(1, BRANCH_W),
            lb=conv_ln_b[l].reshape(1, BRANCH_W),
        ))

    outs = []
    for x, cond_off in groups:
        bsz, seq, _ = x.shape
        x2d = x.reshape(bsz * seq, d)
        for l, lw in enumerate(layers):
            first_row = l * cond_rows + cond_off
            p, gates, h2d = _inproj_call(x2d, lw["nw"], mod_rows, lw["w_all"], seq, first_row)
            p3 = p.reshape(bsz, seq, PACK_COLS)
            ya = _mlstm_call(p3, gates.reshape(bsz, seq, V7X_LANES), lw["bg"], lw["mnw"])
            yb = _nattn_call(p3, lw["na_qw"], lw["na_kw"], lw["na_bias"])
            yc = _swa_call(p3, lw["swa_qw"], lw["swa_kw"], lw["sink"], swa_bias)
            yd = _conv_call(p3, lw["cw"], lw["cb"], lw["lw"], lw["lb"])
            ys = [y.reshape(bsz * seq, BRANCH_W) for y in (ya, yb, yc, yd)]
            merged = _merge_call(h2d, ys, lw["w_all"], lw["w_branch"], seq)
            x2d = _outproj_call(x2d, merged, mod_rows, lw["w_out"], seq, first_row)
        outs.append(x2d.reshape(bsz, seq, d))
    return tuple(outs)
```

```python
import functools
import math

import numpy as np
import jax
import jax.numpy as jnp
from jax import lax
from jax.experimental import pallas as pl
from jax.experimental.pallas import tpu as pltpu

F32 = jnp.float32
BF16 = jnp.bfloat16

V7X_VMEM_BYTES = 64 * 1024 * 1024
V7X_LANES = 128
V7X_SUBLANES = 8
MIB = 1024 * 1024

EPS = 1e-6
N_BRANCH = 4
BRANCH_W = 512
GRID_W = 64
A_HEADS = 4
A_DH = BRANCH_W // A_HEADS
A_GATES = 4 * A_HEADS
CHUNK = 128
MLSTM_BLOCK_CHUNKS = 16
M_INIT = -1e30
LOG_K_SCALE = -0.5 * math.log(A_DH)
LOG2E = math.log2(math.e)
B_HEADS = 4
B_DH = BRANCH_W // B_HEADS
WIN_ROWS = 8
WIN_COLS = 16
NA_BLOCK_ROWS = 8
NA_BLOCK_TOK = NA_BLOCK_ROWS * GRID_W
NA_QUAD_ROWS = 4
NA_QUAD_TOK = NA_QUAD_ROWS * GRID_W
NA_WIN_TOK = (WIN_ROWS + NA_QUAD_ROWS) * GRID_W
NA_RING_SLOTS = 4
C_HEADS = 8
C_KV_HEADS = 2
C_GROUP = C_HEADS // C_KV_HEADS
C_DH = BRANCH_W // C_HEADS
C_KV_W = C_KV_HEADS * C_DH
C_WINDOW = 128
C_BLOCK = 128
SWA_STEP_BLOCKS = 1
N_BUCKETS = 32
MAX_DIST = 128
CONV_W = 31
CONV_HALO = 16
CONV_TILE = 512

COL_AQ, COL_AK, COL_AV, COL_AO, COL_AZ = 0, 512, 1024, 1536, 2048
COL_BQ, COL_BK, COL_BV, COL_BZ = 2560, 3072, 3584, 4096
COL_DA, COL_DG, COL_DZ = 4608, 5120, 5632
COL_CQ, COL_CZ = 6144, 6656
COL_CK, COL_CV, COL_AG = 7168, 7296, 7424
PACK_COLS = 7680


def _tiles():
    return dict(
        ada_tn=1536,
        in_tm=1024, in_tn=1536,
        mg_tm=1024, mg_tn=512,
        op_tm=2048, op_tn=512,
        vmem_small=32 * MIB,
        vmem_big=V7X_VMEM_BYTES - 4 * MIB,
    )


def _sigmoid(x):
    return jax.nn.sigmoid(x)


def _silu(x):
    return x * jax.nn.sigmoid(x)


def _log_sigmoid(x):
    return jnp.minimum(x, 0.0) - jnp.log1p(jnp.exp(-jnp.abs(x)))


def _rms(x, w):
    r = lax.rsqrt(jnp.mean(x * x, axis=-1, keepdims=True) + EPS)
    return x * r * w


def _ada_kernel(c_ref, w_ref, b_ref, o_ref):
    c = c_ref[...]
    sc = _silu(c).astype(BF16)
    o_ref[...] = jnp.dot(sc, w_ref[...].astype(BF16), preferred_element_type=F32) + b_ref[...]


def _ada_call(c_all, w_ada, b_ada):
    t = _tiles()
    depth, d, n3 = w_ada.shape
    rows = c_all.shape[0]
    tn = t["ada_tn"]
    return pl.pallas_call(
        _ada_kernel,
        grid=(depth, n3 // tn),
        in_specs=[
            pl.BlockSpec((rows, d), lambda l, j: (0, 0)),
            pl.BlockSpec((None, d, tn), lambda l, j: (l, 0, j)),
            pl.BlockSpec((None, 1, tn), lambda l, j: (l, 0, j)),
        ],
        out_specs=pl.BlockSpec((None, rows, tn), lambda l, j: (l, 0, j)),
        out_shape=jax.ShapeDtypeStruct((depth, rows, n3), F32),
        compiler_params=pltpu.CompilerParams(
            dimension_semantics=("arbitrary", "arbitrary"), vmem_limit_bytes=t["vmem_small"]),
        name="ada",
    )(c_all, w_ada, b_ada.reshape(depth, 1, n3))


def _mod_spec(d, row_of_tile, kind):
    return pl.BlockSpec((None, 1, d), lambda i, j: (row_of_tile(i) * 3 + kind, 0, 0))


def _modulated(x, nw_ref, scale_ref, shift_ref):
    y = _rms(x, nw_ref[...])
    return (y * (1.0 + scale_ref[...]) + shift_ref[...]).astype(BF16)


def _inproj_kernel(xn_ref, nw_ref, scale0_ref, shift0_ref, scalen_ref, shiftn_ref, w_ref,
                   o_ref, g_ref, hout_ref, ha_ref, hb_ref, *, gate_tile, gate_col, nchunk):
    i = pl.program_id(0)
    j = pl.program_id(1)

    @pl.when((i == 0) & (j == 0))
    def _():
        ha_ref[...] = _modulated(xn_ref[...], nw_ref, scale0_ref, shift0_ref)

    def step(h_cur, h_nxt):
        rows = h_nxt.shape[0] // nchunk
        rs = pl.ds(pl.multiple_of(jnp.clip(j - 1, 0, nchunk - 1) * rows, rows), rows)
        h_nxt[rs, :] = _modulated(xn_ref[rs, :], nw_ref, scalen_ref, shiftn_ref)
        acc = jnp.dot(h_cur[...], w_ref[...], preferred_element_type=F32)
        o_ref[...] = acc.astype(o_ref.dtype)

        @pl.when(j == 0)
        def _():
            hout_ref[...] = h_cur[...]

        @pl.when(j == gate_tile)
        def _():
            g_ref[...] = acc[:, gate_col:gate_col + V7X_LANES]

    @pl.when(i % 2 == 0)
    def _():
        step(ha_ref, hb_ref)

    @pl.when(i % 2 == 1)
    def _():
        step(hb_ref, ha_ref)


def _row_of_tile(tm, seq, first_row):
    assert seq % tm == 0
    return lambda i: first_row + (i * tm) // seq


def _inproj_call(x2d, nw, mod_rows, w_all, seq, first_row):
    t = _tiles()
    m, d = x2d.shape
    n = PACK_COLS
    tm, tn = min(t["in_tm"], seq), t["in_tn"]
    ni, nj = m // tm, n // tn
    nchunk = max(c for c in (1, 2, 4, 8) if c < nj)
    row_of_tile = _row_of_tile(tm, seq, first_row)
    nxt = lambda i: jnp.minimum(i + 1, ni - 1)
    return pl.pallas_call(
        functools.partial(_inproj_kernel, gate_tile=COL_AG // tn, gate_col=COL_AG % tn, nchunk=nchunk),
        grid=(ni, nj),
        in_specs=[
            pl.BlockSpec((tm, d), lambda i, j: (jnp.where((i == 0) & (j == 0), 0, nxt(i)), 0)),
            pl.BlockSpec((1, d), lambda i, j: (0, 0)),
            _mod_spec(d, lambda i: row_of_tile(0), 1),
            _mod_spec(d, lambda i: row_of_tile(0), 0),
            _mod_spec(d, lambda i: row_of_tile(nxt(i)), 1),
            _mod_spec(d, lambda i: row_of_tile(nxt(i)), 0),
            pl.BlockSpec((d, tn), lambda i, j: (0, j)),
        ],
        out_specs=[pl.BlockSpec((tm, tn), lambda i, j: (i, j)),
                   pl.BlockSpec((tm, V7X_LANES), lambda i, j: (i, 0)),
                   pl.BlockSpec((tm, d), lambda i, j: (i, 0))],
        out_shape=[jax.ShapeDtypeStruct((m, n), BF16), jax.ShapeDtypeStruct((m, V7X_LANES), F32),
                   jax.ShapeDtypeStruct((m, d), BF16)],
        scratch_shapes=[pltpu.VMEM((tm, d), BF16), pltpu.VMEM((tm, d), BF16)],
        compiler_params=pltpu.CompilerParams(
            dimension_semantics=("arbitrary", "arbitrary"), vmem_limit_bytes=t["vmem_big"]),
        name="inproj",
    )(x2d, nw, mod_rows, mod_rows, mod_rows, mod_rows, w_all)


def _mlstm_kernel(q_ref, k_ref, v_ref, g_ref, o_ref, z_ref, bg_ref, nw_ref, y_ref,
                  hfw_ref, c_ref, n_ref, m_ref, *, nblk, cb):
    s = pl.program_id(1)
    is_fwd = s < nblk
    blk = jnp.where(is_fwd, s, 2 * nblk - 1 - s)
    sgn = jnp.where(is_fwd, 1, -1)

    @pl.when((s == 0) | (s == nblk))
    def _():
        c_ref[...] = jnp.zeros_like(c_ref)
        n_ref[...] = jnp.zeros_like(n_ref)
        m_ref[...] = jnp.full_like(m_ref, M_INIT)

    ri = lax.broadcasted_iota(jnp.int32, (CHUNK, CHUNK), 0)
    ci = lax.broadcasted_iota(jnp.int32, (CHUNK, CHUNK), 1)
    tri = (((ri - ci) * sgn) >= 0).astype(F32)
    keep_t = ((ci - ri) * sgn) >= 0
    rw = lax.broadcasted_iota(jnp.int32, (BRANCH_W, BRANCH_W), 0)
    cw = lax.broadcasted_iota(jnp.int32, (BRANCH_W, BRANCH_W), 1)
    eye = (rw == cw).astype(BF16)
    bg = bg_ref[...]
    hsl = [slice(h * A_DH, (h + 1) * A_DH) for h in range(A_HEADS)]
    ct_st = [c_ref[:, hsl[h]] for h in range(A_HEADS)]
    n_st = [n_ref[0:1, hsl[h]] for h in range(A_HEADS)]
    m_st = [m_ref[h:h + 1, 0:1] for h in range(A_HEADS)]
    n_pad = jnp.zeros((2 * V7X_SUBLANES - 1, A_DH), BF16)

    chunks, houts = [], []
    for jj in range(cb):
        lc = jnp.where(is_fwd, jj, cb - 1 - jj)
        chunks.append(lc)
        tsl = pl.ds(pl.multiple_of(lc * CHUNK, CHUNK), CHUNK)
        gt = g_ref[tsl, :] + bg
        gt = jnp.where(is_fwd, gt, pltpu.roll(gt, V7X_LANES - A_HEADS, axis=1))
        logf = _log_sigmoid(gt)
        bc = jnp.dot(tri, logf, preferred_element_type=F32, precision=lax.Precision.HIGHEST)
        bc_t = bc.T
        g_row = jnp.where(is_fwd, bc[CHUNK - 1:CHUNK, :], bc[0:1, :])
        q_all = q_ref[tsl, :].astype(BF16)
        k_all = k_ref[tsl, :].astype(BF16)
        vt_all = lax.dot_general(eye, v_ref[tsl, :].astype(BF16), (((1,), (1,)), ((), ())),
                                 preferred_element_type=F32).astype(BF16)
        kq_pair, vt_bd, qcn_pair = [], [], []
        for p in range(A_HEADS // 2):
            ha, hb = 2 * p, 2 * p + 1
            psl = slice(ha * A_DH, (hb + 1) * A_DH)
            first = lax.broadcasted_iota(jnp.int32, (1, 2 * A_DH), 1) < A_DH
            k_pair, q_pair = k_all[:, psl], q_all[:, psl]
            zk = jnp.zeros_like(k_pair)
            k_bd = jnp.concatenate([jnp.where(first, k_pair, zk), jnp.where(first, zk, k_pair)], axis=0)
            kq_pair.append(lax.dot_general(k_bd, q_pair, (((1,), (1,)), ((), ())),
                                           preferred_element_type=F32))
            zblk = jnp.zeros((A_DH, CHUNK), BF16)
            vt_bd.append(jnp.concatenate([jnp.concatenate([vt_all[hsl[ha], :], zblk], axis=1),
                                          jnp.concatenate([zblk, vt_all[hsl[hb], :]], axis=1)], axis=0))
            ctn_a = jnp.concatenate([ct_st[ha].astype(BF16), n_st[ha].astype(BF16), n_pad], axis=0)
            ctn_b = jnp.concatenate([ct_st[hb].astype(BF16), n_st[hb].astype(BF16), n_pad], axis=0)
            zc = jnp.zeros_like(ctn_a)
            ctn_bd = jnp.concatenate([jnp.concatenate([ctn_a, zc], axis=1),
                                      jnp.concatenate([zc, ctn_b], axis=1)], axis=0)
            qcn_pair.append(lax.dot_general(ctn_bd, q_pair, (((1,), (1,)), ((), ())),
                                            preferred_element_type=F32))
        ctn_h = A_DH + 2 * V7X_SUBLANES

        smats, a_ins, m_ts, kas, a_olds, a_news = [], [], [], [], [], []
        for h in range(A_HEADS):
            fcol = 2 * A_HEADS + h
            b_col = bc[:, fcol:fcol + 1]
            i_col = gt[:, h:h + 1]
            g_tot = g_row[:, fcol:fcol + 1]
            rmat = jnp.where(keep_t, i_col - b_col, -jnp.inf)
            big_m = jnp.maximum(jnp.max(rmat, axis=0, keepdims=True), m_st[h])
            kq = kq_pair[h // 2][(h % 2) * CHUNK:(h % 2 + 1) * CHUNK, :]
            smats.append(kq * jnp.exp(rmat - (big_m - LOG_K_SCALE)))
            a_ins.append(jnp.exp(m_st[h] - big_m))
            m_ts.append(bc_t[fcol:fcol + 1, :] + big_m)

            w_end = g_tot - b_col + i_col
            m_loc = jnp.max(w_end, axis=0, keepdims=True)
            kas.append(k_all[:, hsl[h]].astype(F32) * jnp.exp(w_end - (m_loc - LOG_K_SCALE)))
            m_new = jnp.maximum(g_tot + m_st[h], m_loc)
            a_olds.append(jnp.exp(g_tot + m_st[h] - m_new))
            a_news.append(jnp.exp(m_loc - m_new))
            m_st[h] = m_new

        num_pair = [jnp.dot(vt_bd[p], jnp.concatenate(smats[2 * p:2 * p + 2], axis=0).astype(BF16),
                            preferred_element_type=F32) for p in range(A_HEADS // 2)]
        ct_loc_pair = [jnp.dot(vt_bd[p], jnp.concatenate(kas[2 * p:2 * p + 2], axis=0).astype(BF16),
                               preferred_element_type=F32) for p in range(A_HEADS // 2)]
        hrow = []
        for h in range(A_HEADS):
            p, j = h // 2, h % 2
            qcn = qcn_pair[p][j * ctn_h:(j + 1) * ctn_h, :]
            den = jnp.sum(smats[h], axis=0, keepdims=True) + a_ins[h] * qcn[A_DH:A_DH + 1, :]
            hrow.append((num_pair[p][j * A_DH:(j + 1) * A_DH, :] + a_ins[h] * qcn[:A_DH, :])
                        / jnp.maximum(jnp.abs(den), jnp.exp(-m_ts[h])))
            ct_st[h] = a_olds[h] * ct_st[h] + a_news[h] * ct_loc_pair[p][j * A_DH:(j + 1) * A_DH, :]
            n_st[h] = a_olds[h] * n_st[h] + a_news[h] * jnp.sum(kas[h], axis=0, keepdims=True)
        houts.append(hrow)

    for h in range(A_HEADS):
        c_ref[:, hsl[h]] = ct_st[h]
        n_ref[0:1, hsl[h]] = n_st[h]
        m_ref[h:h + 1, :] = jnp.broadcast_to(m_st[h], (1, V7X_LANES))

    @pl.when(is_fwd)
    def _():
        for jj in range(cb):
            for h in range(A_HEADS):
                hfw_ref[blk * cb + chunks[jj], hsl[h], :] = houts[jj][h]

    @pl.when(jnp.logical_not(is_fwd))
    def _():
        for jj in range(cb):
            tsl = pl.ds(pl.multiple_of(chunks[jj] * CHUNK, CHUNK), CHUNK)
            for h in range(A_HEADS):
                tot = hfw_ref[blk * cb + chunks[jj], hsl[h], :] + houts[jj][h]
                tot = tot * lax.rsqrt(jnp.mean(tot * tot, axis=0, keepdims=True) + EPS)
                yh = tot.T * nw_ref[:, hsl[h]] * _sigmoid(o_ref[tsl, hsl[h]].astype(F32))
                y_ref[tsl, hsl[h]] = (yh * _silu(z_ref[tsl, hsl[h]].astype(F32))).astype(BF16)


def _mlstm_call(p3, g3, bg_pad, nw):
    t = _tiles()
    bsz, seq, _ = p3.shape
    nc = seq // CHUNK
    cb = min(MLSTM_BLOCK_CHUNKS, nc)
    assert nc % cb == 0
    nblk = nc // cb
    rows = cb * CHUNK

    def cur(s):
        return jnp.where(s < nblk, s, 2 * nblk - 1 - s)

    def late(s):
        return jnp.where(s < nblk, nblk - 1, 2 * nblk - 1 - s)

    def blk(col, which):
        return pl.BlockSpec((None, rows, BRANCH_W), lambda b, s: (b, which(s), col // BRANCH_W))

    return pl.pallas_call(
        functools.partial(_mlstm_kernel, nblk=nblk, cb=cb),
        grid=(bsz, 2 * nblk),
        in_specs=[
            blk(COL_AQ, cur), blk(COL_AK, cur), blk(COL_AV, cur),
            pl.BlockSpec((None, rows, V7X_LANES), lambda b, s: (b, cur(s), 0)),
            blk(COL_AO, late), blk(COL_AZ, late),
            pl.BlockSpec((1, V7X_LANES), lambda b, s: (0, 0)),
            pl.BlockSpec((1, BRANCH_W), lambda b, s: (0, 0)),
        ],
        out_specs=pl.BlockSpec((None, rows, BRANCH_W), lambda b, s: (b, late(s), 0)),
        out_shape=jax.ShapeDtypeStruct((bsz, seq, BRANCH_W), BF16),
        scratch_shapes=[
            pltpu.VMEM((nc, BRANCH_W, CHUNK), F32),
            pltpu.VMEM((A_DH, BRANCH_W), F32),
            pltpu.VMEM((V7X_SUBLANES, BRANCH_W), F32),
            pltpu.VMEM((V7X_SUBLANES, V7X_LANES), F32),
        ],
        compiler_params=pltpu.CompilerParams(
            dimension_semantics=("arbitrary", "arbitrary"), vmem_limit_bytes=t["vmem_big"]),
        name="mlstm",
    )(p3, p3, p3, g3, p3, p3, bg_pad, nw)


def _nattn_kernel(q_ref, kc_ref, kx_ref, vc_ref, vx_ref, z_ref, qw_ref, kw_ref, bias_ref, y_ref,
                  kn_ref, vn_ref, *, nblk):
    i = pl.program_id(1)

    def put(slot, k_src, v_src):
        dst = pl.ds(pl.multiple_of(slot * NA_BLOCK_TOK, NA_BLOCK_TOK), NA_BLOCK_TOK)
        for h in range(B_HEADS):
            hs = slice(h * B_DH, (h + 1) * B_DH)
            kn_ref[dst, hs] = _rms(k_src[:, hs].astype(F32), kw_ref[...]).astype(BF16)
        vn_ref[dst, :] = v_src[...].astype(BF16)

    @pl.when(i == 0)
    def _():
        put(0, kc_ref, vc_ref)

    nslot = (i + 1) % 3
    put(nslot, kx_ref, vx_ref)

    @pl.when((i == 0) | (nslot == 0))
    def _():
        mirror = slice(3 * NA_BLOCK_TOK, 4 * NA_BLOCK_TOK)
        kn_ref[mirror, :] = kn_ref[0:NA_BLOCK_TOK, :]
        vn_ref[mirror, :] = vn_ref[0:NA_BLOCK_TOK, :]

    qn = []
    for h in range(B_HEADS):
        hs = slice(h * B_DH, (h + 1) * B_DH)
        qn.append((_rms(q_ref[:, hs].astype(F32), qw_ref[...]) * (B_DH ** -0.5 * LOG2E)).astype(BF16))

    half = NA_BLOCK_TOK // 2
    eye = (lax.broadcasted_iota(jnp.int32, (B_DH, B_DH), 0)
           == lax.broadcasted_iota(jnp.int32, (B_DH, B_DH), 1)).astype(BF16)
    starts = (jnp.where(i == 0, 0, ((i + 2) % 3) * NA_BLOCK_TOK + half), (i % 3) * NA_BLOCK_TOK)
    variants = (jnp.where(i == 0, 1, 0), jnp.where(i == nblk - 1, 2, 0))
    for quad in range(NA_BLOCK_ROWS // NA_QUAD_ROWS):
        qrows = slice(quad * NA_QUAD_TOK, (quad + 1) * NA_QUAD_TOK)
        win = pl.ds(pl.multiple_of(starts[quad], half), NA_WIN_TOK)
        for h in range(B_HEADS):
            hs = slice(h * B_DH, (h + 1) * B_DH)
            sc = lax.dot_general(kn_ref[win, hs], qn[h][qrows, :], (((1,), (1,)), ((), ())),
                                 preferred_element_type=F32)
            sc = sc + bias_ref[variants[quad], h]
            m = jnp.max(sc, axis=0, keepdims=True)
            p = jnp.exp2(sc - m)
            l = jnp.sum(p, axis=0, keepdims=True)
            vt = lax.dot_general(eye, vn_ref[win, hs], (((1,), (1,)), ((), ())),
                                 preferred_element_type=F32).astype(BF16)
            o = (jnp.dot(vt, p.astype(BF16), preferred_element_type=F32) / l).T
            y_ref[qrows, hs] = (o * _silu(z_ref[qrows, hs].astype(F32))).astype(BF16)


def _nattn_call(p3, qw, kw, bias_quads):
    t = _tiles()
    bsz, seq, _ = p3.shape
    rows = seq // GRID_W
    assert rows >= WIN_ROWS and rows % NA_BLOCK_ROWS == 0
    nblk = rows // NA_BLOCK_ROWS

    def blk(col, shift):
        cb = col // BRANCH_W
        return pl.BlockSpec((None, NA_BLOCK_TOK, BRANCH_W),
                            lambda b, i: (b, jnp.clip(i + shift, 0, nblk - 1), cb))

    return pl.pallas_call(
        functools.partial(_nattn_kernel, nblk=nblk),
        grid=(bsz, nblk),
        in_specs=[
            blk(COL_BQ, 0),
            blk(COL_BK, 0), blk(COL_BK, 1),
            blk(COL_BV, 0), blk(COL_BV, 1),
            blk(COL_BZ, 0),
            pl.BlockSpec((1, B_DH), lambda b, i: (0, 0)),
            pl.BlockSpec((1, B_DH), lambda b, i: (0, 0)),
            pl.BlockSpec(bias_quads.shape, lambda b, i: (0, 0, 0, 0)),
        ],
        out_specs=pl.BlockSpec((None, NA_BLOCK_TOK, BRANCH_W), lambda b, i: (b, i, 0)),
        out_shape=jax.ShapeDtypeStruct((bsz, seq, BRANCH_W), BF16),
        scratch_shapes=[
            pltpu.VMEM((NA_RING_SLOTS * NA_BLOCK_TOK, BRANCH_W), BF16),
            pltpu.VMEM((NA_RING_SLOTS * NA_BLOCK_TOK, BRANCH_W), BF16),
        ],
        compiler_params=pltpu.CompilerParams(
            dimension_semantics=("arbitrary", "arbitrary"), vmem_limit_bytes=t["vmem_big"]),
        name="nattn",
    )(p3, p3, p3, p3, p3, p3, qw, kw, bias_quads)


def _skew(w, nq, first, ncol):
    width = w.shape[-1]
    assert first - (nq - 1) >= 0 and first + ncol <= width - 1
    lead = w.shape[:-1]
    flat = jnp.tile(w, (1,) * len(lead) + (nq,))[..., :nq * (width - 1)]
    return flat.reshape(lead + (nq, width - 1))[..., first:first + ncol]


def _nattn_bias(rpb):
    heads = rpb.shape[0]
    reach = GRID_W - WIN_COLS
    w = jnp.pad(rpb.astype(F32), ((0, 0), (0, 0), (reach, reach + 1)))
    tbl = _skew(w, GRID_W, GRID_W - 1, GRID_W)
    qc = np.arange(GRID_W)[:, None]
    kc = np.arange(GRID_W)[None, :]
    qs = np.clip(qc - WIN_COLS // 2, 0, GRID_W - WIN_COLS)
    ok = (kc >= qs) & (kc < qs + WIN_COLS)
    tbl = jnp.where(ok[None, None], tbl, -jnp.inf)
    per_d = [jnp.concatenate([tbl[:, d + j] for j in range(WIN_ROWS)], axis=-1) for d in range(WIN_ROWS)]
    fill = lambda n: jnp.full((heads, GRID_W, n), -jnp.inf, F32)

    def quad(ds, shifts):
        rows = [jnp.concatenate([fill(GRID_W * sh), per_d[d], fill(NA_WIN_TOK - NA_BLOCK_TOK - GRID_W * sh)],
                                axis=-1) for d, sh in zip(ds, shifts)]
        return jnp.concatenate(rows, axis=1)

    mid = WIN_ROWS - 1 - WIN_ROWS // 2
    interior = quad((mid,) * NA_QUAD_ROWS, range(NA_QUAD_ROWS))
    top = quad(range(WIN_ROWS - 1, WIN_ROWS - 1 - NA_QUAD_ROWS, -1), (0,) * NA_QUAD_ROWS)
    bottom = quad(range(NA_QUAD_ROWS - 1, -1, -1), (0,) * NA_QUAD_ROWS)
    return jnp.swapaxes(jnp.stack([interior, top, bottom]), 2, 3) * LOG2E


def _swa_kernel(q_ref, kp_ref, kc_ref, kx_ref, vp_ref, vc_ref, vx_ref, z_ref, qw_ref, kw_ref,
                sink_ref, bias_ref, y_ref, *, nsteps, nq):
    n = pl.program_id(1)
    band = 3 * C_BLOCK
    lane = lax.broadcasted_iota(jnp.int32, (1, V7X_LANES), 1)
    lo = lane < C_DH

    def pair_rms(x, w):
        x2 = x * x
        s_lo = jnp.sum(jnp.where(lo, x2, 0.0), axis=-1, keepdims=True)
        s_hi = jnp.sum(jnp.where(lo, 0.0, x2), axis=-1, keepdims=True)
        r = lax.rsqrt(jnp.where(lo, s_lo, s_hi) * (1.0 / C_DH) + EPS)
        return x * r * w

    kn = pair_rms(jnp.concatenate([kp_ref[...], kc_ref[...], kx_ref[...]], axis=0).astype(F32), kw_ref[...])
    vc = jnp.concatenate([vp_ref[...], vc_ref[...], vx_ref[...]], axis=0).astype(F32)
    kn_sw = pltpu.roll(kn, C_DH, axis=1)
    vc_sw = pltpu.roll(vc, C_DH, axis=1)
    stacks = []
    for g in range(C_KV_HEADS):
        k_lo, k_hi = (kn, kn_sw) if g == 0 else (kn_sw, kn)
        v_lo, v_hi = (vc, vc_sw) if g == 0 else (vc_sw, vc)
        stacks.append((jnp.where(lo, k_lo, 0.0).astype(BF16), jnp.where(lo, 0.0, k_hi).astype(BF16),
                       jnp.where(lo, v_lo, 0.0).astype(BF16), jnp.where(lo, 0.0, v_hi).astype(BF16)))

    ki = lax.broadcasted_iota(jnp.int32, (2 * band, 1), 0)
    ki = jnp.where(ki >= band, ki - band, ki)
    first_slab = lax.broadcasted_iota(jnp.int32, (1, 2 * C_BLOCK), 1) < C_BLOCK
    even_rows = lax.broadcasted_iota(jnp.int32, (V7X_LANES, 1), 0) < C_DH
    eye = (lax.broadcasted_iota(jnp.int32, (V7X_LANES, V7X_LANES), 0)
           == lax.broadcasted_iota(jnp.int32, (V7X_LANES, V7X_LANES), 1)).astype(BF16)
    for qb in range(nq):
        qrows = slice(qb * C_BLOCK, (qb + 1) * C_BLOCK)
        krows = slice((nq + qb - 1) * C_BLOCK, (nq + qb + 2) * C_BLOCK)
        has_prev = (n > 0) if qb == 0 else True
        has_next = (n < nsteps - 1) if qb == nq - 1 else True
        valid = ((ki >= C_BLOCK) | has_prev) & ((ki < 2 * C_BLOCK) | has_next)
        for g in range(C_KV_HEADS):
            k_even, k_odd, v_even, v_odd = stacks[g]
            kk = jnp.concatenate([k_even[krows], k_odd[krows]], axis=0)
            vv = jnp.concatenate([v_even[krows], v_odd[krows]], axis=0)
            slabs = [slice(p * V7X_LANES, (p + 1) * V7X_LANES) for p in (2 * g, 2 * g + 1)]
            qn = jnp.concatenate(
                [(pair_rms(q_ref[qrows, sl].astype(F32), qw_ref[...]) * (C_DH ** -0.5 * LOG2E)).astype(BF16)
                 for sl in slabs], axis=0)
            sc = lax.dot_general(kk, qn, (((1,), (1,)), ((), ())), preferred_element_type=F32)
            sc = jnp.where(valid, sc + bias_ref[g], -jnp.inf)
            probs, denoms = [], []
            for e in range(2):
                se = sc[e * band:(e + 1) * band, :]
                h0 = C_GROUP * g + e
                sink = jnp.where(first_slab, sink_ref[h0:h0 + 1, 0:1], sink_ref[h0 + 2:h0 + 3, 0:1])
                m = jnp.maximum(jnp.max(se, axis=0, keepdims=True), sink)
                pe = jnp.exp2(se - m)
                probs.append(pe.astype(BF16))
                denoms.append(jnp.sum(pe, axis=0, keepdims=True) + jnp.exp2(sink - m))
            vvt = lax.dot_general(eye, vv, (((1,), (1,)), ((), ())), preferred_element_type=F32).astype(BF16)
            ot = jnp.dot(vvt, jnp.concatenate(probs, axis=0), preferred_element_type=F32)
            o = (ot / jnp.where(even_rows, denoms[0], denoms[1])).T
            for j, sl in enumerate(slabs):
                rows = slice(j * C_BLOCK, (j + 1) * C_BLOCK)
                y_ref[qrows, sl] = (o[rows] * _silu(z_ref[qrows, sl].astype(F32))).astype(BF16)


def _swa_call(p3, qw, kw, sink_rows, bias):
    t = _tiles()
    bsz, seq, _ = p3.shape
    nq = SWA_STEP_BLOCKS
    rows = nq * C_BLOCK
    assert seq % rows == 0
    nsteps = seq // rows

    def wide(col):
        return pl.BlockSpec((None, rows, BRANCH_W), lambda b, n: (b, n, col // BRANCH_W))

    def kv(col, shift):
        return pl.BlockSpec((None, rows, C_KV_W),
                            lambda b, n: (b, jnp.clip(n + shift, 0, nsteps - 1), col // C_KV_W))

    return pl.pallas_call(
        functools.partial(_swa_kernel, nsteps=nsteps, nq=nq),
        grid=(bsz, nsteps),
        in_specs=[
            wide(COL_CQ),
            kv(COL_CK, -1), kv(COL_CK, 0), kv(COL_CK, 1),
            kv(COL_CV, -1), kv(COL_CV, 0), kv(COL_CV, 1),
            wide(COL_CZ),
            pl.BlockSpec((1, V7X_LANES), lambda b, n: (0, 0)),
            pl.BlockSpec((1, V7X_LANES), lambda b, n: (0, 0)),
            pl.BlockSpec(sink_rows.shape, lambda b, n: (0, 0)),
            pl.BlockSpec(bias.shape, lambda b, n: (0, 0, 0)),
        ],
        out_specs=pl.BlockSpec((None, rows, BRANCH_W), lambda b, n: (b, n, 0)),
        out_shape=jax.ShapeDtypeStruct((bsz, seq, BRANCH_W), BF16),
        compiler_params=pltpu.CompilerParams(
            dimension_semantics=("arbitrary", "arbitrary"), vmem_limit_bytes=t["vmem_small"]),
        name="swa",
    )(p3, p3, p3, p3, p3, p3, p3, p3, qw, kw, sink_rows, bias)


def _t5_bucket_rel():
    rel = np.arange(-(2 * C_BLOCK - 1), 2 * C_BLOCK)
    half = N_BUCKETS // 2
    max_exact = half // 2
    n = np.abs(rel)
    nf = np.maximum(n, 1).astype(np.float32)
    scale = np.float32(math.log(MAX_DIST / max_exact))
    large = max_exact + (np.log(nf / np.float32(max_exact)) / scale
                         * np.float32(half - max_exact)).astype(np.int32)
    large = np.minimum(large, half - 1)
    bucket = np.where(rel > 0, half, 0) + np.where(n < max_exact, n, large)
    return bucket, n <= C_WINDOW


def _swa_bias(rel_bias):
    bucket, in_window = _t5_bucket_rel()
    per_rel = jnp.where(in_window[:, None], rel_bias.astype(F32)[bucket], -jnp.inf)
    w = jnp.pad(per_rel.T, ((0, 0), (0, 1)))
    tbl = _skew(w, C_BLOCK, C_BLOCK - 1, 3 * C_BLOCK)
    slabs = jnp.concatenate([tbl[0::2], tbl[1::2]], axis=-1)
    return jnp.swapaxes(slabs.reshape(C_KV_HEADS, 2 * C_BLOCK, 2 * 3 * C_BLOCK), 1, 2) * LOG2E


def _conv_kernel(a_ref, g_ref, ap_ref, gp_ref, ax_ref, gx_ref, z_ref, cw_ref, cb_ref, lw_ref, lb_ref,
                 y_ref, ext_ref, sh_ref, *, ntile):
    i = pl.program_id(1)
    tt = a_ref.shape[0]
    glu = lambda a, g: a[...].astype(F32) * _sigmoid(g[...].astype(F32))
    prev = jnp.where(i > 0, glu(ap_ref, gp_ref), 0.0)
    nxt = jnp.where(i < ntile - 1, glu(ax_ref, gx_ref), 0.0)
    ext_ref[0:CONV_HALO, :] = prev
    ext_ref[CONV_HALO:CONV_HALO + tt, :] = glu(a_ref, g_ref)
    ext_ref[CONV_HALO + tt:2 * CONV_HALO + tt, :] = nxt
    span = sh_ref.shape[1]
    for r in range(1, V7X_SUBLANES):
        sh_ref[r - 1] = ext_ref[pl.ds(r, span), :]
    acc = None
    for w in range(CONV_W):
        a, r = divmod(CONV_HALO - CONV_W // 2 + w, V7X_SUBLANES)
        lo = a * V7X_SUBLANES
        src = ext_ref[lo:lo + tt, :] if r == 0 else sh_ref[r - 1, lo:lo + tt, :]
        term = src * cw_ref[w:w + 1, :]
        acc = term if acc is None else acc + term
    u = acc + cb_ref[...]
    mu = jnp.mean(u, axis=-1, keepdims=True)
    var = jnp.mean(jnp.square(u - mu), axis=-1, keepdims=True)
    u = (u - mu) * lax.rsqrt(var + EPS) * lw_ref[...] + lb_ref[...]
    y_ref[...] = (_silu(u) * _silu(z_ref[...].astype(F32))).astype(BF16)


def _conv_call(p3, cw_pad, cb, lw, lb):
    t = _tiles()
    bsz, seq, _ = p3.shape
    tt = CONV_TILE
    ntile = seq // tt
    per = tt // CONV_HALO
    nhalo = seq // CONV_HALO
    span = tt + (2 * CONV_HALO - 1) // V7X_SUBLANES * V7X_SUBLANES

    def cur(col):
        return pl.BlockSpec((None, tt, BRANCH_W), lambda b, i: (b, i, col // BRANCH_W))

    def halo(col, after):
        def idx(b, i):
            r = (i + 1) * per if after else i * per - 1
            return (b, jnp.clip(r, 0, nhalo - 1), col // BRANCH_W)
        return pl.BlockSpec((None, CONV_HALO, BRANCH_W), idx)

    vec = pl.BlockSpec((1, BRANCH_W), lambda b, i: (0, 0))
    return pl.pallas_call(
        functools.partial(_conv_kernel, ntile=ntile),
        grid=(bsz, ntile),
        in_specs=[
            cur(COL_DA), cur(COL_DG),
            halo(COL_DA, False), halo(COL_DG, False), halo(COL_DA, True), halo(COL_DG, True),
            cur(COL_DZ),
            pl.BlockSpec(cw_pad.shape, lambda b, i: (0, 0)),
            vec, vec, vec,
        ],
        out_specs=pl.BlockSpec((None, tt, BRANCH_W), lambda b, i: (b, i, 0)),
        out_shape=jax.ShapeDtypeStruct((bsz, seq, BRANCH_W), BF16),
        scratch_shapes=[
            pltpu.VMEM((tt + 2 * CONV_HALO, BRANCH_W), F32),
            pltpu.VMEM((V7X_SUBLANES - 1, span, BRANCH_W), F32),
        ],
        compiler_params=pltpu.CompilerParams(
            dimension_semantics=("arbitrary", "arbitrary"), vmem_limit_bytes=t["vmem_small"]),
        name="conv",
    )(p3, p3, p3, p3, p3, p3, p3, cw_pad, cb, lw, lb)


def _merge_kernel(h_ref, ya_ref, yb_ref, yc_ref, yd_ref, wma_ref, wmb_ref, wmc_ref, wmd_ref, wb_ref, o_ref):
    h = h_ref[...]
    merged = None
    branches = ((ya_ref, wma_ref), (yb_ref, wmb_ref), (yc_ref, wmc_ref), (yd_ref, wmd_ref))
    for b, (y_ref, wm_ref) in enumerate(branches):
        gate = _sigmoid(jnp.dot(h, wm_ref[...], preferred_element_type=F32))
        term = gate * jnp.dot(y_ref[...], wb_ref[b], preferred_element_type=F32)
        merged = term if merged is None else merged + term
    o_ref[...] = merged.astype(BF16)


def _merge_call(h2d, ys, w_all, w_branch, seq):
    t = _tiles()
    m, d = h2d.shape
    tm, tn = min(t["mg_tm"], seq), t["mg_tn"]
    yspec = pl.BlockSpec((tm, BRANCH_W), lambda i, j: (i, 0))
    return pl.pallas_call(
        _merge_kernel,
        grid=(m // tm, d // tn),
        in_specs=[
            pl.BlockSpec((tm, d), lambda i, j: (i, 0)),
            yspec, yspec, yspec, yspec,
            *[pl.BlockSpec((d, tn), functools.partial(lambda i, j, b: (0, (PACK_COLS + b * d) // tn + j), b=b))
              for b in range(N_BRANCH)],
            pl.BlockSpec((N_BRANCH, BRANCH_W, tn), lambda i, j: (0, 0, j)),
        ],
        out_specs=pl.BlockSpec((tm, tn), lambda i, j: (i, j)),
        out_shape=jax.ShapeDtypeStruct((m, d), BF16),
        compiler_params=pltpu.CompilerParams(
            dimension_semantics=("arbitrary", "arbitrary"), vmem_limit_bytes=t["vmem_big"]),
        name="merge",
    )(h2d, *ys, *([w_all] * N_BRANCH), w_branch)


def _outproj_kernel(x_ref, mg_ref, gate_ref, wo_ref, o_ref):
    o_ref[...] = x_ref[...] + gate_ref[...] * jnp.dot(mg_ref[...], wo_ref[...], preferred_element_type=F32)


def _outproj_call(x2d, merged, mod_rows, w_out, seq, first_row):
    t = _tiles()
    m, d = x2d.shape
    tm, tn = min(t["op_tm"], seq), t["op_tn"]
    row_of_tile = _row_of_tile(tm, seq, first_row)
    return pl.pallas_call(
        _outproj_kernel,
        grid=(m // tm, d // tn),
        in_specs=[
            pl.BlockSpec((tm, tn), lambda i, j: (i, j)),
            pl.BlockSpec((tm, d), lambda i, j: (i, 0)),
            pl.BlockSpec((None, 1, tn), lambda i, j: (row_of_tile(i) * 3 + 2, 0, j)),
            pl.BlockSpec((d, tn), lambda i, j: (0, j)),
        ],
        out_specs=pl.BlockSpec((tm, tn), lambda i, j: (i, j)),
        out_shape=jax.ShapeDtypeStruct((m, d), F32),
        compiler_params=pltpu.CompilerParams(
            dimension_semantics=("arbitrary", "arbitrary"), vmem_limit_bytes=t["vmem_big"]),
        name="outproj",
    )(x2d, merged, mod_rows, w_out)


REPACK_TILE = 256
GATE_SHIFT = A_GATES


def _repack_plan(d):
    sizes = (512, 512, 512, 512, 512, A_GATES, 512, 512, 512, 512, 512, C_KV_W, C_KV_W, 512, 1024, 512,
             N_BRANCH * d)
    names = ("aq", "ak", "av", "ao", "az", "ag", "bq", "bk", "bv", "bz", "cq", "ck", "cv", "cz", "dglu", "dz", "mg")
    src = dict(zip(names, np.concatenate([[0], np.cumsum(sizes)])[:-1]))
    groups = [
        (COL_AQ, src["aq"], 5 * BRANCH_W), (COL_BQ, src["bq"], 4 * BRANCH_W), (COL_DA, src["dglu"], 3 * BRANCH_W),
        (COL_CQ, src["cq"], BRANCH_W), (COL_CZ, src["cz"], BRANCH_W), (COL_CK, src["ck"], 2 * C_KV_W),
        (COL_AG, src["ag"], REPACK_TILE), (PACK_COLS, src["mg"], N_BRANCH * d)]
    ntile = (PACK_COLS + N_BRANCH * d) // REPACK_TILE
    a_idx, b_idx, mode = np.zeros(ntile, np.int32), np.zeros(ntile, np.int32), np.zeros(ntile, np.int32)
    for dst, s0, width in groups:
        for k in range(width // REPACK_TILE):
            t = dst // REPACK_TILE + k
            start = int(s0) + k * REPACK_TILE
            off = start % REPACK_TILE
            assert off in (0, GATE_SHIFT) and dst % REPACK_TILE == 0
            a_idx[t] = start // REPACK_TILE
            b_idx[t] = (start - off + REPACK_TILE) // V7X_LANES
            mode[t] = 2 if dst == COL_AG else (1 if off else 0)
    return jnp.asarray(a_idx), jnp.asarray(b_idx), jnp.asarray(mode)


def _repack_kernel(a_idx, b_idx, mode, a_ref, b_ref, o_ref):
    t = pl.program_id(0)
    a = a_ref[...]
    lane = lax.broadcasted_iota(jnp.int32, (1, REPACK_TILE), 1)
    shifted = jnp.concatenate([a[:, GATE_SHIFT:], b_ref[:, :GATE_SHIFT]], axis=1)
    gates = jnp.where(lane < A_GATES, a, 0.0)
    m = mode[t]
    o_ref[...] = jnp.where(m == 1, shifted, jnp.where(m == 2, gates, a)).astype(BF16)


def _repack_call(w_in, layer):
    t = _tiles()
    _, d, _ = w_in.shape
    a_idx, b_idx, mode = _repack_plan(d)
    ntile = a_idx.shape[0]
    return pl.pallas_call(
        _repack_kernel,
        grid_spec=pltpu.PrefetchScalarGridSpec(
            num_scalar_prefetch=3,
            grid=(ntile,),
            in_specs=[
                pl.BlockSpec((None, d, REPACK_TILE), lambda i, a, b, m: (layer, 0, a[i])),
                pl.BlockSpec((None, d, V7X_LANES), lambda i, a, b, m: (layer, 0, b[i])),
            ],
            out_specs=pl.BlockSpec((d, REPACK_TILE), lambda i, a, b, m: (0, i)),
        ),
        out_shape=jax.ShapeDtypeStruct((d, ntile * REPACK_TILE), BF16),
        compiler_params=pltpu.CompilerParams(
            dimension_semantics=("arbitrary",), vmem_limit_bytes=t["vmem_small"]),
        name="repack",
    )(a_idx, b_idx, mode, w_in, w_in)


def kernel(x_prompt, x_sample, c_prompt, c_sample, rel_bias, norm_w, w_ada, b_ada, w_in, b_gate, mlstm_norm_w, na_q_norm, na_k_norm, na_rpb, swa_q_norm, swa_k_norm, swa_sink, conv_w, conv_b, conv_ln_w, conv_ln_b, w_branch, w_out):
    depth, d = norm_w.shape
    groups = ((x_prompt, 0), (x_sample, c_prompt.shape[0]))
    n_cond = c_prompt.shape[0] + c_sample.shape[0]
    cond_rows = -(-n_cond // V7X_SUBLANES) * V7X_SUBLANES
    c_all = jnp.concatenate([c_prompt, c_sample, jnp.zeros((cond_rows - n_cond, d), F32)], axis=0)
    mod = _ada_call(c_all, w_ada, b_ada)
    mod_rows = mod.reshape(depth * cond_rows * 3, 1, d)

    swa_bias = _swa_bias(rel_bias)
    w_in_bf16 = w_in.astype(BF16)
    layers = []
    for l in range(depth):
        layers.append(dict(
            w_all=_repack_call(w_in_bf16, l),
            w_branch=w_branch[l].astype(BF16), w_out=w_out[l].astype(BF16),
            nw=norm_w[l].reshape(1, d),
            bg=jnp.pad(b_gate[l], (0, V7X_LANES - A_GATES)).reshape(1, V7X_LANES),
            mnw=mlstm_norm_w[l].reshape(1, BRANCH_W),
            na_qw=na_q_norm[l].reshape(1, B_DH), na_kw=na_k_norm[l].reshape(1, B_DH),
            na_bias=_nattn_bias(na_rpb[l]),
            swa_qw=jnp.tile(swa_q_norm[l], 2).reshape(1, V7X_LANES),
            swa_kw=jnp.tile(swa_k_norm[l], 2).reshape(1, V7X_LANES),
            sink=jnp.broadcast_to(swa_sink[l].reshape(C_HEADS, 1) * LOG2E, (C_HEADS, V7X_LANES)),
            cw=jnp.pad(conv_w[l], ((0, 1), (0, 0))),
            cb=conv_b[l].reshape(1, BRANCH_W), lw=conv_ln_w[l].reshape(1, BRANCH_W),
            lb=conv_ln_b[l].reshape(1, BRANCH_W),
        ))

    outs = []
    for x, cond_off in groups:
        bsz, seq, _ = x.shape
        x2d = x.reshape(bsz * seq, d)
        for l, lw in enumerate(layers):
            first_row = l * cond_rows + cond_off
            p, gates, h2d = _inproj_call(x2d, lw["nw"], mod_rows, lw["w_all"], seq, first_row)
            p3 = p.reshape(bsz, seq, PACK_COLS)
            ya = _mlstm_call(p3, gates.reshape(bsz, seq, V7X_LANES), lw["bg"], lw["mnw"])
            yb = _nattn_call(p3, lw["na_qw"], lw["na_kw"], lw["na_bias"])
            yc = _swa_call(p3, lw["swa_qw"], lw["swa_kw"], lw["sink"], swa_bias)
            yd = _conv_call(p3, lw["cw"], lw["cb"], lw["lw"], lw["lb"])
            ys = [y.reshape(bsz * seq, BRANCH_W) for y in (ya, yb, yc, yd)]
            merged = _merge_call(h2d, ys, lw["w_all"], lw["w_branch"], seq)
            x2d = _outproj_call(x2d, merged, mod_rows, lw["w_out"], seq, first_row)
        outs.append(x2d.reshape(bsz, seq, d))
    return tuple(outs)
```

```python
import functools
import math

import numpy as np
import jax
import jax.numpy as jnp
from jax import lax
from jax.experimental import pallas as pl
from jax.experimental.pallas import tpu as pltpu

F32 = jnp.float32
BF16 = jnp.bfloat16

V7X_VMEM_BYTES = 64 * 1024 * 1024
V7X_LANES = 128
V7X_SUBLANES = 8
MIB = 1024 * 1024

EPS = 1e-6
N_BRANCH = 4
BRANCH_W = 512
GRID_W = 64
A_HEADS = 4
A_DH = BRANCH_W // A_HEADS
A_GATES = 4 * A_HEADS
CHUNK = 128
MLSTM_BLOCK_CHUNKS = 16
M_INIT = -1e30
LOG_K_SCALE = -0.5 * math.log(A_DH)
LOG2E = math.log2(math.e)
B_HEADS = 4
B_DH = BRANCH_W // B_HEADS
WIN_ROWS = 8
WIN_COLS = 16
NA_BLOCK_ROWS = 8
NA_BLOCK_TOK = NA_BLOCK_ROWS * GRID_W
NA_QUAD_ROWS = 4
NA_QUAD_TOK = NA_QUAD_ROWS * GRID_W
NA_WIN_TOK = (WIN_ROWS + NA_QUAD_ROWS) * GRID_W
NA_RING_SLOTS = 4
C_HEADS = 8
C_KV_HEADS = 2
C_GROUP = C_HEADS // C_KV_HEADS
C_DH = BRANCH_W // C_HEADS
C_KV_W = C_KV_HEADS * C_DH
C_WINDOW = 128
C_BLOCK = 128
SWA_STEP_BLOCKS = 4
N_BUCKETS = 32
MAX_DIST = 128
CONV_W = 31
CONV_HALO = 16
CONV_TILE = 1024

COL_AQ, COL_AK, COL_AV, COL_AO, COL_AZ = 0, 512, 1024, 1536, 2048
COL_BQ, COL_BK, COL_BV, COL_BZ = 2560, 3072, 3584, 4096
COL_DA, COL_DG, COL_DZ = 4608, 5120, 5632
COL_CQ, COL_CZ = 6144, 6656
COL_CK, COL_CV, COL_AG = 7168, 7296, 7424
PACK_COLS = 7680


def _tiles():
    return dict(
        ada_tn=1536,
        in_tm=1024, in_tn=1536,
        mg_tm=1024, mg_tn=512,
        op_tm=2048, op_tn=512,
        vmem_small=32 * MIB,
        vmem_big=V7X_VMEM_BYTES - 4 * MIB,
    )


def _sigmoid(x):
    return jax.nn.sigmoid(x)


def _silu(x):
    return x * jax.nn.sigmoid(x)


def _log_sigmoid(x):
    return jnp.minimum(x, 0.0) - jnp.log1p(jnp.exp(-jnp.abs(x)))


def _rms(x, w):
    r = lax.rsqrt(jnp.mean(x * x, axis=-1, keepdims=True) + EPS)
    return x * r * w


def _ada_kernel(c_ref, w_ref, b_ref, o_ref):
    c = c_ref[...]
    sc = _silu(c).astype(BF16)
    o_ref[...] = jnp.dot(sc, w_ref[...].astype(BF16), preferred_element_type=F32) + b_ref[...]


def _ada_call(c_all, w_ada, b_ada):
    t = _tiles()
    depth, d, n3 = w_ada.shape
    rows = c_all.shape[0]
    tn = t["ada_tn"]
    return pl.pallas_call(
        _ada_kernel,
        grid=(depth, n3 // tn),
        in_specs=[
            pl.BlockSpec((rows, d), lambda l, j: (0, 0)),
            pl.BlockSpec((None, d, tn), lambda l, j: (l, 0, j)),
            pl.BlockSpec((None, 1, tn), lambda l, j: (l, 0, j)),
        ],
        out_specs=pl.BlockSpec((None, rows, tn), lambda l, j: (l, 0, j)),
        out_shape=jax.ShapeDtypeStruct((depth, rows, n3), F32),
        compiler_params=pltpu.CompilerParams(
            dimension_semantics=("arbitrary", "arbitrary"), vmem_limit_bytes=t["vmem_small"]),
        name="ada",
    )(c_all, w_ada, b_ada.reshape(depth, 1, n3))


def _mod_spec(d, row_of_tile, kind):
    return pl.BlockSpec((None, 1, d), lambda i, j: (row_of_tile(i) * 3 + kind, 0, 0))


def _modulated(x, nw_ref, scale_ref, shift_ref):
    y = _rms(x, nw_ref[...])
    return (y * (1.0 + scale_ref[...]) + shift_ref[...]).astype(BF16)


def _inproj_kernel(xn_ref, nw_ref, scale0_ref, shift0_ref, scalen_ref, shiftn_ref, w_ref,
                   o_ref, g_ref, hout_ref, ha_ref, hb_ref, *, gate_tile, gate_col, nchunk):
    i = pl.program_id(0)
    j = pl.program_id(1)

    @pl.when((i == 0) & (j == 0))
    def _():
        ha_ref[...] = _modulated(xn_ref[...], nw_ref, scale0_ref, shift0_ref)

    def step(h_cur, h_nxt):
        rows = h_nxt.shape[0] // nchunk
        rs = pl.ds(pl.multiple_of(jnp.clip(j - 1, 0, nchunk - 1) * rows, rows), rows)
        h_nxt[rs, :] = _modulated(xn_ref[rs, :], nw_ref, scalen_ref, shiftn_ref)
        acc = jnp.dot(h_cur[...], w_ref[...], preferred_element_type=F32)
        o_ref[...] = acc.astype(o_ref.dtype)

        @pl.when(j == 0)
        def _():
            hout_ref[...] = h_cur[...]

        @pl.when(j == gate_tile)
        def _():
            g_ref[...] = acc[:, gate_col:gate_col + V7X_LANES]

    @pl.when(i % 2 == 0)
    def _():
        step(ha_ref, hb_ref)

    @pl.when(i % 2 == 1)
    def _():
        step(hb_ref, ha_ref)


def _row_of_tile(tm, seq, first_row):
    assert seq % tm == 0
    return lambda i: first_row + (i * tm) // seq


def _inproj_call(x2d, nw, mod_rows, w_all, seq, first_row):
    t = _tiles()
    m, d = x2d.shape
    n = PACK_COLS
    tm, tn = min(t["in_tm"], seq), t["in_tn"]
    ni, nj = m // tm, n // tn
    nchunk = max(c for c in (1, 2, 4, 8) if c < nj)
    row_of_tile = _row_of_tile(tm, seq, first_row)
    nxt = lambda i: jnp.minimum(i + 1, ni - 1)
    return pl.pallas_call(
        functools.partial(_inproj_kernel, gate_tile=COL_AG // tn, gate_col=COL_AG % tn, nchunk=nchunk),
        grid=(ni, nj),
        in_specs=[
            pl.BlockSpec((tm, d), lambda i, j: (jnp.where((i == 0) & (j == 0), 0, nxt(i)), 0)),
            pl.BlockSpec((1, d), lambda i, j: (0, 0)),
            _mod_spec(d, lambda i: row_of_tile(0), 1),
            _mod_spec(d, lambda i: row_of_tile(0), 0),
            _mod_spec(d, lambda i: row_of_tile(nxt(i)), 1),
            _mod_spec(d, lambda i: row_of_tile(nxt(i)), 0),
            pl.BlockSpec((d, tn), lambda i, j: (0, j)),
        ],
        out_specs=[pl.BlockSpec((tm, tn), lambda i, j: (i, j)),
                   pl.BlockSpec((tm, V7X_LANES), lambda i, j: (i, 0)),
                   pl.BlockSpec((tm, d), lambda i, j: (i, 0))],
        out_shape=[jax.ShapeDtypeStruct((m, n), BF16), jax.ShapeDtypeStruct((m, V7X_LANES), F32),
                   jax.ShapeDtypeStruct((m, d), BF16)],
        scratch_shapes=[pltpu.VMEM((tm, d), BF16), pltpu.VMEM((tm, d), BF16)],
        compiler_params=pltpu.CompilerParams(
            dimension_semantics=("arbitrary", "arbitrary"), vmem_limit_bytes=t["vmem_big"]),
        name="inproj",
    )(x2d, nw, mod_rows, mod_rows, mod_rows, mod_rows, w_all)


def _mlstm_kernel(q_ref, k_ref, v_ref, g_ref, o_ref, z_ref, bg_ref, nw_ref, y_ref,
                  hfw_ref, c_ref, n_ref, m_ref, *, nblk, cb):
    s = pl.program_id(1)
    is_fwd = s < nblk
    blk = jnp.where(is_fwd, s, 2 * nblk - 1 - s)
    sgn = jnp.where(is_fwd, 1, -1)

    @pl.when((s == 0) | (s == nblk))
    def _():
        c_ref[...] = jnp.zeros_like(c_ref)
        n_ref[...] = jnp.zeros_like(n_ref)
        m_ref[...] = jnp.full_like(m_ref, M_INIT)

    ri = lax.broadcasted_iota(jnp.int32, (CHUNK, CHUNK), 0)
    ci = lax.broadcasted_iota(jnp.int32, (CHUNK, CHUNK), 1)
    tri = (((ri - ci) * sgn) >= 0).astype(F32)
    keep_t = ((ci - ri) * sgn) >= 0
    rw = lax.broadcasted_iota(jnp.int32, (BRANCH_W, BRANCH_W), 0)
    cw = lax.broadcasted_iota(jnp.int32, (BRANCH_W, BRANCH_W), 1)
    eye = (rw == cw).astype(BF16)
    bg = bg_ref[...]
    hsl = [slice(h * A_DH, (h + 1) * A_DH) for h in range(A_HEADS)]
    ct_st = [c_ref[:, hsl[h]] for h in range(A_HEADS)]
    n_st = [n_ref[0:1, hsl[h]] for h in range(A_HEADS)]
    m_st = [m_ref[h:h + 1, 0:1] for h in range(A_HEADS)]
    n_pad = jnp.zeros((2 * V7X_SUBLANES - 1, A_DH), BF16)

    chunks, houts = [], []
    for jj in range(cb):
        lc = jnp.where(is_fwd, jj, cb - 1 - jj)
        chunks.append(lc)
        tsl = pl.ds(pl.multiple_of(lc * CHUNK, CHUNK), CHUNK)
        gt = g_ref[tsl, :] + bg
        gt = jnp.where(is_fwd, gt, pltpu.roll(gt, V7X_LANES - A_HEADS, axis=1))
        logf = _log_sigmoid(gt)
        bc = jnp.dot(tri, logf, preferred_element_type=F32, precision=lax.Precision.HIGHEST)
        bc_t = bc.T
        g_row = jnp.where(is_fwd, bc[CHUNK - 1:CHUNK, :], bc[0:1, :])
        q_all = q_ref[tsl, :].astype(BF16)
        k_all = k_ref[tsl, :].astype(BF16)
        vt_all = lax.dot_general(eye, v_ref[tsl, :].astype(BF16), (((1,), (1,)), ((), ())),
                                 preferred_element_type=F32).astype(BF16)
        kq_pair, vt_bd, qcn_pair = [], [], []
        for p in range(A_HEADS // 2):
            ha, hb = 2 * p, 2 * p + 1
            psl = slice(ha * A_DH, (hb + 1) * A_DH)
            first = lax.broadcasted_iota(jnp.int32, (1, 2 * A_DH), 1) < A_DH
            k_pair, q_pair = k_all[:, psl], q_all[:, psl]
            zk = jnp.zeros_like(k_pair)
            k_bd = jnp.concatenate([jnp.where(first, k_pair, zk), jnp.where(first, zk, k_pair)], axis=0)
            kq_pair.append(lax.dot_general(k_bd, q_pair, (((1,), (1,)), ((), ())),
                                           preferred_element_type=F32))
            zblk = jnp.zeros((A_DH, CHUNK), BF16)
            vt_bd.append(jnp.concatenate([jnp.concatenate([vt_all[hsl[ha], :], zblk], axis=1),
                                          jnp.concatenate([zblk, vt_all[hsl[hb], :]], axis=1)], axis=0))
            ctn_a = jnp.concatenate([ct_st[ha].astype(BF16), n_st[ha].astype(BF16), n_pad], axis=0)
            ctn_b = jnp.concatenate([ct_st[hb].astype(BF16), n_st[hb].astype(BF16), n_pad], axis=0)
            zc = jnp.zeros_like(ctn_a)
            ctn_bd = jnp.concatenate([jnp.concatenate([ctn_a, zc], axis=1),
                                      jnp.concatenate([zc, ctn_b], axis=1)], axis=0)
            qcn_pair.append(lax.dot_general(ctn_bd, q_pair, (((1,), (1,)), ((), ())),
                                            preferred_element_type=F32))
        ctn_h = A_DH + 2 * V7X_SUBLANES

        smats, a_ins, m_ts, kas, a_olds, a_news = [], [], [], [], [], []
        for h in range(A_HEADS):
            fcol = 2 * A_HEADS + h
            b_col = bc[:, fcol:fcol + 1]
            i_col = gt[:, h:h + 1]
            g_tot = g_row[:, fcol:fcol + 1]
            rmat = jnp.where(keep_t, i_col - b_col, -jnp.inf)
            big_m = jnp.maximum(jnp.max(rmat, axis=0, keepdims=True), m_st[h])
            kq = kq_pair[h // 2][(h % 2) * CHUNK:(h % 2 + 1) * CHUNK, :]
            smats.append(kq * jnp.exp(rmat - (big_m - LOG_K_SCALE)))
            a_ins.append(jnp.exp(m_st[h] - big_m))
            m_ts.append(bc_t[fcol:fcol + 1, :] + big_m)

            w_end = g_tot - b_col + i_col
            m_loc = jnp.max(w_end, axis=0, keepdims=True)
            kas.append(k_all[:, hsl[h]].astype(F32) * jnp.exp(w_end - (m_loc - LOG_K_SCALE)))
            m_new = jnp.maximum(g_tot + m_st[h], m_loc)
            a_olds.append(jnp.exp(g_tot + m_st[h] - m_new))
            a_news.append(jnp.exp(m_loc - m_new))
            m_st[h] = m_new

        num_pair = [jnp.dot(vt_bd[p], jnp.concatenate(smats[2 * p:2 * p + 2], axis=0).astype(BF16),
                            preferred_element_type=F32) for p in range(A_HEADS // 2)]
        ct_loc_pair = [jnp.dot(vt_bd[p], jnp.concatenate(kas[2 * p:2 * p + 2], axis=0).astype(BF16),
                               preferred_element_type=F32) for p in range(A_HEADS // 2)]
        hrow = []
        for h in range(A_HEADS):
            p, j = h // 2, h % 2
            qcn = qcn_pair[p][j * ctn_h:(j + 1) * ctn_h, :]
            den = jnp.sum(smats[h], axis=0, keepdims=True) + a_ins[h] * qcn[A_DH:A_DH + 1, :]
            hrow.append((num_pair[p][j * A_DH:(j + 1) * A_DH, :] + a_ins[h] * qcn[:A_DH, :])
                        / jnp.maximum(jnp.abs(den), jnp.exp(-m_ts[h])))
            ct_st[h] = a_olds[h] * ct_st[h] + a_news[h] * ct_loc_pair[p][j * A_DH:(j + 1) * A_DH, :]
            n_st[h] = a_olds[h] * n_st[h] + a_news[h] * jnp.sum(kas[h], axis=0, keepdims=True)
        houts.append(hrow)

    for h in range(A_HEADS):
        c_ref[:, hsl[h]] = ct_st[h]
        n_ref[0:1, hsl[h]] = n_st[h]
        m_ref[h:h + 1, :] = jnp.broadcast_to(m_st[h], (1, V7X_LANES))

    @pl.when(is_fwd)
    def _():
        for jj in range(cb):
            for h in range(A_HEADS):
                hfw_ref[blk * cb + chunks[jj], hsl[h], :] = houts[jj][h]

    @pl.when(jnp.logical_not(is_fwd))
    def _():
        for jj in range(cb):
            tsl = pl.ds(pl.multiple_of(chunks[jj] * CHUNK, CHUNK), CHUNK)
            for h in range(A_HEADS):
                tot = hfw_ref[blk * cb + chunks[jj], hsl[h], :] + houts[jj][h]
                tot = tot * lax.rsqrt(jnp.mean(tot * tot, axis=0, keepdims=True) + EPS)
                yh = tot.T * nw_ref[:, hsl[h]] * _sigmoid(o_ref[tsl, hsl[h]].astype(F32))
                y_ref[tsl, hsl[h]] = (yh * _silu(z_ref[tsl, hsl[h]].astype(F32))).astype(BF16)


def _mlstm_call(p3, g3, bg_pad, nw):
    t = _tiles()
    bsz, seq, _ = p3.shape
    nc = seq // CHUNK
    cb = min(MLSTM_BLOCK_CHUNKS, nc)
    assert nc % cb == 0
    nblk = nc // cb
    rows = cb * CHUNK

    def cur(s):
        return jnp.where(s < nblk, s, 2 * nblk - 1 - s)

    def late(s):
        return jnp.where(s < nblk, nblk - 1, 2 * nblk - 1 - s)

    def blk(col, which):
        return pl.BlockSpec((None, rows, BRANCH_W), lambda b, s: (b, which(s), col // BRANCH_W))

    return pl.pallas_call(
        functools.partial(_mlstm_kernel, nblk=nblk, cb=cb),
        grid=(bsz, 2 * nblk),
        in_specs=[
            blk(COL_AQ, cur), blk(COL_AK, cur), blk(COL_AV, cur),
            pl.BlockSpec((None, rows, V7X_LANES), lambda b, s: (b, cur(s), 0)),
            blk(COL_AO, late), blk(COL_AZ, late),
            pl.BlockSpec((1, V7X_LANES), lambda b, s: (0, 0)),
            pl.BlockSpec((1, BRANCH_W), lambda b, s: (0, 0)),
        ],
        out_specs=pl.BlockSpec((None, rows, BRANCH_W), lambda b, s: (b, late(s), 0)),
        out_shape=jax.ShapeDtypeStruct((bsz, seq, BRANCH_W), BF16),
        scratch_shapes=[
            pltpu.VMEM((nc, BRANCH_W, CHUNK), F32),
            pltpu.VMEM((A_DH, BRANCH_W), F32),
            pltpu.VMEM((V7X_SUBLANES, BRANCH_W), F32),
            pltpu.VMEM((V7X_SUBLANES, V7X_LANES), F32),
        ],
        compiler_params=pltpu.CompilerParams(
            dimension_semantics=("arbitrary", "arbitrary"), vmem_limit_bytes=t["vmem_big"]),
        name="mlstm",
    )(p3, p3, p3, g3, p3, p3, bg_pad, nw)


def _nattn_kernel(q_ref, kc_ref, kx_ref, vc_ref, vx_ref, z_ref, qw_ref, kw_ref, bias_ref, y_ref,
                  kn_ref, vn_ref, *, nblk):
    i = pl.program_id(1)

    def put(slot, k_src, v_src):
        dst = pl.ds(pl.multiple_of(slot * NA_BLOCK_TOK, NA_BLOCK_TOK), NA_BLOCK_TOK)
        for h in range(B_HEADS):
            hs = slice(h * B_DH, (h + 1) * B_DH)
            kn_ref[dst, hs] = _rms(k_src[:, hs].astype(F32), kw_ref[...]).astype(BF16)
        vn_ref[dst, :] = v_src[...].astype(BF16)

    @pl.when(i == 0)
    def _():
        put(0, kc_ref, vc_ref)

    nslot = (i + 1) % 3
    put(nslot, kx_ref, vx_ref)

    @pl.when((i == 0) | (nslot == 0))
    def _():
        mirror = slice(3 * NA_BLOCK_TOK, 4 * NA_BLOCK_TOK)
        kn_ref[mirror, :] = kn_ref[0:NA_BLOCK_TOK, :]
        vn_ref[mirror, :] = vn_ref[0:NA_BLOCK_TOK, :]

    qn = []
    for h in range(B_HEADS):
        hs = slice(h * B_DH, (h + 1) * B_DH)
        qn.append((_rms(q_ref[:, hs].astype(F32), qw_ref[...]) * (B_DH ** -0.5 * LOG2E)).astype(BF16))

    half = NA_BLOCK_TOK // 2
    eye = (lax.broadcasted_iota(jnp.int32, (B_DH, B_DH), 0)
           == lax.broadcasted_iota(jnp.int32, (B_DH, B_DH), 1)).astype(BF16)
    starts = (jnp.where(i == 0, 0, ((i + 2) % 3) * NA_BLOCK_TOK + half), (i % 3) * NA_BLOCK_TOK)
    variants = (jnp.where(i == 0, 1, 0), jnp.where(i == nblk - 1, 2, 0))
    for quad in range(NA_BLOCK_ROWS // NA_QUAD_ROWS):
        qrows = slice(quad * NA_QUAD_TOK, (quad + 1) * NA_QUAD_TOK)
        win = pl.ds(pl.multiple_of(starts[quad], half), NA_WIN_TOK)
        for h in range(B_HEADS):
            hs = slice(h * B_DH, (h + 1) * B_DH)
            sc = lax.dot_general(kn_ref[win, hs], qn[h][qrows, :], (((1,), (1,)), ((), ())),
                                 preferred_element_type=F32)
            sc = sc + bias_ref[variants[quad], h]
            m = jnp.max(sc, axis=0, keepdims=True)
            p = jnp.exp2(sc - m)
            l = jnp.sum(p, axis=0, keepdims=True)
            vt = lax.dot_general(eye, vn_ref[win, hs], (((1,), (1,)), ((), ())),
                                 preferred_element_type=F32).astype(BF16)
            o = (jnp.dot(vt, p.astype(BF16), preferred_element_type=F32) / l).T
            y_ref[qrows, hs] = (o * _silu(z_ref[qrows, hs].astype(F32))).astype(BF16)


def _nattn_call(p3, qw, kw, bias_quads):
    t = _tiles()
    bsz, seq, _ = p3.shape
    rows = seq // GRID_W
    assert rows >= WIN_ROWS and rows % NA_BLOCK_ROWS == 0
    nblk = rows // NA_BLOCK_ROWS

    def blk(col, shift):
        cb = col // BRANCH_W
        return pl.BlockSpec((None, NA_BLOCK_TOK, BRANCH_W),
                            lambda b, i: (b, jnp.clip(i + shift, 0, nblk - 1), cb))

    return pl.pallas_call(
        functools.partial(_nattn_kernel, nblk=nblk),
        grid=(bsz, nblk),
        in_specs=[
            blk(COL_BQ, 0),
            blk(COL_BK, 0), blk(COL_BK, 1),
            blk(COL_BV, 0), blk(COL_BV, 1),
            blk(COL_BZ, 0),
            pl.BlockSpec((1, B_DH), lambda b, i: (0, 0)),
            pl.BlockSpec((1, B_DH), lambda b, i: (0, 0)),
            pl.BlockSpec(bias_quads.shape, lambda b, i: (0, 0, 0, 0)),
        ],
        out_specs=pl.BlockSpec((None, NA_BLOCK_TOK, BRANCH_W), lambda b, i: (b, i, 0)),
        out_shape=jax.ShapeDtypeStruct((bsz, seq, BRANCH_W), BF16),
        scratch_shapes=[
            pltpu.VMEM((NA_RING_SLOTS * NA_BLOCK_TOK, BRANCH_W), BF16),
            pltpu.VMEM((NA_RING_SLOTS * NA_BLOCK_TOK, BRANCH_W), BF16),
        ],
        compiler_params=pltpu.CompilerParams(
            dimension_semantics=("arbitrary", "arbitrary"), vmem_limit_bytes=t["vmem_big"]),
        name="nattn",
    )(p3, p3, p3, p3, p3, p3, qw, kw, bias_quads)


def _skew(w, nq, first, ncol):
    width = w.shape[-1]
    assert first - (nq - 1) >= 0 and first + ncol <= width - 1
    lead = w.shape[:-1]
    flat = jnp.tile(w, (1,) * len(lead) + (nq,))[..., :nq * (width - 1)]
    return flat.reshape(lead + (nq, width - 1))[..., first:first + ncol]


def _nattn_bias(rpb):
    heads = rpb.shape[0]
    reach = GRID_W - WIN_COLS
    w = jnp.pad(rpb.astype(F32), ((0, 0), (0, 0), (reach, reach + 1)))
    tbl = _skew(w, GRID_W, GRID_W - 1, GRID_W)
    qc = np.arange(GRID_W)[:, None]
    kc = np.arange(GRID_W)[None, :]
    qs = np.clip(qc - WIN_COLS // 2, 0, GRID_W - WIN_COLS)
    ok = (kc >= qs) & (kc < qs + WIN_COLS)
    tbl = jnp.where(ok[None, None], tbl, -jnp.inf)
    per_d = [jnp.concatenate([tbl[:, d + j] for j in range(WIN_ROWS)], axis=-1) for d in range(WIN_ROWS)]
    fill = lambda n: jnp.full((heads, GRID_W, n), -jnp.inf, F32)

    def quad(ds, shifts):
        rows = [jnp.concatenate([fill(GRID_W * sh), per_d[d], fill(NA_WIN_TOK - NA_BLOCK_TOK - GRID_W * sh)],
                                axis=-1) for d, sh in zip(ds, shifts)]
        return jnp.concatenate(rows, axis=1)

    mid = WIN_ROWS - 1 - WIN_ROWS // 2
    interior = quad((mid,) * NA_QUAD_ROWS, range(NA_QUAD_ROWS))
    top = quad(range(WIN_ROWS - 1, WIN_ROWS - 1 - NA_QUAD_ROWS, -1), (0,) * NA_QUAD_ROWS)
    bottom = quad(range(NA_QUAD_ROWS - 1, -1, -1), (0,) * NA_QUAD_ROWS)
    return jnp.swapaxes(jnp.stack([interior, top, bottom]), 2, 3) * LOG2E


def _swa_kernel(q_ref, kp_ref, kc_ref, kx_ref, vp_ref, vc_ref, vx_ref, z_ref, qw_ref, kw_ref,
                sink_ref, bias_ref, y_ref, *, nsteps, nq):
    n = pl.program_id(1)
    band = 3 * C_BLOCK
    lane = lax.broadcasted_iota(jnp.int32, (1, V7X_LANES), 1)
    lo = lane < C_DH

    def pair_rms(x, w):
        x2 = x * x
        s_lo = jnp.sum(jnp.where(lo, x2, 0.0), axis=-1, keepdims=True)
        s_hi = jnp.sum(jnp.where(lo, 0.0, x2), axis=-1, keepdims=True)
        r = lax.rsqrt(jnp.where(lo, s_lo, s_hi) * (1.0 / C_DH) + EPS)
        return x * r * w

    kn = pair_rms(jnp.concatenate([kp_ref[...], kc_ref[...], kx_ref[...]], axis=0).astype(F32), kw_ref[...])
    vc = jnp.concatenate([vp_ref[...], vc_ref[...], vx_ref[...]], axis=0).astype(F32)
    kn_sw = pltpu.roll(kn, C_DH, axis=1)
    vc_sw = pltpu.roll(vc, C_DH, axis=1)
    stacks = []
    for g in range(C_KV_HEADS):
        k_lo, k_hi = (kn, kn_sw) if g == 0 else (kn_sw, kn)
        v_lo, v_hi = (vc, vc_sw) if g == 0 else (vc_sw, vc)
        stacks.append((jnp.where(lo, k_lo, 0.0).astype(BF16), jnp.where(lo, 0.0, k_hi).astype(BF16),
                       jnp.where(lo, v_lo, 0.0).astype(BF16), jnp.where(lo, 0.0, v_hi).astype(BF16)))

    ki = lax.broadcasted_iota(jnp.int32, (2 * band, 1), 0)
    ki = jnp.where(ki >= band, ki - band, ki)
    first_slab = lax.broadcasted_iota(jnp.int32, (1, 2 * C_BLOCK), 1) < C_BLOCK
    even_rows = lax.broadcasted_iota(jnp.int32, (V7X_LANES, 1), 0) < C_DH
    eye = (lax.broadcasted_iota(jnp.int32, (V7X_LANES, V7X_LANES), 0)
           == lax.broadcasted_iota(jnp.int32, (V7X_LANES, V7X_LANES), 1)).astype(BF16)
    for qb in range(nq):
        qrows = slice(qb * C_BLOCK, (qb + 1) * C_BLOCK)
        krows = slice((nq + qb - 1) * C_BLOCK, (nq + qb + 2) * C_BLOCK)
        has_prev = (n > 0) if qb == 0 else True
        has_next = (n < nsteps - 1) if qb == nq - 1 else True
        valid = ((ki >= C_BLOCK) | has_prev) & ((ki < 2 * C_BLOCK) | has_next)
        for g in range(C_KV_HEADS):
            k_even, k_odd, v_even, v_odd = stacks[g]
            kk = jnp.concatenate([k_even[krows], k_odd[krows]], axis=0)
            vv = jnp.concatenate([v_even[krows], v_odd[krows]], axis=0)
            slabs = [slice(p * V7X_LANES, (p + 1) * V7X_LANES) for p in (2 * g, 2 * g + 1)]
            qn = jnp.concatenate(
                [(pair_rms(q_ref[qrows, sl].astype(F32), qw_ref[...]) * (C_DH ** -0.5 * LOG2E)).astype(BF16)
                 for sl in slabs], axis=0)
            sc = lax.dot_general(kk, qn, (((1,), (1,)), ((), ())), preferred_element_type=F32)
            sc = jnp.where(valid, sc + bias_ref[g], -jnp.inf)
            probs, denoms = [], []
            for e in range(2):
                se = sc[e * band:(e + 1) * band, :]
                h0 = C_GROUP * g + e
                sink = jnp.where(first_slab, sink_ref[h0:h0 + 1, 0:1], sink_ref[h0 + 2:h0 + 3, 0:1])
                m = jnp.maximum(jnp.max(se, axis=0, keepdims=True), sink)
                pe = jnp.exp2(se - m)
                probs.append(pe.astype(BF16))
                denoms.append(jnp.sum(pe, axis=0, keepdims=True) + jnp.exp2(sink - m))
            vvt = lax.dot_general(eye, vv, (((1,), (1,)), ((), ())), preferred_element_type=F32).astype(BF16)
            ot = jnp.dot(vvt, jnp.concatenate(probs, axis=0), preferred_element_type=F32)
            o = (ot / jnp.where(even_rows, denoms[0], denoms[1])).T
            for j, sl in enumerate(slabs):
                rows = slice(j * C_BLOCK, (j + 1) * C_BLOCK)
                y_ref[qrows, sl] = (o[rows] * _silu(z_ref[qrows, sl].astype(F32))).astype(BF16)


def _swa_call(p3, qw, kw, sink_rows, bias):
    t = _tiles()
    bsz, seq, _ = p3.shape
    nq = SWA_STEP_BLOCKS
    rows = nq * C_BLOCK
    assert seq % rows == 0
    nsteps = seq // rows

    def wide(col):
        return pl.BlockSpec((None, rows, BRANCH_W), lambda b, n: (b, n, col // BRANCH_W))

    def kv(col, shift):
        return pl.BlockSpec((None, rows, C_KV_W),
                            lambda b, n: (b, jnp.clip(n + shift, 0, nsteps - 1), col // C_KV_W))

    return pl.pallas_call(
        functools.partial(_swa_kernel, nsteps=nsteps, nq=nq),
        grid=(bsz, nsteps),
        in_specs=[
            wide(COL_CQ),
            kv(COL_CK, -1), kv(COL_CK, 0), kv(COL_CK, 1),
            kv(COL_CV, -1), kv(COL_CV, 0), kv(COL_CV, 1),
            wide(COL_CZ),
            pl.BlockSpec((1, V7X_LANES), lambda b, n: (0, 0)),
            pl.BlockSpec((1, V7X_LANES), lambda b, n: (0, 0)),
            pl.BlockSpec(sink_rows.shape, lambda b, n: (0, 0)),
            pl.BlockSpec(bias.shape, lambda b, n: (0, 0, 0)),
        ],
        out_specs=pl.BlockSpec((None, rows, BRANCH_W), lambda b, n: (b, n, 0)),
        out_shape=jax.ShapeDtypeStruct((bsz, seq, BRANCH_W), BF16),
        compiler_params=pltpu.CompilerParams(
            dimension_semantics=("arbitrary", "arbitrary"), vmem_limit_bytes=t["vmem_small"]),
        name="swa",
    )(p3, p3, p3, p3, p3, p3, p3, p3, qw, kw, sink_rows, bias)


def _t5_bucket_rel():
    rel = np.arange(-(2 * C_BLOCK - 1), 2 * C_BLOCK)
    half = N_BUCKETS // 2
    max_exact = half // 2
    n = np.abs(rel)
    nf = np.maximum(n, 1).astype(np.float32)
    scale = np.float32(math.log(MAX_DIST / max_exact))
    large = max_exact + (np.log(nf / np.float32(max_exact)) / scale
                         * np.float32(half - max_exact)).astype(np.int32)
    large = np.minimum(large, half - 1)
    bucket = np.where(rel > 0, half, 0) + np.where(n < max_exact, n, large)
    return bucket, n <= C_WINDOW


def _swa_bias(rel_bias):
    bucket, in_window = _t5_bucket_rel()
    per_rel = jnp.where(in_window[:, None], rel_bias.astype(F32)[bucket], -jnp.inf)
    w = jnp.pad(per_rel.T, ((0, 0), (0, 1)))
    tbl = _skew(w, C_BLOCK, C_BLOCK - 1, 3 * C_BLOCK)
    slabs = jnp.concatenate([tbl[0::2], tbl[1::2]], axis=-1)
    return jnp.swapaxes(slabs.reshape(C_KV_HEADS, 2 * C_BLOCK, 2 * 3 * C_BLOCK), 1, 2) * LOG2E


def _conv_kernel(a_ref, g_ref, ap_ref, gp_ref, ax_ref, gx_ref, z_ref, cw_ref, cb_ref, lw_ref, lb_ref,
                 y_ref, ext_ref, sh_ref, *, ntile):
    i = pl.program_id(1)
    tt = a_ref.shape[0]
    glu = lambda a, g: a[...].astype(F32) * _sigmoid(g[...].astype(F32))
    prev = jnp.where(i > 0, glu(ap_ref, gp_ref), 0.0)
    nxt = jnp.where(i < ntile - 1, glu(ax_ref, gx_ref), 0.0)
    ext_ref[0:CONV_HALO, :] = prev
    ext_ref[CONV_HALO:CONV_HALO + tt, :] = glu(a_ref, g_ref)
    ext_ref[CONV_HALO + tt:2 * CONV_HALO + tt, :] = nxt
    span = sh_ref.shape[1]
    for r in range(1, V7X_SUBLANES):
        sh_ref[r - 1] = ext_ref[pl.ds(r, span), :]
    acc = None
    for w in range(CONV_W):
        a, r = divmod(CONV_HALO - CONV_W // 2 + w, V7X_SUBLANES)
        lo = a * V7X_SUBLANES
        src = ext_ref[lo:lo + tt, :] if r == 0 else sh_ref[r - 1, lo:lo + tt, :]
        term = src * cw_ref[w:w + 1, :]
        acc = term if acc is None else acc + term
    u = acc + cb_ref[...]
    mu = jnp.mean(u, axis=-1, keepdims=True)
    var = jnp.mean(jnp.square(u - mu), axis=-1, keepdims=True)
    u = (u - mu) * lax.rsqrt(var + EPS) * lw_ref[...] + lb_ref[...]
    y_ref[...] = (_silu(u) * _silu(z_ref[...].astype(F32))).astype(BF16)


def _conv_call(p3, cw_pad, cb, lw, lb):
    t = _tiles()
    bsz, seq, _ = p3.shape
    tt = CONV_TILE
    ntile = seq // tt
    per = tt // CONV_HALO
    nhalo = seq // CONV_HALO
    span = tt + (2 * CONV_HALO - 1) // V7X_SUBLANES * V7X_SUBLANES

    def cur(col):
        return pl.BlockSpec((None, tt, BRANCH_W), lambda b, i: (b, i, col // BRANCH_W))

    def halo(col, after):
        def idx(b, i):
            r = (i + 1) * per if after else i * per - 1
            return (b, jnp.clip(r, 0, nhalo - 1), col // BRANCH_W)
        return pl.BlockSpec((None, CONV_HALO, BRANCH_W), idx)

    vec = pl.BlockSpec((1, BRANCH_W), lambda b, i: (0, 0))
    return pl.pallas_call(
        functools.partial(_conv_kernel, ntile=ntile),
        grid=(bsz, ntile),
        in_specs=[
            cur(COL_DA), cur(COL_DG),
            halo(COL_DA, False), halo(COL_DG, False), halo(COL_DA, True), halo(COL_DG, True),
            cur(COL_DZ),
            pl.BlockSpec(cw_pad.shape, lambda b, i: (0, 0)),
            vec, vec, vec,
        ],
        out_specs=pl.BlockSpec((None, tt, BRANCH_W), lambda b, i: (b, i, 0)),
        out_shape=jax.ShapeDtypeStruct((bsz, seq, BRANCH_W), BF16),
        scratch_shapes=[
            pltpu.VMEM((tt + 2 * CONV_HALO, BRANCH_W), F32),
            pltpu.VMEM((V7X_SUBLANES - 1, span, BRANCH_W), F32),
        ],
        compiler_params=pltpu.CompilerParams(
            dimension_semantics=("arbitrary", "arbitrary"), vmem_limit_bytes=t["vmem_small"]),
        name="conv",
    )(p3, p3, p3, p3, p3, p3, p3, cw_pad, cb, lw, lb)


def _merge_kernel(h_ref, ya_ref, yb_ref, yc_ref, yd_ref, wma_ref, wmb_ref, wmc_ref, wmd_ref, wb_ref, o_ref):
    h = h_ref[...]
    merged = None
    branches = ((ya_ref, wma_ref), (yb_ref, wmb_ref), (yc_ref, wmc_ref), (yd_ref, wmd_ref))
    for b, (y_ref, wm_ref) in enumerate(branches):
        gate = _sigmoid(jnp.dot(h, wm_ref[...], preferred_element_type=F32))
        term = gate * jnp.dot(y_ref[...], wb_ref[b], preferred_element_type=F32)
        merged = term if merged is None else merged + term
    o_ref[...] = merged.astype(BF16)


def _merge_call(h2d, ys, w_all, w_branch, seq):
    t = _tiles()
    m, d = h2d.shape
    tm, tn = min(t["mg_tm"], seq), t["mg_tn"]
    yspec = pl.BlockSpec((tm, BRANCH_W), lambda i, j: (i, 0))
    return pl.pallas_call(
        _merge_kernel,
        grid=(m // tm, d // tn),
        in_specs=[
            pl.BlockSpec((tm, d), lambda i, j: (i, 0)),
            yspec, yspec, yspec, yspec,
            *[pl.BlockSpec((d, tn), functools.partial(lambda i, j, b: (0, (PACK_COLS + b * d) // tn + j), b=b))
              for b in range(N_BRANCH)],
            pl.BlockSpec((N_BRANCH, BRANCH_W, tn), lambda i, j: (0, 0, j)),
        ],
        out_specs=pl.BlockSpec((tm, tn), lambda i, j: (i, j)),
        out_shape=jax.ShapeDtypeStruct((m, d), BF16),
        compiler_params=pltpu.CompilerParams(
            dimension_semantics=("arbitrary", "arbitrary"), vmem_limit_bytes=t["vmem_big"]),
        name="merge",
    )(h2d, *ys, *([w_all] * N_BRANCH), w_branch)


def _outproj_kernel(x_ref, mg_ref, gate_ref, wo_ref, o_ref):
    o_ref[...] = x_ref[...] + gate_ref[...] * jnp.dot(mg_ref[...], wo_ref[...], preferred_element_type=F32)


def _outproj_call(x2d, merged, mod_rows, w_out, seq, first_row):
    t = _tiles()
    m, d = x2d.shape
    tm, tn = min(t["op_tm"], seq), t["op_tn"]
    row_of_tile = _row_of_tile(tm, seq, first_row)
    return pl.pallas_call(
        _outproj_kernel,
        grid=(m // tm, d // tn),
        in_specs=[
            pl.BlockSpec((tm, tn), lambda i, j: (i, j)),
            pl.BlockSpec((tm, d), lambda i, j: (i, 0)),
            pl.BlockSpec((None, 1, tn), lambda i, j: (row_of_tile(i) * 3 + 2, 0, j)),
            pl.BlockSpec((d, tn), lambda i, j: (0, j)),
        ],
        out_specs=pl.BlockSpec((tm, tn), lambda i, j: (i, j)),
        out_shape=jax.ShapeDtypeStruct((m, d), F32),
        compiler_params=pltpu.CompilerParams(
            dimension_semantics=("arbitrary", "arbitrary"), vmem_limit_bytes=t["vmem_big"]),
        name="outproj",
    )(x2d, merged, mod_rows, w_out)


REPACK_TILE = 256
GATE_SHIFT = A_GATES


def _repack_plan(d):
    sizes = (512, 512, 512, 512, 512, A_GATES, 512, 512, 512, 512, 512, C_KV_W, C_KV_W, 512, 1024, 512,
             N_BRANCH * d)
    names = ("aq", "ak", "av", "ao", "az", "ag", "bq", "bk", "bv", "bz", "cq", "ck", "cv", "cz", "dglu", "dz", "mg")
    src = dict(zip(names, np.concatenate([[0], np.cumsum(sizes)])[:-1]))
    groups = [
        (COL_AQ, src["aq"], 5 * BRANCH_W), (COL_BQ, src["bq"], 4 * BRANCH_W), (COL_DA, src["dglu"], 3 * BRANCH_W),
        (COL_CQ, src["cq"], BRANCH_W), (COL_CZ, src["cz"], BRANCH_W), (COL_CK, src["ck"], 2 * C_KV_W),
        (COL_AG, src["ag"], REPACK_TILE), (PACK_COLS, src["mg"], N_BRANCH * d)]
    ntile = (PACK_COLS + N_BRANCH * d) // REPACK_TILE
    a_idx, b_idx, mode = np.zeros(ntile, np.int32), np.zeros(ntile, np.int32), np.zeros(ntile, np.int32)
    for dst, s0, width in groups:
        for k in range(width // REPACK_TILE):
            t = dst // REPACK_TILE + k
            start = int(s0) + k * REPACK_TILE
            off = start % REPACK_TILE
            assert off in (0, GATE_SHIFT) and dst % REPACK_TILE == 0
            a_idx[t] = start // REPACK_TILE
            b_idx[t] = (start - off + REPACK_TILE) // V7X_LANES
            mode[t] = 2 if dst == COL_AG else (1 if off else 0)
    return jnp.asarray(a_idx), jnp.asarray(b_idx), jnp.asarray(mode)


def _repack_kernel(a_idx, b_idx, mode, a_ref, b_ref, o_ref):
    t = pl.program_id(0)
    a = a_ref[...]
    lane = lax.broadcasted_iota(jnp.int32, (1, REPACK_TILE), 1)
    shifted = jnp.concatenate([a[:, GATE_SHIFT:], b_ref[:, :GATE_SHIFT]], axis=1)
    gates = jnp.where(lane < A_GATES, a, 0.0)
    m = mode[t]
    o_ref[...] = jnp.where(m == 1, shifted, jnp.where(m == 2, gates, a)).astype(BF16)


def _repack_call(w_in, layer):
    t = _tiles()
    _, d, _ = w_in.shape
    a_idx, b_idx, mode = _repack_plan(d)
    ntile = a_idx.shape[0]
    return pl.pallas_call(
        _repack_kernel,
        grid_spec=pltpu.PrefetchScalarGridSpec(
            num_scalar_prefetch=3,
            grid=(ntile,),
            in_specs=[
                pl.BlockSpec((None, d, REPACK_TILE), lambda i, a, b, m: (layer, 0, a[i])),
                pl.BlockSpec((None, d, V7X_LANES), lambda i, a, b, m: (layer, 0, b[i])),
            ],
            out_specs=pl.BlockSpec((d, REPACK_TILE), lambda i, a, b, m: (0, i)),
        ),
        out_shape=jax.ShapeDtypeStruct((d, ntile * REPACK_TILE), BF16),
        compiler_params=pltpu.CompilerParams(
            dimension_semantics=("arbitrary",), vmem_limit_bytes=t["vmem_small"]),
        name="repack",
    )(a_idx, b_idx, mode, w_in, w_in)


def kernel(x_prompt, x_sample, c_prompt, c_sample, rel_bias, norm_w, w_ada, b_ada, w_in, b_gate, mlstm_norm_w, na_q_norm, na_k_norm, na_rpb, swa_q_norm, swa_k_norm, swa_sink, conv_w, conv_b, conv_ln_w, conv_ln_b, w_branch, w_out):
    depth, d = norm_w.shape
    groups = ((x_prompt, 0), (x_sample, c_prompt.shape[0]))
    n_cond = c_prompt.shape[0] + c_sample.shape[0]
    cond_rows = -(-n_cond // V7X_SUBLANES) * V7X_SUBLANES
    c_all = jnp.concatenate([c_prompt, c_sample, jnp.zeros((cond_rows - n_cond, d), F32)], axis=0)
    mod = _ada_call(c_all, w_ada, b_ada)
    mod_rows = mod.reshape(depth * cond_rows * 3, 1, d)

    swa_bias = _swa_bias(rel_bias)
    w_in_bf16 = w_in.astype(BF16)
    layers = []
    for l in range(depth):
        layers.append(dict(
            w_all=_repack_call(w_in_bf16, l),
            w_branch=w_branch[l].astype(BF16), w_out=w_out[l].astype(BF16),
            nw=norm_w[l].reshape(1, d),
            bg=jnp.pad(b_gate[l], (0, V7X_LANES - A_GATES)).reshape(1, V7X_LANES),
            mnw=mlstm_norm_w[l].reshape(1, BRANCH_W),
            na_qw=na_q_norm[l].reshape(1, B_DH), na_kw=na_k_norm[l].reshape(1, B_DH),
            na_bias=_nattn_bias(na_rpb[l]),
            swa_qw=jnp.tile(swa_q_norm[l], 2).reshape(1, V7X_LANES),
            swa_kw=jnp.tile(swa_k_norm[l], 2).reshape(1, V7X_LANES),
            sink=jnp.broadcast_to(swa_sink[l].reshape(C_HEADS, 1) * LOG2E, (C_HEADS, V7X_LANES)),
            cw=jnp.pad(conv_w[l], ((0, 1), (0, 0))),
            cb=conv_b[l].reshape(1, BRANCH_W), lw=conv_ln_w[l].reshape(1, BRANCH_W),
            lb=conv_ln_b[l].reshape(1, BRANCH_W),
        ))

    outs = []
    for x, cond_off in groups:
        bsz, seq, _ = x.shape
        x2d = x.reshape(bsz * seq, d)
        for l, lw in enumerate(layers):
            first_row = l * cond_rows + cond_off
            p, gates, h2d = _inproj_call(x2d, lw["nw"], mod_rows, lw["w_all"], seq, first_row)
            p3 = p.reshape(bsz, seq, PACK_COLS)
            ya = _mlstm_call(p3, gates.reshape(bsz, seq, V7X_LANES), lw["bg"], lw["mnw"])
            yb = _nattn_call(p3, lw["na_qw"], lw["na_kw"], lw["na_bias"])
            yc = _swa_call(p3, lw["swa_qw"], lw["swa_kw"], lw["sink"], swa_bias)
            yd = _conv_call(p3, lw["cw"], lw["cb"], lw["lw"], lw["lb"])
            ys = [y.reshape(bsz * seq, BRANCH_W) for y in (ya, yb, yc, yd)]
            merged = _merge_call(h2d, ys, lw["w_all"], lw["w_branch"], seq)
            x2d = _outproj_call(x2d, merged, mod_rows, lw["w_out"], seq, first_row)
        outs.append(x2d.reshape(bsz, seq, d))
    return tuple(outs)
```

```python
import functools
import math

import numpy as np
import jax
import jax.numpy as jnp
from jax import lax
from jax.experimental import pallas as pl
from jax.experimental.pallas import tpu as pltpu

F32 = jnp.float32
BF16 = jnp.bfloat16

V7X_VMEM_BYTES = 64 * 1024 * 1024
V7X_LANES = 128
V7X_SUBLANES = 8
MIB = 1024 * 1024

EPS = 1e-6
N_BRANCH = 4
BRANCH_W = 512
GRID_W = 64
A_HEADS = 4
A_DH = BRANCH_W // A_HEADS
A_GATES = 4 * A_HEADS
CHUNK = 128
MLSTM_BLOCK_CHUNKS = 16
M_INIT = -1e30
LOG_K_SCALE = -0.5 * math.log(A_DH)
LOG2E = math.log2(math.e)
B_HEADS = 4
B_DH = BRANCH_W // B_HEADS
WIN_ROWS = 8
WIN_COLS = 16
NA_BLOCK_ROWS = 8
NA_BLOCK_TOK = NA_BLOCK_ROWS * GRID_W
NA_QUAD_ROWS = 4
NA_QUAD_TOK = NA_QUAD_ROWS * GRID_W
NA_WIN_TOK = (WIN_ROWS + NA_QUAD_ROWS) * GRID_W
NA_RING_SLOTS = 4
C_HEADS = 8
C_KV_HEADS = 2
C_GROUP = C_HEADS // C_KV_HEADS
C_DH = BRANCH_W // C_HEADS
C_KV_W = C_KV_HEADS * C_DH
C_WINDOW = 128
C_BLOCK = 128
SWA_STEP_BLOCKS = 4
N_BUCKETS = 32
MAX_DIST = 128
CONV_W = 31
CONV_HALO = 16
CONV_TILE = 1024

COL_AQ, COL_AK, COL_AV, COL_AO, COL_AZ = 0, 512, 1024, 1536, 2048
COL_BQ, COL_BK, COL_BV, COL_BZ = 2560, 3072, 3584, 4096
COL_DA, COL_DG, COL_DZ = 4608, 5120, 5632
COL_CQ, COL_CZ = 6144, 6656
COL_CK, COL_CV, COL_AG = 7168, 7296, 7424
PACK_COLS = 7680


def _tiles():
    return dict(
        ada_tn=1536,
        in_tm=1024, in_tn=1536,
        mg_tm=1024, mg_tn=512,
        op_tm=2048, op_tn=512,
        vmem_small=32 * MIB,
        vmem_big=V7X_VMEM_BYTES - 4 * MIB,
    )


def _sigmoid(x):
    return jax.nn.sigmoid(x)


def _silu(x):
    return x * jax.nn.sigmoid(x)


def _log_sigmoid(x):
    return jnp.minimum(x, 0.0) - jnp.log1p(jnp.exp(-jnp.abs(x)))


def _rms(x, w):
    r = lax.rsqrt(jnp.mean(x * x, axis=-1, keepdims=True) + EPS)
    return x * r * w


def _ada_kernel(c_ref, w_ref, b_ref, o_ref):
    c = c_ref[...]
    sc = _silu(c).astype(BF16)
    o_ref[...] = jnp.dot(sc, w_ref[...].astype(BF16), preferred_element_type=F32) + b_ref[...]


def _ada_call(c_all, w_ada, b_ada):
    t = _tiles()
    depth, d, n3 = w_ada.shape
    rows = c_all.shape[0]
    tn = t["ada_tn"]
    return pl.pallas_call(
        _ada_kernel,
        grid=(depth, n3 // tn),
        in_specs=[
            pl.BlockSpec((rows, d), lambda l, j: (0, 0)),
            pl.BlockSpec((None, d, tn), lambda l, j: (l, 0, j)),
            pl.BlockSpec((None, 1, tn), lambda l, j: (l, 0, j)),
        ],
        out_specs=pl.BlockSpec((None, rows, tn), lambda l, j: (l, 0, j)),
        out_shape=jax.ShapeDtypeStruct((depth, rows, n3), F32),
        compiler_params=pltpu.CompilerParams(
            dimension_semantics=("arbitrary", "arbitrary"), vmem_limit_bytes=t["vmem_small"]),
        name="ada",
    )(c_all, w_ada, b_ada.reshape(depth, 1, n3))


def _mod_spec(d, row_of_tile, kind):
    return pl.BlockSpec((None, 1, d), lambda i, j: (row_of_tile(i) * 3 + kind, 0, 0))


def _modulated(x, nw_ref, scale_ref, shift_ref):
    y = _rms(x, nw_ref[...])
    return (y * (1.0 + scale_ref[...]) + shift_ref[...]).astype(BF16)


def _inproj_kernel(xn_ref, nw_ref, scale0_ref, shift0_ref, scalen_ref, shiftn_ref, w_ref,
                   o_ref, g_ref, hout_ref, ha_ref, hb_ref, *, gate_tile, gate_col, nchunk):
    i = pl.program_id(0)
    j = pl.program_id(1)

    @pl.when((i == 0) & (j == 0))
    def _():
        ha_ref[...] = _modulated(xn_ref[...], nw_ref, scale0_ref, shift0_ref)

    def step(h_cur, h_nxt):
        rows = h_nxt.shape[0] // nchunk
        rs = pl.ds(pl.multiple_of(jnp.clip(j - 1, 0, nchunk - 1) * rows, rows), rows)
        h_nxt[rs, :] = _modulated(xn_ref[rs, :], nw_ref, scalen_ref, shiftn_ref)
        acc = jnp.dot(h_cur[...], w_ref[...], preferred_element_type=F32)
        o_ref[...] = acc.astype(o_ref.dtype)

        @pl.when(j == 0)
        def _():
            hout_ref[...] = h_cur[...]

        @pl.when(j == gate_tile)
        def _():
            g_ref[...] = acc[:, gate_col:gate_col + V7X_LANES]

    @pl.when(i % 2 == 0)
    def _():
        step(ha_ref, hb_ref)

    @pl.when(i % 2 == 1)
    def _():
        step(hb_ref, ha_ref)


def _row_of_tile(tm, seq, first_row):
    assert seq % tm == 0
    return lambda i: first_row + (i * tm) // seq


def _inproj_call(x2d, nw, mod_rows, w_all, seq, first_row):
    t = _tiles()
    m, d = x2d.shape
    n = PACK_COLS
    tm, tn = min(t["in_tm"], seq), t["in_tn"]
    ni, nj = m // tm, n // tn
    nchunk = max(c for c in (1, 2, 4, 8) if c < nj)
    row_of_tile = _row_of_tile(tm, seq, first_row)
    nxt = lambda i: jnp.minimum(i + 1, ni - 1)
    return pl.pallas_call(
        functools.partial(_inproj_kernel, gate_tile=COL_AG // tn, gate_col=COL_AG % tn, nchunk=nchunk),
        grid=(ni, nj),
        in_specs=[
            pl.BlockSpec((tm, d), lambda i, j: (jnp.where((i == 0) & (j == 0), 0, nxt(i)), 0)),
            pl.BlockSpec((1, d), lambda i, j: (0, 0)),
            _mod_spec(d, lambda i: row_of_tile(0), 1),
            _mod_spec(d, lambda i: row_of_tile(0), 0),
            _mod_spec(d, lambda i: row_of_tile(nxt(i)), 1),
            _mod_spec(d, lambda i: row_of_tile(nxt(i)), 0),
            pl.BlockSpec((d, tn), lambda i, j: (0, j)),
        ],
        out_specs=[pl.BlockSpec((tm, tn), lambda i, j: (i, j)),
                   pl.BlockSpec((tm, V7X_LANES), lambda i, j: (i, 0)),
                   pl.BlockSpec((tm, d), lambda i, j: (i, 0))],
        out_shape=[jax.ShapeDtypeStruct((m, n), BF16), jax.ShapeDtypeStruct((m, V7X_LANES), F32),
                   jax.ShapeDtypeStruct((m, d), BF16)],
        scratch_shapes=[pltpu.VMEM((tm, d), BF16), pltpu.VMEM((tm, d), BF16)],
        compiler_params=pltpu.CompilerParams(
            dimension_semantics=("arbitrary", "arbitrary"), vmem_limit_bytes=t["vmem_big"]),
        name="inproj",
    )(x2d, nw, mod_rows, mod_rows, mod_rows, mod_rows, w_all)


def _mlstm_kernel(q_ref, k_ref, v_ref, g_ref, o_ref, z_ref, bg_ref, nw_ref, y_ref,
                  hfw_ref, c_ref, n_ref, m_ref, *, nblk, cb):
    s = pl.program_id(1)
    is_fwd = s < nblk
    blk = jnp.where(is_fwd, s, 2 * nblk - 1 - s)
    sgn = jnp.where(is_fwd, 1, -1)

    @pl.when((s == 0) | (s == nblk))
    def _():
        c_ref[...] = jnp.zeros_like(c_ref)
        n_ref[...] = jnp.zeros_like(n_ref)
        m_ref[...] = jnp.full_like(m_ref, M_INIT)

    ri = lax.broadcasted_iota(jnp.int32, (CHUNK, CHUNK), 0)
    ci = lax.broadcasted_iota(jnp.int32, (CHUNK, CHUNK), 1)
    tri = (((ri - ci) * sgn) >= 0).astype(F32)
    keep_t = ((ci - ri) * sgn) >= 0
    rw = lax.broadcasted_iota(jnp.int32, (BRANCH_W, BRANCH_W), 0)
    cw = lax.broadcasted_iota(jnp.int32, (BRANCH_W, BRANCH_W), 1)
    eye = (rw == cw).astype(BF16)
    bg = bg_ref[...]
    hsl = [slice(h * A_DH, (h + 1) * A_DH) for h in range(A_HEADS)]
    ct_st = [c_ref[:, hsl[h]] for h in range(A_HEADS)]
    n_st = [n_ref[0:1, hsl[h]] for h in range(A_HEADS)]
    m_st = [m_ref[h:h + 1, 0:1] for h in range(A_HEADS)]
    n_pad = jnp.zeros((2 * V7X_SUBLANES - 1, A_DH), BF16)

    chunks, houts = [], []
    for jj in range(cb):
        lc = jnp.where(is_fwd, jj, cb - 1 - jj)
        chunks.append(lc)
        tsl = pl.ds(pl.multiple_of(lc * CHUNK, CHUNK), CHUNK)
        gt = g_ref[tsl, :] + bg
        gt = jnp.where(is_fwd, gt, pltpu.roll(gt, V7X_LANES - A_HEADS, axis=1))
        logf = _log_sigmoid(gt)
        bc = jnp.dot(tri, logf, preferred_element_type=F32, precision=lax.Precision.HIGHEST)
        bc_t = bc.T
        g_row = jnp.where(is_fwd, bc[CHUNK - 1:CHUNK, :], bc[0:1, :])
        q_all = q_ref[tsl, :].astype(BF16)
        k_all = k_ref[tsl, :].astype(BF16)
        vt_all = lax.dot_general(eye, v_ref[tsl, :].astype(BF16), (((1,), (1,)), ((), ())),
                                 preferred_element_type=F32).astype(BF16)
        kq_pair, vt_bd, qcn_pair = [], [], []
        for p in range(A_HEADS // 2):
            ha, hb = 2 * p, 2 * p + 1
            psl = slice(ha * A_DH, (hb + 1) * A_DH)
            first = lax.broadcasted_iota(jnp.int32, (1, 2 * A_DH), 1) < A_DH
            k_pair, q_pair = k_all[:, psl], q_all[:, psl]
            zk = jnp.zeros_like(k_pair)
            k_bd = jnp.concatenate([jnp.where(first, k_pair, zk), jnp.where(first, zk, k_pair)], axis=0)
            kq_pair.append(lax.dot_general(k_bd, q_pair, (((1,), (1,)), ((), ())),
                                           preferred_element_type=F32))
            zblk = jnp.zeros((A_DH, CHUNK), BF16)
            vt_bd.append(jnp.concatenate([jnp.concatenate([vt_all[hsl[ha], :], zblk], axis=1),
                                          jnp.concatenate([zblk, vt_all[hsl[hb], :]], axis=1)], axis=0))
            ctn_a = jnp.concatenate([ct_st[ha].astype(BF16), n_st[ha].astype(BF16), n_pad], axis=0)
            ctn_b = jnp.concatenate([ct_st[hb].astype(BF16), n_st[hb].astype(BF16), n_pad], axis=0)
            zc = jnp.zeros_like(ctn_a)
            ctn_bd = jnp.concatenate([jnp.concatenate([ctn_a, zc], axis=1),
                                      jnp.concatenate([zc, ctn_b], axis=1)], axis=0)
            qcn_pair.append(lax.dot_general(ctn_bd, q_pair, (((1,), (1,)), ((), ())),
                                            preferred_element_type=F32))
        ctn_h = A_DH + 2 * V7X_SUBLANES

        smats, a_ins, m_ts, kas, a_olds, a_news = [], [], [], [], [], []
        for h in range(A_HEADS):
            fcol = 2 * A_HEADS + h
            b_col = bc[:, fcol:fcol + 1]
            i_col = gt[:, h:h + 1]
            g_tot = g_row[:, fcol:fcol + 1]
            rmat = jnp.where(keep_t, i_col - b_col, -jnp.inf)
            big_m = jnp.maximum(jnp.max(rmat, axis=0, keepdims=True), m_st[h])
            kq = kq_pair[h // 2][(h % 2) * CHUNK:(h % 2 + 1) * CHUNK, :]
            smats.append(kq * jnp.exp(rmat - (big_m - LOG_K_SCALE)))
            a_ins.append(jnp.exp(m_st[h] - big_m))
            m_ts.append(bc_t[fcol:fcol + 1, :] + big_m)

            w_end = g_tot - b_col + i_col
            m_loc = jnp.max(w_end, axis=0, keepdims=True)
            kas.append(k_all[:, hsl[h]].astype(F32) * jnp.exp(w_end - (m_loc - LOG_K_SCALE)))
            m_new = jnp.maximum(g_tot + m_st[h], m_loc)
            a_olds.append(jnp.exp(g_tot + m_st[h] - m_new))
            a_news.append(jnp.exp(m_loc - m_new))
            m_st[h] = m_new

        num_pair = [jnp.dot(vt_bd[p], jnp.concatenate(smats[2 * p:2 * p + 2], axis=0).astype(BF16),
                            preferred_element_type=F32) for p in range(A_HEADS // 2)]
        ct_loc_pair = [jnp.dot(vt_bd[p], jnp.concatenate(kas[2 * p:2 * p + 2], axis=0).astype(BF16),
                               preferred_element_type=F32) for p in range(A_HEADS // 2)]
        hrow = []
        for h in range(A_HEADS):
            p, j = h // 2, h % 2
            qcn = qcn_pair[p][j * ctn_h:(j + 1) * ctn_h, :]
            den = jnp.sum(smats[h], axis=0, keepdims=True) + a_ins[h] * qcn[A_DH:A_DH + 1, :]
            hrow.append((num_pair[p][j * A_DH:(j + 1) * A_DH, :] + a_ins[h] * qcn[:A_DH, :])
                        / jnp.maximum(jnp.abs(den), jnp.exp(-m_ts[h])))
            ct_st[h] = a_olds[h] * ct_st[h] + a_news[h] * ct_loc_pair[p][j * A_DH:(j + 1) * A_DH, :]
            n_st[h] = a_olds[h] * n_st[h] + a_news[h] * jnp.sum(kas[h], axis=0, keepdims=True)
        houts.append(hrow)

    for h in range(A_HEADS):
        c_ref[:, hsl[h]] = ct_st[h]
        n_ref[0:1, hsl[h]] = n_st[h]
        m_ref[h:h + 1, :] = jnp.broadcast_to(m_st[h], (1, V7X_LANES))

    @pl.when(is_fwd)
    def _():
        for jj in range(cb):
            for h in range(A_HEADS):
                hfw_ref[blk * cb + chunks[jj], hsl[h], :] = houts[jj][h]

    @pl.when(jnp.logical_not(is_fwd))
    def _():
        for jj in range(cb):
            tsl = pl.ds(pl.multiple_of(chunks[jj] * CHUNK, CHUNK), CHUNK)
            for h in range(A_HEADS):
                tot = hfw_ref[blk * cb + chunks[jj], hsl[h], :] + houts[jj][h]
                tot = tot * lax.rsqrt(jnp.mean(tot * tot, axis=0, keepdims=True) + EPS)
                yh = tot.T * nw_ref[:, hsl[h]] * _sigmoid(o_ref[tsl, hsl[h]].astype(F32))
                y_ref[tsl, hsl[h]] = (yh * _silu(z_ref[tsl, hsl[h]].astype(F32))).astype(BF16)


def _mlstm_call(p3, g3, bg_pad, nw):
    t = _tiles()
    bsz, seq, _ = p3.shape
    nc = seq // CHUNK
    cb = min(MLSTM_BLOCK_CHUNKS, nc)
    assert nc % cb == 0
    nblk = nc // cb
    rows = cb * CHUNK

    def cur(s):
        return jnp.where(s < nblk, s, 2 * nblk - 1 - s)

    def late(s):
        return jnp.where(s < nblk, nblk - 1, 2 * nblk - 1 - s)

    def blk(col, which):
        return pl.BlockSpec((None, rows, BRANCH_W), lambda b, s: (b, which(s), col // BRANCH_W))

    return pl.pallas_call(
        functools.partial(_mlstm_kernel, nblk=nblk, cb=cb),
        grid=(bsz, 2 * nblk),
        in_specs=[
            blk(COL_AQ, cur), blk(COL_AK, cur), blk(COL_AV, cur),
            pl.BlockSpec((None, rows, V7X_LANES), lambda b, s: (b, cur(s), 0)),
            blk(COL_AO, late), blk(COL_AZ, late),
            pl.BlockSpec((1, V7X_LANES), lambda b, s: (0, 0)),
            pl.BlockSpec((1, BRANCH_W), lambda b, s: (0, 0)),
        ],
        out_specs=pl.BlockSpec((None, rows, BRANCH_W), lambda b, s: (b, late(s), 0)),
        out_shape=jax.ShapeDtypeStruct((bsz, seq, BRANCH_W), BF16),
        scratch_shapes=[
            pltpu.VMEM((nc, BRANCH_W, CHUNK), F32),
            pltpu.VMEM((A_DH, BRANCH_W), F32),
            pltpu.VMEM((V7X_SUBLANES, BRANCH_W), F32),
            pltpu.VMEM((V7X_SUBLANES, V7X_LANES), F32),
        ],
        compiler_params=pltpu.CompilerParams(
            dimension_semantics=("arbitrary", "arbitrary"), vmem_limit_bytes=t["vmem_big"]),
        name="mlstm",
    )(p3, p3, p3, g3, p3, p3, bg_pad, nw)


def _nattn_kernel(q_ref, kc_ref, kx_ref, vc_ref, vx_ref, z_ref, qw_ref, kw_ref, bias_ref, y_ref,
                  kn_ref, vn_ref, *, nblk):
    i = pl.program_id(1)

    def put(slot, k_src, v_src):
        dst = pl.ds(pl.multiple_of(slot * NA_BLOCK_TOK, NA_BLOCK_TOK), NA_BLOCK_TOK)
        for h in range(B_HEADS):
            hs = slice(h * B_DH, (h + 1) * B_DH)
            kn_ref[dst, hs] = _rms(k_src[:, hs].astype(F32), kw_ref[...]).astype(BF16)
        vn_ref[dst, :] = v_src[...].astype(BF16)

    @pl.when(i == 0)
    def _():
        put(0, kc_ref, vc_ref)

    nslot = (i + 1) % 3
    put(nslot, kx_ref, vx_ref)

    @pl.when((i == 0) | (nslot == 0))
    def _():
        mirror = slice(3 * NA_BLOCK_TOK, 4 * NA_BLOCK_TOK)
        kn_ref[mirror, :] = kn_ref[0:NA_BLOCK_TOK, :]
        vn_ref[mirror, :] = vn_ref[0:NA_BLOCK_TOK, :]

    qn = []
    for h in range(B_HEADS):
        hs = slice(h * B_DH, (h + 1) * B_DH)
        qn.append((_rms(q_ref[:, hs].astype(F32), qw_ref[...]) * (B_DH ** -0.5 * LOG2E)).astype(BF16))

    half = NA_BLOCK_TOK // 2
    eye = (lax.broadcasted_iota(jnp.int32, (B_DH, B_DH), 0)
           == lax.broadcasted_iota(jnp.int32, (B_DH, B_DH), 1)).astype(BF16)
    starts = (jnp.where(i == 0, 0, ((i + 2) % 3) * NA_BLOCK_TOK + half), (i % 3) * NA_BLOCK_TOK)
    variants = (jnp.where(i == 0, 1, 0), jnp.where(i == nblk - 1, 2, 0))
    for quad in range(NA_BLOCK_ROWS // NA_QUAD_ROWS):
        qrows = slice(quad * NA_QUAD_TOK, (quad + 1) * NA_QUAD_TOK)
        win = pl.ds(pl.multiple_of(starts[quad], half), NA_WIN_TOK)
        for h in range(B_HEADS):
            hs = slice(h * B_DH, (h + 1) * B_DH)
            sc = lax.dot_general(kn_ref[win, hs], qn[h][qrows, :], (((1,), (1,)), ((), ())),
                                 preferred_element_type=F32)
            sc = sc + bias_ref[variants[quad], h]
            m = jnp.max(sc, axis=0, keepdims=True)
            p = jnp.exp2(sc - m)
            l = jnp.sum(p, axis=0, keepdims=True)
            vt = lax.dot_general(eye, vn_ref[win, hs], (((1,), (1,)), ((), ())),
                                 preferred_element_type=F32).astype(BF16)
            o = (jnp.dot(vt, p.astype(BF16), preferred_element_type=F32) / l).T
            y_ref[qrows, hs] = (o * _silu(z_ref[qrows, hs].astype(F32))).astype(BF16)


def _nattn_call(p3, qw, kw, bias_quads):
    t = _tiles()
    bsz, seq, _ = p3.shape
    rows = seq // GRID_W
    assert rows >= WIN_ROWS and rows % NA_BLOCK_ROWS == 0
    nblk = rows // NA_BLOCK_ROWS

    def blk(col, shift):
        cb = col // BRANCH_W
        return pl.BlockSpec((None, NA_BLOCK_TOK, BRANCH_W),
                            lambda b, i: (b, jnp.clip(i + shift, 0, nblk - 1), cb))

    return pl.pallas_call(
        functools.partial(_nattn_kernel, nblk=nblk),
        grid=(bsz, nblk),
        in_specs=[
            blk(COL_BQ, 0),
            blk(COL_BK, 0), blk(COL_BK, 1),
            blk(COL_BV, 0), blk(COL_BV, 1),
            blk(COL_BZ, 0),
            pl.BlockSpec((1, B_DH), lambda b, i: (0, 0)),
            pl.BlockSpec((1, B_DH), lambda b, i: (0, 0)),
            pl.BlockSpec(bias_quads.shape, lambda b, i: (0, 0, 0, 0)),
        ],
        out_specs=pl.BlockSpec((None, NA_BLOCK_TOK, BRANCH_W), lambda b, i: (b, i, 0)),
        out_shape=jax.ShapeDtypeStruct((bsz, seq, BRANCH_W), BF16),
        scratch_shapes=[
            pltpu.VMEM((NA_RING_SLOTS * NA_BLOCK_TOK, BRANCH_W), BF16),
            pltpu.VMEM((NA_RING_SLOTS * NA_BLOCK_TOK, BRANCH_W), BF16),
        ],
        compiler_params=pltpu.CompilerParams(
            dimension_semantics=("arbitrary", "arbitrary"), vmem_limit_bytes=t["vmem_big"]),
        name="nattn",
    )(p3, p3, p3, p3, p3, p3, qw, kw, bias_quads)


def _skew(w, nq, first, ncol):
    width = w.shape[-1]
    assert first - (nq - 1) >= 0 and first + ncol <= width - 1
    lead = w.shape[:-1]
    flat = jnp.tile(w, (1,) * len(lead) + (nq,))[..., :nq * (width - 1)]
    return flat.reshape(lead + (nq, width - 1))[..., first:first + ncol]


def _nattn_bias(rpb):
    heads = rpb.shape[0]
    reach = GRID_W - WIN_COLS
    w = jnp.pad(rpb.astype(F32), ((0, 0), (0, 0), (reach, reach + 1)))
    tbl = _skew(w, GRID_W, GRID_W - 1, GRID_W)
    qc = np.arange(GRID_W)[:, None]
    kc = np.arange(GRID_W)[None, :]
    qs = np.clip(qc - WIN_COLS // 2, 0, GRID_W - WIN_COLS)
    ok = (kc >= qs) & (kc < qs + WIN_COLS)
    tbl = jnp.where(ok[None, None], tbl, -jnp.inf)
    per_d = [jnp.concatenate([tbl[:, d + j] for j in range(WIN_ROWS)], axis=-1) for d in range(WIN_ROWS)]
    fill = lambda n: jnp.full((heads, GRID_W, n), -jnp.inf, F32)

    def quad(ds, shifts):
        rows = [jnp.concatenate([fill(GRID_W * sh), per_d[d], fill(NA_WIN_TOK - NA_BLOCK_TOK - GRID_W * sh)],
                                axis=-1) for d, sh in zip(ds, shifts)]
        return jnp.concatenate(rows, axis=1)

    mid = WIN_ROWS - 1 - WIN_ROWS // 2
    interior = quad((mid,) * NA_QUAD_ROWS, range(NA_QUAD_ROWS))
    top = quad(range(WIN_ROWS - 1, WIN_ROWS - 1 - NA_QUAD_ROWS, -1), (0,) * NA_QUAD_ROWS)
    bottom = quad(range(NA_QUAD_ROWS - 1, -1, -1), (0,) * NA_QUAD_ROWS)
    return jnp.swapaxes(jnp.stack([interior, top, bottom]), 2, 3) * LOG2E


def _swa_kernel(q_ref, kp_ref, kc_ref, kx_ref, vp_ref, vc_ref, vx_ref, z_ref, qw_ref, kw_ref,
                sink_ref, bias_ref, y_ref, *, nsteps, nq):
    n = pl.program_id(1)
    band = 3 * C_BLOCK
    lane = lax.broadcasted_iota(jnp.int32, (1, V7X_LANES), 1)
    lo = lane < C_DH

    def pair_rms(x, w):
        x2 = x * x
        s_lo = jnp.sum(jnp.where(lo, x2, 0.0), axis=-1, keepdims=True)
        s_hi = jnp.sum(jnp.where(lo, 0.0, x2), axis=-1, keepdims=True)
        r = lax.rsqrt(jnp.where(lo, s_lo, s_hi) * (1.0 / C_DH) + EPS)
        return x * r * w

    kn = pair_rms(jnp.concatenate([kp_ref[...], kc_ref[...], kx_ref[...]], axis=0).astype(F32), kw_ref[...])
    vc = jnp.concatenate([vp_ref[...], vc_ref[...], vx_ref[...]], axis=0).astype(F32)
    kn_sw = pltpu.roll(kn, C_DH, axis=1)
    vc_sw = pltpu.roll(vc, C_DH, axis=1)
    stacks = []
    for g in range(C_KV_HEADS):
        k_lo, k_hi = (kn, kn_sw) if g == 0 else (kn_sw, kn)
        v_lo, v_hi = (vc, vc_sw) if g == 0 else (vc_sw, vc)
        stacks.append((jnp.where(lo, k_lo, 0.0).astype(BF16), jnp.where(lo, 0.0, k_hi).astype(BF16),
                       jnp.where(lo, v_lo, 0.0).astype(BF16), jnp.where(lo, 0.0, v_hi).astype(BF16)))

    ki = lax.broadcasted_iota(jnp.int32, (2 * band, 1), 0)
    ki = jnp.where(ki >= band, ki - band, ki)
    first_slab = lax.broadcasted_iota(jnp.int32, (1, 2 * C_BLOCK), 1) < C_BLOCK
    even_rows = lax.broadcasted_iota(jnp.int32, (V7X_LANES, 1), 0) < C_DH
    eye = (lax.broadcasted_iota(jnp.int32, (V7X_LANES, V7X_LANES), 0)
           == lax.broadcasted_iota(jnp.int32, (V7X_LANES, V7X_LANES), 1)).astype(BF16)
    for qb in range(nq):
        qrows = slice(qb * C_BLOCK, (qb + 1) * C_BLOCK)
        krows = slice((nq + qb - 1) * C_BLOCK, (nq + qb + 2) * C_BLOCK)
        has_prev = (n > 0) if qb == 0 else True
        has_next = (n < nsteps - 1) if qb == nq - 1 else True
        valid = ((ki >= C_BLOCK) | has_prev) & ((ki < 2 * C_BLOCK) | has_next)
        for g in range(C_KV_HEADS):
            k_even, k_odd, v_even, v_odd = stacks[g]
            kk = jnp.concatenate([k_even[krows], k_odd[krows]], axis=0)
            vv = jnp.concatenate([v_even[krows], v_odd[krows]], axis=0)
            slabs = [slice(p * V7X_LANES, (p + 1) * V7X_LANES) for p in (2 * g, 2 * g + 1)]
            qn = jnp.concatenate(
                [(pair_rms(q_ref[qrows, sl].astype(F32), qw_ref[...]) * (C_DH ** -0.5 * LOG2E)).astype(BF16)
                 for sl in slabs], axis=0)
            sc = lax.dot_general(kk, qn, (((1,), (1,)), ((), ())), preferred_element_type=F32)
            sc = jnp.where(valid, sc + bias_ref[g], -jnp.inf)
            probs, denoms = [], []
            for e in range(2):
                se = sc[e * band:(e + 1) * band, :]
                h0 = C_GROUP * g + e
                sink = jnp.where(first_slab, sink_ref[h0:h0 + 1, 0:1], sink_ref[h0 + 2:h0 + 3, 0:1])
                m = jnp.maximum(jnp.max(se, axis=0, keepdims=True), sink)
                pe = jnp.exp2(se - m)
                probs.append(pe.astype(BF16))
                denoms.append(jnp.sum(pe, axis=0, keepdims=True) + jnp.exp2(sink - m))
            vvt = lax.dot_general(eye, vv, (((1,), (1,)), ((), ())), preferred_element_type=F32).astype(BF16)
            ot = jnp.dot(vvt, jnp.concatenate(probs, axis=0), preferred_element_type=F32)
            o = (ot / jnp.where(even_rows, denoms[0], denoms[1])).T
            for j, sl in enumerate(slabs):
                rows = slice(j * C_BLOCK, (j + 1) * C_BLOCK)
                y_ref[qrows, sl] = (o[rows] * _silu(z_ref[qrows, sl].astype(F32))).astype(BF16)


def _swa_call(p3, qw, kw, sink_rows, bias):
    t = _tiles()
    bsz, seq, _ = p3.shape
    nq = SWA_STEP_BLOCKS
    rows = nq * C_BLOCK
    assert seq % rows == 0
    nsteps = seq // rows

    def wide(col):
        return pl.BlockSpec((None, rows, BRANCH_W), lambda b, n: (b, n, col // BRANCH_W))

    def kv(col, shift):
        return pl.BlockSpec((None, rows, C_KV_W),
                            lambda b, n: (b, jnp.clip(n + shift, 0, nsteps - 1), col // C_KV_W))

    return pl.pallas_call(
        functools.partial(_swa_kernel, nsteps=nsteps, nq=nq),
        grid=(bsz, nsteps),
        in_specs=[
            wide(COL_CQ),
            kv(COL_CK, -1), kv(COL_CK, 0), kv(COL_CK, 1),
            kv(COL_CV, -1), kv(COL_CV, 0), kv(COL_CV, 1),
            wide(COL_CZ),
            pl.BlockSpec((1, V7X_LANES), lambda b, n: (0, 0)),
            pl.BlockSpec((1, V7X_LANES), lambda b, n: (0, 0)),
            pl.BlockSpec(sink_rows.shape, lambda b, n: (0, 0)),
            pl.BlockSpec(bias.shape, lambda b, n: (0, 0, 0)),
        ],
        out_specs=pl.BlockSpec((None, rows, BRANCH_W), lambda b, n: (b, n, 0)),
        out_shape=jax.ShapeDtypeStruct((bsz, seq, BRANCH_W), BF16),
        compiler_params=pltpu.CompilerParams(
            dimension_semantics=("arbitrary", "arbitrary"), vmem_limit_bytes=t["vmem_small"]),
        name="swa",
    )(p3, p3, p3, p3, p3, p3, p3, p3, qw, kw, sink_rows, bias)


def _t5_bucket_rel():
    rel = np.arange(-(2 * C_BLOCK - 1), 2 * C_BLOCK)
    half = N_BUCKETS // 2
    max_exact = half // 2
    n = np.abs(rel)
    nf = np.maximum(n, 1).astype(np.float32)
    scale = np.float32(math.log(MAX_DIST / max_exact))
    large = max_exact + (np.log(nf / np.float32(max_exact)) / scale
                         * np.float32(half - max_exact)).astype(np.int32)
    large = np.minimum(large, half - 1)
    bucket = np.where(rel > 0, half, 0) + np.where(n < max_exact, n, large)
    return bucket, n <= C_WINDOW


def _swa_bias(rel_bias):
    bucket, in_window = _t5_bucket_rel()
    per_rel = jnp.where(in_window[:, None], rel_bias.astype(F32)[bucket], -jnp.inf)
    w = jnp.pad(per_rel.T, ((0, 0), (0, 1)))
    tbl = _skew(w, C_BLOCK, C_BLOCK - 1, 3 * C_BLOCK)
    slabs = jnp.concatenate([tbl[0::2], tbl[1::2]], axis=-1)
    return jnp.swapaxes(slabs.reshape(C_KV_HEADS, 2 * C_BLOCK, 2 * 3 * C_BLOCK), 1, 2) * LOG2E


def _conv_kernel(a_ref, g_ref, ap_ref, gp_ref, ax_ref, gx_ref, z_ref, cw_ref, cb_ref, lw_ref, lb_ref,
                 y_ref, ext_ref, sh_ref, *, ntile):
    i = pl.program_id(1)
    tt = a_ref.shape[0]
    glu = lambda a, g: a[...].astype(F32) * _sigmoid(g[...].astype(F32))
    prev = jnp.where(i > 0, glu(ap_ref, gp_ref), 0.0)
    nxt = jnp.where(i < ntile - 1, glu(ax_ref, gx_ref), 0.0)
    ext_ref[0:CONV_HALO, :] = prev
    ext_ref[CONV_HALO:CONV_HALO + tt, :] = glu(a_ref, g_ref)
    ext_ref[CONV_HALO + tt:2 * CONV_HALO + tt, :] = nxt
    span = sh_ref.shape[1]
    for r in range(1, V7X_SUBLANES):
        sh_ref[r - 1] = ext_ref[pl.ds(r, span), :]
    acc = None
    for w in range(CONV_W):
        a, r = divmod(CONV_HALO - CONV_W // 2 + w, V7X_SUBLANES)
        lo = a * V7X_SUBLANES
        src = ext_ref[lo:lo + tt, :] if r == 0 else sh_ref[r - 1, lo:lo + tt, :]
        term = src * cw_ref[w:w + 1, :]
        acc = term if acc is None else acc + term
    u = acc + cb_ref[...]
    mu = jnp.mean(u, axis=-1, keepdims=True)
    var = jnp.mean(jnp.square(u - mu), axis=-1, keepdims=True)
    u = (u - mu) * lax.rsqrt(var + EPS) * lw_ref[...] + lb_ref[...]
    y_ref[...] = (_silu(u) * _silu(z_ref[...].astype(F32))).astype(BF16)


def _conv_call(p3, cw_pad, cb, lw, lb):
    t = _tiles()
    bsz, seq, _ = p3.shape
    tt = CONV_TILE
    ntile = seq // tt
    per = tt // CONV_HALO
    nhalo = seq // CONV_HALO
    span = tt + (2 * CONV_HALO - 1) // V7X_SUBLANES * V7X_SUBLANES

    def cur(col):
        return pl.BlockSpec((None, tt, BRANCH_W), lambda b, i: (b, i, col // BRANCH_W))

    def halo(col, after):
        def idx(b, i):
            r = (i + 1) * per if after else i * per - 1
            return (b, jnp.clip(r, 0, nhalo - 1), col // BRANCH_W)
        return pl.BlockSpec((None, CONV_HALO, BRANCH_W), idx)

    vec = pl.BlockSpec((1, BRANCH_W), lambda b, i: (0, 0))
    return pl.pallas_call(
        functools.partial(_conv_kernel, ntile=ntile),
        grid=(bsz, ntile),
        in_specs=[
            cur(COL_DA), cur(COL_DG),
            halo(COL_DA, False), halo(COL_DG, False), halo(COL_DA, True), halo(COL_DG, True),
            cur(COL_DZ),
            pl.BlockSpec(cw_pad.shape, lambda b, i: (0, 0)),
            vec, vec, vec,
        ],
        out_specs=pl.BlockSpec((None, tt, BRANCH_W), lambda b, i: (b, i, 0)),
        out_shape=jax.ShapeDtypeStruct((bsz, seq, BRANCH_W), BF16),
        scratch_shapes=[
            pltpu.VMEM((tt + 2 * CONV_HALO, BRANCH_W), F32),
            pltpu.VMEM((V7X_SUBLANES - 1, span, BRANCH_W), F32),
        ],
        compiler_params=pltpu.CompilerParams(
            dimension_semantics=("arbitrary", "arbitrary"), vmem_limit_bytes=t["vmem_small"]),
        name="conv",
    )(p3, p3, p3, p3, p3, p3, p3, cw_pad, cb, lw, lb)


def _merge_kernel(h_ref, ya_ref, yb_ref, yc_ref, yd_ref, wma_ref, wmb_ref, wmc_ref, wmd_ref, wb_ref, o_ref):
    h = h_ref[...]
    merged = None
    branches = ((ya_ref, wma_ref), (yb_ref, wmb_ref), (yc_ref, wmc_ref), (yd_ref, wmd_ref))
    for b, (y_ref, wm_ref) in enumerate(branches):
        gate = _sigmoid(jnp.dot(h, wm_ref[...], preferred_element_type=F32))
        term = gate * jnp.dot(y_ref[...], wb_ref[b], preferred_element_type=F32)
        merged = term if merged is None else merged + term
    o_ref[...] = merged.astype(BF16)


def _merge_call(h2d, ys, w_all, w_branch, seq):
    t = _tiles()
    m, d = h2d.shape
    tm, tn = min(t["mg_tm"], seq), t["mg_tn"]
    yspec = pl.BlockSpec((tm, BRANCH_W), lambda i, j: (i, 0))
    return pl.pallas_call(
        _merge_kernel,
        grid=(m // tm, d // tn),
        in_specs=[
            pl.BlockSpec((tm, d), lambda i, j: (i, 0)),
            yspec, yspec, yspec, yspec,
            *[pl.BlockSpec((d, tn), functools.partial(lambda i, j, b: (0, (PACK_COLS + b * d) // tn + j), b=b))
              for b in range(N_BRANCH)],
            pl.BlockSpec((N_BRANCH, BRANCH_W, tn), lambda i, j: (0, 0, j)),
        ],
        out_specs=pl.BlockSpec((tm, tn), lambda i, j: (i, j)),
        out_shape=jax.ShapeDtypeStruct((m, d), BF16),
        compiler_params=pltpu.CompilerParams(
            dimension_semantics=("arbitrary", "arbitrary"), vmem_limit_bytes=t["vmem_big"]),
        name="merge",
    )(h2d, *ys, *([w_all] * N_BRANCH), w_branch)


def _outproj_kernel(x_ref, mg_ref, gate_ref, wo_ref, o_ref):
    o_ref[...] = x_ref[...] + gate_ref[...] * jnp.dot(mg_ref[...], wo_ref[...], preferred_element_type=F32)


def _outproj_call(x2d, merged, mod_rows, w_out, seq, first_row):
    t = _tiles()
    m, d = x2d.shape
    tm, tn = min(t["op_tm"], seq), t["op_tn"]
    row_of_tile = _row_of_tile(tm, seq, first_row)
    return pl.pallas_call(
        _outproj_kernel,
        grid=(m // tm, d // tn),
        in_specs=[
            pl.BlockSpec((tm, tn), lambda i, j: (i, j)),
            pl.BlockSpec((tm, d), lambda i, j: (i, 0)),
            pl.BlockSpec((None, 1, tn), lambda i, j: (row_of_tile(i) * 3 + 2, 0, j)),
            pl.BlockSpec((d, tn), lambda i, j: (0, j)),
        ],
        out_specs=pl.BlockSpec((tm, tn), lambda i, j: (i, j)),
        out_shape=jax.ShapeDtypeStruct((m, d), F32),
        compiler_params=pltpu.CompilerParams(
            dimension_semantics=("arbitrary", "arbitrary"), vmem_limit_bytes=t["vmem_big"]),
        name="outproj",
    )(x2d, merged, mod_rows, w_out)


REPACK_TILE = 256
GATE_SHIFT = A_GATES


def _repack_plan(d):
    sizes = (512, 512, 512, 512, 512, A_GATES, 512, 512, 512, 512, 512, C_KV_W, C_KV_W, 512, 1024, 512,
             N_BRANCH * d)
    names = ("aq", "ak", "av", "ao", "az", "ag", "bq", "bk", "bv", "bz", "cq", "ck", "cv", "cz", "dglu", "dz", "mg")
    src = dict(zip(names, np.concatenate([[0], np.cumsum(sizes)])[:-1]))
    groups = [
        (COL_AQ, src["aq"], 5 * BRANCH_W), (COL_BQ, src["bq"], 4 * BRANCH_W), (COL_DA, src["dglu"], 3 * BRANCH_W),
        (COL_CQ, src["cq"], BRANCH_W), (COL_CZ, src["cz"], BRANCH_W), (COL_CK, src["ck"], 2 * C_KV_W),
        (COL_AG, src["ag"], REPACK_TILE), (PACK_COLS, src["mg"], N_BRANCH * d)]
    ntile = (PACK_COLS + N_BRANCH * d) // REPACK_TILE
    a_idx, b_idx, mode = np.zeros(ntile, np.int32), np.zeros(ntile, np.int32), np.zeros(ntile, np.int32)
    for dst, s0, width in groups:
        for k in range(width // REPACK_TILE):
            t = dst // REPACK_TILE + k
            start = int(s0) + k * REPACK_TILE
            off = start % REPACK_TILE
            assert off in (0, GATE_SHIFT) and dst % REPACK_TILE == 0
            a_idx[t] = start // REPACK_TILE
            b_idx[t] = (start - off + REPACK_TILE) // V7X_LANES
            mode[t] = 2 if dst == COL_AG else (1 if off else 0)
    return jnp.asarray(a_idx), jnp.asarray(b_idx), jnp.asarray(mode)


def _repack_kernel(a_idx, b_idx, mode, a_ref, b_ref, o_ref):
    t = pl.program_id(0)
    a = a_ref[...]
    lane = lax.broadcasted_iota(jnp.int32, (1, REPACK_TILE), 1)
    shifted = jnp.concatenate([a[:, GATE_SHIFT:], b_ref[:, :GATE_SHIFT]], axis=1)
    gates = jnp.where(lane < A_GATES, a, 0.0)
    m = mode[t]
    o_ref[...] = jnp.where(m == 1, shifted, jnp.where(m == 2, gates, a)).astype(BF16)


def _repack_call(w_in, layer):
    t = _tiles()
    _, d, _ = w_in.shape
    a_idx, b_idx, mode = _repack_plan(d)
    ntile = a_idx.shape[0]
    return pl.pallas_call(
        _repack_kernel,
        grid_spec=pltpu.PrefetchScalarGridSpec(
            num_scalar_prefetch=3,
            grid=(ntile,),
            in_specs=[
                pl.BlockSpec((None, d, REPACK_TILE), lambda i, a, b, m: (layer, 0, a[i])),
                pl.BlockSpec((None, d, V7X_LANES), lambda i, a, b, m: (layer, 0, b[i])),
            ],
            out_specs=pl.BlockSpec((d, REPACK_TILE), lambda i, a, b, m: (0, i)),
        ),
        out_shape=jax.ShapeDtypeStruct((d, ntile * REPACK_TILE), BF16),
        compiler_params=pltpu.CompilerParams(
            dimension_semantics=("arbitrary",), vmem_limit_bytes=t["vmem_small"],
            allow_input_fusion=[False, False, False, True, True]),
        name="repack",
    )(a_idx, b_idx, mode, w_in, w_in)


def kernel(x_prompt, x_sample, c_prompt, c_sample, rel_bias, norm_w, w_ada, b_ada, w_in, b_gate, mlstm_norm_w, na_q_norm, na_k_norm, na_rpb, swa_q_norm, swa_k_norm, swa_sink, conv_w, conv_b, conv_ln_w, conv_ln_b, w_branch, w_out):
    depth, d = norm_w.shape
    groups = ((x_prompt, 0), (x_sample, c_prompt.shape[0]))
    n_cond = c_prompt.shape[0] + c_sample.shape[0]
    cond_rows = -(-n_cond // V7X_SUBLANES) * V7X_SUBLANES
    c_all = jnp.concatenate([c_prompt, c_sample, jnp.zeros((cond_rows - n_cond, d), F32)], axis=0)
    mod = _ada_call(c_all, w_ada, b_ada)
    mod_rows = mod.reshape(depth * cond_rows * 3, 1, d)

    swa_bias = _swa_bias(rel_bias)
    w_in_bf16 = w_in.astype(BF16)
    layers = []
    for l in range(depth):
        layers.append(dict(
            w_all=_repack_call(w_in_bf16, l),
            w_branch=w_branch[l].astype(BF16), w_out=w_out[l].astype(BF16),
            nw=norm_w[l].reshape(1, d),
            bg=jnp.pad(b_gate[l], (0, V7X_LANES - A_GATES)).reshape(1, V7X_LANES),
            mnw=mlstm_norm_w[l].reshape(1, BRANCH_W),
            na_qw=na_q_norm[l].reshape(1, B_DH), na_kw=na_k_norm[l].reshape(1, B_DH),
            na_bias=_nattn_bias(na_rpb[l]),
            swa_qw=jnp.tile(swa_q_norm[l], 2).reshape(1, V7X_LANES),
            swa_kw=jnp.tile(swa_k_norm[l], 2).reshape(1, V7X_LANES),
            sink=jnp.broadcast_to(swa_sink[l].reshape(C_HEADS, 1) * LOG2E, (C_HEADS, V7X_LANES)),
            cw=jnp.pad(conv_w[l], ((0, 1), (0, 0))),
            cb=conv_b[l].reshape(1, BRANCH_W), lw=conv_ln_w[l].reshape(1, BRANCH_W),
            lb=conv_ln_b[l].reshape(1, BRANCH_W),
        ))

    outs = []
    for x, cond_off in groups:
        bsz, seq, _ = x.shape
        x2d = x.reshape(bsz * seq, d)
        for l, lw in enumerate(layers):
            first_row = l * cond_rows + cond_off
            p, gates, h2d = _inproj_call(x2d, lw["nw"], mod_rows, lw["w_all"], seq, first_row)
            p3 = p.reshape(bsz, seq, PACK_COLS)
            ya = _mlstm_call(p3, gates.reshape(bsz, seq, V7X_LANES), lw["bg"], lw["mnw"])
            yb = _nattn_call(p3, lw["na_qw"], lw["na_kw"], lw["na_bias"])
            yc = _swa_call(p3, lw["swa_qw"], lw["swa_kw"], lw["sink"], swa_bias)
            yd = _conv_call(p3, lw["cw"], lw["cb"], lw["lw"], lw["lb"])
            ys = [y.reshape(bsz * seq, BRANCH_W) for y in (ya, yb, yc, yd)]
            merged = _merge_call(h2d, ys, lw["w_all"], lw["w_branch"], seq)
            x2d = _outproj_call(x2d, merged, mod_rows, lw["w_out"], seq, first_row)
        outs.append(x2d.reshape(bsz, seq, d))
    return tuple(outs)
```
